```python
import functools
import jax, jax.numpy as jnp
from jax import lax
import numpy as np

D_MODEL = 2048
BATCH = 4
SEQ = 8192
DEPTH = 4
DEC_BATCH = 8
DEC_SEQ = 16
PAST_LEN = 1024

CHUNK = 64
N_MIXERS = 2
A_HEADS = 16
A_HEAD_DIM = D_MODEL // A_HEADS
A_PREV_CHUNKS = 8
A_REL_CLIP = 128
B_WINDOW = 128
B_PREV_CHUNKS = B_WINDOW // CHUNK
B_HEAD_DIM = 64
B_Q_HEADS = D_MODEL // B_HEAD_DIM
B_KV_HEADS = 8
B_GROUP = B_Q_HEADS // B_KV_HEADS
D_FF = 4 * D_MODEL
RMS_EPS = 1e-6
NEG_INF = -1e30

kernel_name = 'chunk_streaming_hybrid_encoder_step'


def _n_layers_of(kind):
    return len([i for i in range(DEPTH) if i % N_MIXERS == kind])


def _rmsnorm(x, g):
    x32 = x.astype(jnp.float32)
    y = x32 * lax.rsqrt(jnp.mean(x32 * x32, axis=-1, keepdims=True) + RMS_EPS)
    return (y * g.astype(jnp.float32)).astype(x.dtype)


def _sq_relu_mlp(h, w_up, w_down):
    return jnp.square(jax.nn.relu(h @ w_up)) @ w_down


def _relpos_bias(table, rel):
    idx = jnp.clip(rel, -A_REL_CLIP, A_REL_CLIP) + A_REL_CLIP
    return table.astype(jnp.float32)[:, idx][:, None]


def _alibi_bias(rel):
    slopes = jnp.asarray(2.0 ** (-8.0 * np.arange(1, B_Q_HEADS + 1) / B_Q_HEADS), dtype=jnp.float32)
    b = -slopes[:, None, None] * jnp.abs(rel).astype(jnp.float32)[None]
    return b.reshape((B_KV_HEADS, B_GROUP) + rel.shape)


def _attend(q, k, v, bias, valid, sinks):
    scale = q.shape[-1] ** -0.5
    s = jnp.einsum('bqhgd,bkhd->bhgqk', q, k).astype(jnp.float32) * scale + bias
    s = jnp.where(valid, s, NEG_INF)
    if sinks is not None:
        sink_col = jnp.broadcast_to(sinks.astype(jnp.float32)[None, :, :, None, None], s.shape[:-1] + (1,))
        p = jax.nn.softmax(jnp.concatenate([s, sink_col], axis=-1), axis=-1)[..., :-1]
    else:
        p = jax.nn.softmax(s, axis=-1)
    o = jnp.einsum('bhgqk,bkhd->bqhgd', p.astype(v.dtype), v)
    return o.reshape(o.shape[0], o.shape[1], -1)


def _band_prompt(q, k, v, n_prev, bias_fn, sinks):
    b, s = q.shape[:2]
    pad = n_prev * CHUNK
    band = pad + CHUNK
    kp = jnp.pad(k, ((0, 0), (pad, 0), (0, 0), (0, 0)))
    vp = jnp.pad(v, ((0, 0), (pad, 0), (0, 0), (0, 0)))
    i = jnp.arange(CHUNK)[:, None]
    j = jnp.arange(band)[None, :]
    bias = bias_fn(j - pad - i)

    def one_chunk(c):
        start = c * CHUNK
        qc = lax.dynamic_slice_in_dim(q, start, CHUNK, axis=1)
        kc = lax.dynamic_slice_in_dim(kp, start, band, axis=1)
        vc = lax.dynamic_slice_in_dim(vp, start, band, axis=1)
        valid = j >= pad - start
        return _attend(qc, kc, vc, bias, valid, sinks)

    out = lax.map(one_chunk, jnp.arange(s // CHUNK))
    return out.transpose(1, 0, 2, 3).reshape(b, s, -1)


def _band_sample(q, k_new, v_new, k_cache, v_cache, bias_fn, sinks):
    cl = k_cache.shape[1]
    t = q.shape[1]
    k = jnp.concatenate([k_cache, k_new.astype(k_cache.dtype)], axis=1)
    v = jnp.concatenate([v_cache, v_new.astype(v_cache.dtype)], axis=1)
    qpos = PAST_LEN + jnp.arange(t)
    kpos = jnp.concatenate([PAST_LEN - cl + jnp.arange(cl), PAST_LEN + jnp.arange(t)])
    rel = kpos[None, :] - qpos[:, None]
    valid = jnp.ones(rel.shape, dtype=bool)
    out = _attend(q, k, v, bias_fn(rel), valid, sinks)
    return out, k[:, t:], v[:, t:]


def _qkv_a(h, w_qkv):
    b, s = h.shape[:2]
    q, k, v = jnp.split(h @ w_qkv, 3, axis=-1)
    return (q.reshape(b, s, A_HEADS, 1, A_HEAD_DIM),
            k.reshape(b, s, A_HEADS, A_HEAD_DIM),
            v.reshape(b, s, A_HEADS, A_HEAD_DIM))


def _qkv_b(h, w_qkv):
    b, s = h.shape[:2]
    qkv = h @ w_qkv
    qd = B_Q_HEADS * B_HEAD_DIM
    kd = B_KV_HEADS * B_HEAD_DIM
    return (qkv[..., :qd].reshape(b, s, B_KV_HEADS, B_GROUP, B_HEAD_DIM),
            qkv[..., qd:qd + kd].reshape(b, s, B_KV_HEADS, B_HEAD_DIM),
            qkv[..., qd + kd:].reshape(b, s, B_KV_HEADS, B_HEAD_DIM))


def setup_inputs(seed: int = 0) -> dict:
    key = jax.random.key(seed)
    ks = jax.random.split(key, 17)
    n_a = _n_layers_of(0)
    n_b = _n_layers_of(1)
    cl_a = min(A_PREV_CHUNKS * CHUNK, PAST_LEN)
    cl_b = min(B_PREV_CHUNKS * CHUNK, PAST_LEN)
    f32 = jnp.float32

    def nrm(k, shape, scale):
        return scale * jax.random.normal(k, shape, f32)

    qkv_a_w = 3 * A_HEADS * A_HEAD_DIM
    qkv_b_w = (B_Q_HEADS + 2 * B_KV_HEADS) * B_HEAD_DIM
    return {
        'x_prompt': nrm(ks[0], (BATCH, SEQ, D_MODEL), 1.0),
        'x_sample': nrm(ks[1], (DEC_BATCH, DEC_SEQ, D_MODEL), 1.0),
        'cache_a_k': nrm(ks[2], (n_a, DEC_BATCH, cl_a, A_HEADS, A_HEAD_DIM), 1.0),
        'cache_a_v': nrm(ks[3], (n_a, DEC_BATCH, cl_a, A_HEADS, A_HEAD_DIM), 1.0),
        'cache_b_k': nrm(ks[4], (n_b, DEC_BATCH, cl_b, B_KV_HEADS, B_HEAD_DIM), 1.0),
        'cache_b_v': nrm(ks[5], (n_b, DEC_BATCH, cl_b, B_KV_HEADS, B_HEAD_DIM), 1.0),
        'norm_mix': 1.0 + nrm(ks[6], (DEPTH, D_MODEL), 0.02),
        'norm_ffn': 1.0 + nrm(ks[7], (DEPTH, D_MODEL), 0.02),
        'norm_final': 1.0 + nrm(ks[8], (D_MODEL,), 0.02),
        'a_w_qkv': nrm(ks[9], (n_a, D_MODEL, qkv_a_w), D_MODEL ** -0.5),
        'a_w_o': nrm(ks[10], (n_a, A_HEADS * A_HEAD_DIM, D_MODEL), (A_HEADS * A_HEAD_DIM) ** -0.5),
        'a_rel_bias': nrm(ks[11], (n_a, A_HEADS, 2 * A_REL_CLIP + 1), 0.5),
        'b_w_qkv': nrm(ks[12], (n_b, D_MODEL, qkv_b_w), D_MODEL ** -0.5),
        'b_w_o': nrm(ks[13], (n_b, B_Q_HEADS * B_HEAD_DIM, D_MODEL), (B_Q_HEADS * B_HEAD_DIM) ** -0.5),
        'b_sinks': nrm(ks[14], (n_b, B_Q_HEADS), 1.0),
        'w_up': nrm(ks[15], (DEPTH, D_MODEL, D_FF), D_MODEL ** -0.5),
        'w_down': nrm(ks[16], (DEPTH, D_FF, D_MODEL), D_FF ** -0.5),
    }


def reference(x_prompt, x_sample, cache_a_k, cache_a_v, cache_b_k, cache_b_v,
              norm_mix, norm_ffn, norm_final, a_w_qkv, a_w_o, a_rel_bias,
              b_w_qkv, b_w_o, b_sinks, w_up, w_down):
    xp, xs = x_prompt, x_sample
    s_len = xp.shape[1]
    a_kp, a_vp, a_ks, a_vs = [], [], [], []
    b_kp, b_vp, b_ks, b_vs = [], [], [], []
    for layer in range(DEPTH):
        slot = layer // N_MIXERS
        hp = _rmsnorm(xp, norm_mix[layer])
        hs = _rmsnorm(xs, norm_mix[layer])
        if layer % N_MIXERS == 0:
            bias_fn = functools.partial(_relpos_bias, a_rel_bias[slot])
            qp, kp, vp = _qkv_a(hp, a_w_qkv[slot])
            qs, kn, vn = _qkv_a(hs, a_w_qkv[slot])
            mp = _band_prompt(qp, kp, vp, A_PREV_CHUNKS, bias_fn, None)
            ms, nk, nv = _band_sample(qs, kn, vn, cache_a_k[slot], cache_a_v[slot], bias_fn, None)
            cl = min(A_PREV_CHUNKS * CHUNK, s_len)
            a_kp.append(kp[:, s_len - cl:])
            a_vp.append(vp[:, s_len - cl:])
            a_ks.append(nk)
            a_vs.append(nv)
            xp = xp + mp @ a_w_o[slot]
            xs = xs + ms @ a_w_o[slot]
        else:
            sinks = b_sinks[slot].reshape(B_KV_HEADS, B_GROUP)
            qp, kp, vp = _qkv_b(hp, b_w_qkv[slot])
            qs, kn, vn = _qkv_b(hs, b_w_qkv[slot])
            mp = _band_prompt(qp, kp, vp, B_PREV_CHUNKS, _alibi_bias, sinks)
            ms, nk, nv = _band_sample(qs, kn, vn, cache_b_k[slot], cache_b_v[slot], _alibi_bias, sinks)
            cl = min(B_PREV_CHUNKS * CHUNK, s_len)
            b_kp.append(kp[:, s_len - cl:])
            b_vp.append(vp[:, s_len - cl:])
            b_ks.append(nk)
            b_vs.append(nv)
            xp = xp + mp @ b_w_o[slot]
            xs = xs + ms @ b_w_o[slot]
        xp = xp + _sq_relu_mlp(_rmsnorm(xp, norm_ffn[layer]), w_up[layer], w_down[layer])
        xs = xs + _sq_relu_mlp(_rmsnorm(xs, norm_ffn[layer]), w_up[layer], w_down[layer])
    y_prompt = _rmsnorm(xp, norm_final)
    y_sample = _rmsnorm(xs, norm_final)
    return (y_prompt, y_sample,
            jnp.stack(a_kp), jnp.stack(a_vp), jnp.stack(b_kp), jnp.stack(b_vp),
            jnp.stack(a_ks), jnp.stack(a_vs), jnp.stack(b_ks), jnp.stack(b_vs))
```

```python
import functools

import jax
import jax.numpy as jnp
import numpy as np
from jax import lax
from jax.experimental import pallas as pl
from jax.experimental.pallas import tpu as pltpu

F32 = jnp.float32
BF16 = jnp.bfloat16

CHUNK = 64
N_MIXERS = 2
A_HEAD_DIM = 128
A_PREV_CHUNKS = 8
A_REL_CLIP = 128
B_PREV_CHUNKS = 2
B_HEAD_DIM = 64
B_KV_HEADS = 8
RMS_EPS = 1e-6
NEG_INF = -1e30

LANES = 128
VMEM_LIMIT_BYTES = 56 * 1024 * 1024

A_TQ = 256
A_HEADS_PER_STEP = 4
B_TQ = 128
NEW_KEY_PAD = 128


def _rms_scale(x):
    return x * lax.rsqrt(jnp.mean(x * x, axis=-1, keepdims=True) + RMS_EPS)


def _dot_nt(a, b):
    return lax.dot_general(a, b, (((1,), (1,)), ((), ())), preferred_element_type=F32)


def _softmax_pv(s_blocks, v_blocks, sink):
    m = functools.reduce(jnp.maximum, [s.max(axis=-1, keepdims=True) for s in s_blocks])
    if sink is not None:
        m = jnp.maximum(m, sink)
    ps = [jnp.exp(s - m) for s in s_blocks]
    l = functools.reduce(jnp.add, [p.sum(axis=-1, keepdims=True) for p in ps])
    if sink is not None:
        l = l + jnp.exp(sink - m)
    o = functools.reduce(
        jnp.add,
        [jnp.dot(p.astype(BF16), v, preferred_element_type=F32) for p, v in zip(ps, v_blocks)])
    return o / l


def _norm_qkv_kernel(x_ref, g_ref, w_ref, qkv_ref, kst_ref, vst_ref, h_ref, *,
                     nq, nk, q_scale, tiles_per_batch, tail_rows):
    m = pl.program_id(0)
    n = pl.program_id(1)

    @pl.when(n == 0)
    def _():
        h_ref[...] = (_rms_scale(x_ref[...]) * g_ref[...]).astype(BF16)

    acc = jnp.dot(h_ref[...], w_ref[...], preferred_element_type=F32)
    qkv_ref[...] = (acc * jnp.where(n < nq, q_scale, 1.0)).astype(BF16)

    tm = acc.shape[0]
    is_tail = (m % tiles_per_batch) == tiles_per_batch - 1

    @pl.when(is_tail & (n >= nq) & (n < nq + nk))
    def _():
        kst_ref[0] = acc[tm - tail_rows:, :]

    @pl.when(is_tail & (n >= nq + nk))
    def _():
        vst_ref[0] = acc[tm - tail_rows:, :]


def _norm_qkv(x, g, w, *, tm, tn, q_width, kv_width, q_scale, tiles_per_batch, tail_rows):
    rows, d = x.shape
    n_total = w.shape[1]
    nq = q_width // tn
    nk = kv_width // tn
    nb = rows // (tm * tiles_per_batch)

    def state_map(first):
        def index_map(m, n):
            tail = (m % tiles_per_batch) == tiles_per_batch - 1
            j = jnp.clip(n - first, 0, nk - 1)
            return (m // tiles_per_batch, 0, jnp.where(tail, j, 0))
        return index_map

    kernel = functools.partial(_norm_qkv_kernel, nq=nq, nk=nk, q_scale=q_scale,
                               tiles_per_batch=tiles_per_batch, tail_rows=tail_rows)
    return pl.pallas_call(
        kernel,
        grid=(rows // tm, n_total // tn),
        in_specs=[
            pl.BlockSpec((tm, d), lambda m, n: (m, 0)),
            pl.BlockSpec((1, d), lambda m, n: (0, 0)),
            pl.BlockSpec((d, tn), lambda m, n: (0, n)),
        ],
        out_specs=[
            pl.BlockSpec((tm, tn), lambda m, n: (m, n)),
            pl.BlockSpec((1, tail_rows, tn), state_map(nq)),
            pl.BlockSpec((1, tail_rows, tn), state_map(nq + nk)),
        ],
        out_shape=[
            jax.ShapeDtypeStruct((rows, n_total), BF16),
            jax.ShapeDtypeStruct((nb, tail_rows, kv_width), F32),
            jax.ShapeDtypeStruct((nb, tail_rows, kv_width), F32),
        ],
        scratch_shapes=[pltpu.VMEM((tm, d), BF16)],
        compiler_params=pltpu.CompilerParams(
            dimension_semantics=("arbitrary", "arbitrary"),
            vmem_limit_bytes=VMEM_LIMIT_BYTES),
        name="norm_qkv",
    )(x, g, w)


def _attn_a_prompt_kernel(q_ref, k0_ref, k1_ref, k2_ref, v0_ref, v1_ref, v2_ref,
                          bias_ref, o_ref):
    i = pl.program_id(2)
    tq = q_ref.shape[0]
    pens = (jnp.where(i >= 2, 0.0, NEG_INF), jnp.where(i >= 1, 0.0, NEG_INF), None)
    k_refs = (k0_ref, k1_ref, k2_ref)
    v_refs = (v0_ref, v1_ref, v2_ref)
    for h in range(q_ref.shape[1] // A_HEAD_DIM):
        sl = slice(h * A_HEAD_DIM, (h + 1) * A_HEAD_DIM)
        q = q_ref[:, sl]
        s_blocks = []
        for j in range(3):
            s = _dot_nt(q, k_refs[j][:, sl]) + bias_ref[h, :, j * tq:(j + 1) * tq]
            if pens[j] is not None:
                s = s + pens[j]
            s_blocks.append(s)
        o = _softmax_pv(s_blocks, [v[:, sl] for v in v_refs], None)
        o_ref[:, sl] = o.astype(BF16)


def _attn_a_prompt(qkv, bias, *, batch, seq, d_model):
    tq = A_TQ
    hw = A_HEADS_PER_STEP * A_HEAD_DIM
    nhb = d_model // hw
    nqb = seq // tq

    def kv_spec(back, col0):
        return pl.BlockSpec(
            (tq, hw), lambda b, h, i: (b * nqb + jnp.maximum(i - back, 0), col0 + h))

    return pl.pallas_call(
        _attn_a_prompt_kernel,
        grid=(batch, nhb, nqb),
        in_specs=[
            pl.BlockSpec((tq, hw), lambda b, h, i: (b * nqb + i, h)),
            kv_spec(2, nhb), kv_spec(1, nhb), kv_spec(0, nhb),
            kv_spec(2, 2 * nhb), kv_spec(1, 2 * nhb), kv_spec(0, 2 * nhb),
            pl.BlockSpec((A_HEADS_PER_STEP, tq, 3 * tq), lambda b, h, i: (h, 0, 0)),
        ],
        out_specs=pl.BlockSpec((tq, hw), lambda b, h, i: (b * nqb + i, h)),
        out_shape=jax.ShapeDtypeStruct((batch * seq, d_model), BF16),
        compiler_params=pltpu.CompilerParams(
            dimension_semantics=("parallel", "parallel", "parallel"),
            vmem_limit_bytes=VMEM_LIMIT_BYTES),
        name="attn_a_prompt",
    )(qkv, qkv, qkv, qkv, qkv, qkv, qkv, bias)


def _attn_a_sample_kernel(q_ref, kc_ref, vc_ref, kn_ref, vn_ref, bias_ref,
                          o_ref, ko_ref, vo_ref):
    t = q_ref.shape[0]
    cl = kc_ref.shape[1]
    kc, vc, kn, vn = kc_ref[0], vc_ref[0], kn_ref[0], vn_ref[0]
    ko_ref[0, :cl - t, :] = kc[t:, :]
    ko_ref[0, cl - t:, :] = kn
    vo_ref[0, :cl - t, :] = vc[t:, :]
    vo_ref[0, cl - t:, :] = vn
    pad = jnp.zeros((NEW_KEY_PAD - t, A_HEAD_DIM), BF16)
    for h in range(q_ref.shape[1] // A_HEAD_DIM):
        sl = slice(h * A_HEAD_DIM, (h + 1) * A_HEAD_DIM)
        q = q_ref[:, sl]
        k_new = jnp.concatenate([kn[:, sl].astype(BF16), pad], axis=0)
        v_new = jnp.concatenate([vn[:, sl].astype(BF16), pad], axis=0)
        s_cache = _dot_nt(q, kc[:, sl].astype(BF16)) + bias_ref[h, :, :cl]
        s_new = _dot_nt(q, k_new) + bias_ref[h, :, cl:]
        o = _softmax_pv([s_cache, s_new], [vc[:, sl].astype(BF16), v_new], None)
        o_ref[:, sl] = o.astype(BF16)


def _attn_a_sample(qkv, k_new, v_new, cache_k, cache_v, bias, *, d_model):
    nb, cl, _ = cache_k.shape
    t = k_new.shape[1]
    hw = A_HEADS_PER_STEP * A_HEAD_DIM
    nhb = d_model // hw
    cache_spec = pl.BlockSpec((1, cl, hw), lambda b, h: (b, 0, h))
    new_spec = pl.BlockSpec((1, t, hw), lambda b, h: (b, 0, h))
    return pl.pallas_call(
        _attn_a_sample_kernel,
        grid=(nb, nhb),
        in_specs=[
            pl.BlockSpec((t, hw), lambda b, h: (b, h)),
            cache_spec, cache_spec, new_spec, new_spec,
            pl.BlockSpec((A_HEADS_PER_STEP, t, cl + NEW_KEY_PAD), lambda b, h: (h, 0, 0)),
        ],
        out_specs=[pl.BlockSpec((t, hw), lambda b, h: (b, h)), cache_spec, cache_spec],
        out_shape=[
            jax.ShapeDtypeStruct((nb * t, d_model), BF16),
            jax.ShapeDtypeStruct(cache_k.shape, F32),
            jax.ShapeDtypeStruct(cache_v.shape, F32),
        ],
        compiler_params=pltpu.CompilerParams(
            dimension_semantics=("parallel", "parallel"),
            vmem_limit_bytes=VMEM_LIMIT_BYTES),
        name="attn_a_sample",
    )(qkv, cache_k, cache_v, k_new, v_new, bias)


def _attn_b_heads(q_ref, k_blocks, v_blocks, bias_ref, sink_ref, o_ref, pens):
    tq = q_ref.shape[0]
    low = lax.broadcasted_iota(jnp.int32, (tq, LANES), 1) < B_HEAD_DIM
    blocks_per_pair = q_ref.shape[1] // LANES // (B_KV_HEADS // 2)
    for cb in range(q_ref.shape[1] // LANES):
        pair = cb // blocks_per_pair
        pair_sl = slice(pair * LANES, (pair + 1) * LANES)
        q_col = q_ref[:, cb * LANES:(cb + 1) * LANES]
        ks = [kb(pair_sl) for kb in k_blocks]
        vs = [vb(pair_sl) for vb in v_blocks]
        outs = []
        for side in range(2):
            head = 2 * cb + side
            q_half = jnp.where(low if side == 0 else jnp.logical_not(low),
                               q_col, jnp.zeros_like(q_col))
            s_blocks = []
            for j, k in enumerate(ks):
                nk = k.shape[0]
                s = _dot_nt(q_half, k) + bias_ref[head, :, j * nk:(j + 1) * nk]
                if pens[j] is not None:
                    s = s + pens[j]
                s_blocks.append(s)
            outs.append(_softmax_pv(s_blocks, vs, sink_ref[0, head]))
        o_ref[:, cb * LANES:(cb + 1) * LANES] = jnp.where(low, outs[0], outs[1]).astype(BF16)


def _attn_b_prompt_kernel(q_ref, k0_ref, k1_ref, v0_ref, v1_ref, bias_ref, sink_ref, o_ref):
    i = pl.program_id(1)
    pens = (jnp.where(i >= 1, 0.0, NEG_INF), None)
    _attn_b_heads(
        q_ref,
        [lambda sl: k0_ref[:, sl], lambda sl: k1_ref[:, sl]],
        [lambda sl: v0_ref[:, sl], lambda sl: v1_ref[:, sl]],
        bias_ref, sink_ref, o_ref, pens)


def _attn_b_prompt(qkv, bias, sinks, *, batch, seq, d_model, kv_width):
    tq = B_TQ
    nqb = seq // tq
    nq_cols = d_model // kv_width

    def kv_spec(back, col):
        return pl.BlockSpec(
            (tq, kv_width), lambda b, i: (b * nqb + jnp.maximum(i - back, 0), col))

    n_heads = bias.shape[0]
    return pl.pallas_call(
        _attn_b_prompt_kernel,
        grid=(batch, nqb),
        in_specs=[
            pl.BlockSpec((tq, d_model), lambda b, i: (b * nqb + i, 0)),
            kv_spec(1, nq_cols), kv_spec(0, nq_cols),
            kv_spec(1, nq_cols + 1), kv_spec(0, nq_cols + 1),
            pl.BlockSpec((n_heads, tq, 2 * tq), lambda b, i: (0, 0, 0)),
            pl.BlockSpec(memory_space=pltpu.SMEM),
        ],
        out_specs=pl.BlockSpec((tq, d_model), lambda b, i: (b * nqb + i, 0)),
        out_shape=jax.ShapeDtypeStruct((batch * seq, d_model), BF16),
        compiler_params=pltpu.CompilerParams(
            dimension_semantics=("parallel", "parallel"),
            vmem_limit_bytes=VMEM_LIMIT_BYTES),
        name="attn_b_prompt",
    )(qkv, qkv, qkv, qkv, qkv, bias, sinks)


def _attn_b_sample_kernel(q_ref, kc_ref, vc_ref, kn_ref, vn_ref, bias_ref, sink_ref,
                          o_ref, ko_ref, vo_ref):
    t = q_ref.shape[0]
    cl = kc_ref.shape[1]
    kc, vc, kn, vn = kc_ref[0], vc_ref[0], kn_ref[0], vn_ref[0]
    ko_ref[0, :cl - t, :] = kc[t:, :]
    ko_ref[0, cl - t:, :] = kn
    vo_ref[0, :cl - t, :] = vc[t:, :]
    vo_ref[0, cl - t:, :] = vn
    pad = jnp.zeros((NEW_KEY_PAD - t, LANES), BF16)
    _attn_b_heads(
        q_ref,
        [lambda sl: kc[:, sl].astype(BF16),
         lambda sl: jnp.concatenate([kn[:, sl].astype(BF16), pad], axis=0)],
        [lambda sl: vc[:, sl].astype(BF16),
         lambda sl: jnp.concatenate([vn[:, sl].astype(BF16), pad], axis=0)],
        bias_ref, sink_ref, o_ref, (None, None))


def _attn_b_sample(qkv, k_new, v_new, cache_k, cache_v, bias, sinks, *, d_model):
    nb, cl, kv_width = cache_k.shape
    t = k_new.shape[1]
    n_heads = bias.shape[0]
    cache_spec = pl.BlockSpec((1, cl, kv_width), lambda b: (b, 0, 0))
    new_spec = pl.BlockSpec((1, t, kv_width), lambda b: (b, 0, 0))
    return pl.pallas_call(
        _attn_b_sample_kernel,
        grid=(nb,),
        in_specs=[
            pl.BlockSpec((t, d_model), lambda b: (b, 0)),
            cache_spec, cache_spec, new_spec, new_spec,
            pl.BlockSpec((n_heads, t, cl + NEW_KEY_PAD), lambda b: (0, 0, 0)),
            pl.BlockSpec(memory_space=pltpu.SMEM),
        ],
        out_specs=[pl.BlockSpec((t, d_model), lambda b: (b, 0)), cache_spec, cache_spec],
        out_shape=[
            jax.ShapeDtypeStruct((nb * t, d_model), BF16),
            jax.ShapeDtypeStruct(cache_k.shape, F32),
            jax.ShapeDtypeStruct(cache_v.shape, F32),
        ],
        compiler_params=pltpu.CompilerParams(
            dimension_semantics=("parallel",),
            vmem_limit_bytes=VMEM_LIMIT_BYTES),
        name="attn_b_sample",
    )(qkv, cache_k, cache_v, k_new, v_new, bias, sinks)


def _oproj_mlp_kernel(x_ref, a_ref, wo_ref, g_ref, wup_ref, wdn_ref, gfin_ref,
                      o_ref, h_ref, *, final_norm):
    f = pl.program_id(1)

    @pl.when(f == 0)
    def _():
        x1 = x_ref[...] + jnp.dot(a_ref[...], wo_ref[...], preferred_element_type=F32)
        o_ref[...] = x1
        h_ref[...] = (_rms_scale(x1) * g_ref[...]).astype(BF16)

    u = jnp.dot(h_ref[...], wup_ref[...], preferred_element_type=F32)
    act = jnp.square(jnp.maximum(u, 0.0)).astype(BF16)
    o_ref[...] += jnp.dot(act, wdn_ref[...], preferred_element_type=F32)

    if final_norm:
        @pl.when(f == pl.num_programs(1) - 1)
        def _():
            o_ref[...] = _rms_scale(o_ref[...]) * gfin_ref[...]


def _oproj_mlp(x, attn, wo, g, wup, wdn, gfin, *, tm, tf, final_norm):
    rows, d = x.shape
    d_ff = wup.shape[1]
    kernel = functools.partial(_oproj_mlp_kernel, final_norm=final_norm)
    return pl.pallas_call(
        kernel,
        grid=(rows // tm, d_ff // tf),
        in_specs=[
            pl.BlockSpec((tm, d), lambda m, f: (m, 0)),
            pl.BlockSpec((tm, d), lambda m, f: (m, 0)),
            pl.BlockSpec((d, d), lambda m, f: (0, 0), pipeline_mode=pl.Buffered(1)),
            pl.BlockSpec((1, d), lambda m, f: (0, 0)),
            pl.BlockSpec((d, tf), lambda m, f: (0, f)),
            pl.BlockSpec((tf, d), lambda m, f: (f, 0)),
            pl.BlockSpec((1, d), lambda m, f: (0, 0)),
        ],
        out_specs=pl.BlockSpec((tm, d), lambda m, f: (m, 0)),
        out_shape=jax.ShapeDtypeStruct((rows, d), F32),
        scratch_shapes=[pltpu.VMEM((tm, d), BF16)],
        compiler_params=pltpu.CompilerParams(
            dimension_semantics=("parallel", "arbitrary"),
            vmem_limit_bytes=VMEM_LIMIT_BYTES),
        name="oproj_mlp",
    )(x, attn, wo, g, wup, wdn, gfin)


def _band_mask(n_q_chunks, n_k_chunks, n_prev):
    ci = np.arange(n_q_chunks * CHUNK)[:, None] // CHUNK
    cj = np.arange(n_k_chunks * CHUNK)[None, :] // CHUNK
    return (cj >= ci) & (cj <= ci + n_prev)


def _a_prompt_bias(table):
    qi = np.arange(A_TQ)[:, None]
    kj = np.arange(3 * A_TQ)[None, :]
    idx = np.clip(kj - 2 * A_TQ - qi, -A_REL_CLIP, A_REL_CLIP) + A_REL_CLIP
    mask = _band_mask(A_TQ // CHUNK, 3 * A_TQ // CHUNK, A_PREV_CHUNKS)
    return jnp.where(mask[None], table[:, idx], NEG_INF)


def _a_sample_bias(table, t, cl):
    qi = np.arange(t)[:, None]
    kj = np.arange(cl + NEW_KEY_PAD)[None, :]
    idx = np.clip(kj - cl - qi, -A_REL_CLIP, A_REL_CLIP) + A_REL_CLIP
    valid = np.broadcast_to(kj < cl + t, idx.shape)
    return jnp.where(valid[None], table[:, idx], NEG_INF)


def _b_head_order(n_q_heads):
    group = n_q_heads // B_KV_HEADS
    order = []
    for pair in range(B_KV_HEADS // 2):
        for g in range(group):
            for side in range(2):
                order.append(group * (2 * pair + side) + g)
    return np.asarray(order)


def _alibi_slopes(n_q_heads, order):
    slopes = 2.0 ** (-8.0 * np.arange(1, n_q_heads + 1) / n_q_heads)
    return slopes[order].astype(np.float32)


def _b_prompt_bias(n_q_heads, order):
    qi = np.arange(B_TQ)[:, None]
    kj = np.arange(2 * B_TQ)[None, :]
    rel = np.abs(kj - B_TQ - qi).astype(np.float32)
    bias = -_alibi_slopes(n_q_heads, order)[:, None, None] * rel[None]
    mask = _band_mask(B_TQ // CHUNK, 2 * B_TQ // CHUNK, B_PREV_CHUNKS)
    return np.where(mask[None], bias, np.float32(NEG_INF)).astype(np.float32)


def _b_sample_bias(n_q_heads, order, t, cl):
    qi = np.arange(t)[:, None]
    kj = np.arange(cl + NEW_KEY_PAD)[None, :]
    rel = np.abs(kj - cl - qi).astype(np.float32)
    bias = -_alibi_slopes(n_q_heads, order)[:, None, None] * rel[None]
    valid = np.broadcast_to(kj < cl + t, rel.shape)
    return np.where(valid[None], bias, np.float32(NEG_INF)).astype(np.float32)


def kernel(x_prompt, x_sample, cache_a_k, cache_a_v, cache_b_k, cache_b_v, norm_mix, norm_ffn,
           norm_final, a_w_qkv, a_w_o, a_rel_bias, b_w_qkv, b_w_o, b_sinks, w_up, w_down):
    batch, seq, d = x_prompt.shape
    dec_batch, dec_seq, _ = x_sample.shape
    depth = norm_mix.shape[0]
    a_kv_width = cache_a_k.shape[3] * cache_a_k.shape[4]
    b_kv_width = cache_b_k.shape[3] * cache_b_k.shape[4]
    cl_a, cl_b = cache_a_k.shape[2], cache_b_k.shape[2]
    n_b_heads = d // B_HEAD_DIM
    tail_a = min(A_PREV_CHUNKS * CHUNK, seq)
    tail_b = min(B_PREV_CHUNKS * CHUNK, seq)

    tm = 512
    tf = 1024
    rows_s = dec_batch * dec_seq

    order = _b_head_order(n_b_heads)
    q_cols = (order[:, None] * B_HEAD_DIM + np.arange(B_HEAD_DIM)[None, :]).reshape(-1)
    b_cols = np.concatenate([q_cols, np.arange(d, d + 2 * b_kv_width)])
    a_w_qkv16 = a_w_qkv.astype(BF16)
    a_w_o16 = a_w_o.astype(BF16)
    b_w_qkv16 = b_w_qkv[:, :, b_cols].astype(BF16)
    b_w_o16 = b_w_o[:, q_cols, :].astype(BF16)
    w_up16 = w_up.astype(BF16)
    w_down16 = w_down.astype(BF16)
    sinks_perm = b_sinks[:, order]

    b_bias_prompt = jnp.asarray(_b_prompt_bias(n_b_heads, order))
    b_bias_sample = jnp.asarray(_b_sample_bias(n_b_heads, order, dec_seq, cl_b))

    xp = x_prompt.reshape(batch * seq, d)
    xs = x_sample.reshape(rows_s, d)
    gfin = norm_final.reshape(1, d)

    states = {k: [] for k in ("a_kp", "a_vp", "a_ks", "a_vs", "b_kp", "b_vp", "b_ks", "b_vs")}
    for layer in range(depth):
        slot = layer // N_MIXERS
        g_mix = norm_mix[layer].reshape(1, d)
        g_ffn = norm_ffn[layer].reshape(1, d)
        if layer % N_MIXERS == 0:
            qkv_args = dict(tn=1024, q_width=d, kv_width=a_kv_width, q_scale=A_HEAD_DIM ** -0.5)
            qkv_p, kp, vp = _norm_qkv(xp, g_mix, a_w_qkv16[slot], tm=tm,
                                      tiles_per_batch=seq // tm, tail_rows=tail_a, **qkv_args)
            qkv_s, kn, vn = _norm_qkv(xs, g_mix, a_w_qkv16[slot], tm=rows_s,
                                      tiles_per_batch=1, tail_rows=rows_s, **qkv_args)
            table = a_rel_bias[slot]
            mp = _attn_a_prompt(qkv_p, _a_prompt_bias(table), batch=batch, seq=seq, d_model=d)
            ms, ks, vs = _attn_a_sample(
                qkv_s, kn.reshape(dec_batch, dec_seq, a_kv_width),
                vn.reshape(dec_batch, dec_seq, a_kv_width),
                cache_a_k[slot].reshape(dec_batch, cl_a, a_kv_width),
                cache_a_v[slot].reshape(dec_batch, cl_a, a_kv_width),
                _a_sample_bias(table, dec_seq, cl_a), d_model=d)
            w_o = a_w_o16[slot]
            states["a_kp"].append(kp.reshape((batch, tail_a) + cache_a_k.shape[3:]))
            states["a_vp"].append(vp.reshape((batch, tail_a) + cache_a_k.shape[3:]))
            states["a_ks"].append(ks.reshape(cache_a_k.shape[1:]))
            states["a_vs"].append(vs.reshape(cache_a_k.shape[1:]))
        else:
            qkv_args = dict(tn=b_kv_width, q_width=d, kv_width=b_kv_width,
                            q_scale=B_HEAD_DIM ** -0.5)
            qkv_p, kp, vp = _norm_qkv(xp, g_mix, b_w_qkv16[slot], tm=tm,
                                      tiles_per_batch=seq // tm, tail_rows=tail_b, **qkv_args)
            qkv_s, kn, vn = _norm_qkv(xs, g_mix, b_w_qkv16[slot], tm=rows_s,
                                      tiles_per_batch=1, tail_rows=rows_s, **qkv_args)
            sinks = sinks_perm[slot].reshape(1, n_b_heads)
            mp = _attn_b_prompt(qkv_p, b_bias_prompt, sinks, batch=batch, seq=seq,
                                d_model=d, kv_width=b_kv_width)
            ms, ks, vs = _attn_b_sample(
                qkv_s, kn.reshape(dec_batch, dec_seq, b_kv_width),
                vn.reshape(dec_batch, dec_seq, b_kv_width),
                cache_b_k[slot].reshape(dec_batch, cl_b, b_kv_width),
                cache_b_v[slot].reshape(dec_batch, cl_b, b_kv_width),
                b_bias_sample, sinks, d_model=d)
            w_o = b_w_o16[slot]
            states["b_kp"].append(kp.reshape((batch, tail_b) + cache_b_k.shape[3:]))
            states["b_vp"].append(vp.reshape((batch, tail_b) + cache_b_k.shape[3:]))
            states["b_ks"].append(ks.reshape(cache_b_k.shape[1:]))
            states["b_vs"].append(vs.reshape(cache_b_k.shape[1:]))
        last = layer == depth - 1
        xp = _oproj_mlp(xp, mp, w_o, g_ffn, w_up16[layer], w_down16[layer], gfin,
                        tm=tm, tf=tf, final_norm=last)
        xs = _oproj_mlp(xs, ms, w_o, g_ffn, w_up16[layer], w_down16[layer], gfin,
                        tm=rows_s, tf=tf, final_norm=last)

    return (xp.reshape(batch, seq, d), xs.reshape(dec_batch, dec_seq, d),
            jnp.stack(states["a_kp"]), jnp.stack(states["a_vp"]),
            jnp.stack(states["b_kp"]), jnp.stack(states["b_vp"]),
            jnp.stack(states["a_ks"]), jnp.stack(states["a_vs"]),
            jnp.stack(states["b_ks"]), jnp.stack(states["b_vs"]))
```

```python
import functools

import jax
import jax.numpy as jnp
import numpy as np
from jax import lax
from jax.experimental import pallas as pl
from jax.experimental.pallas import tpu as pltpu

F32 = jnp.float32
BF16 = jnp.bfloat16

CHUNK = 64
N_MIXERS = 2
A_HEAD_DIM = 128
A_PREV_CHUNKS = 8
A_REL_CLIP = 128
B_PREV_CHUNKS = 2
B_HEAD_DIM = 64
B_KV_HEADS = 8
RMS_EPS = 1e-6
NEG_INF = -1e30
LOG2E = float(np.log2(np.e))

LANES = 128
VMEM_LIMIT_BYTES = 56 * 1024 * 1024

A_TQ = 256
A_HEADS_PER_STEP = 4
B_TQ = 128
NEW_KEY_PAD = 128


def _rms_scale(x):
    return x * lax.rsqrt(jnp.mean(x * x, axis=-1, keepdims=True) + RMS_EPS)


def _dot_nt(a, b):
    return lax.dot_general(a, b, (((1,), (1,)), ((), ())), preferred_element_type=F32)


def _softmax_pv(s, v):
    m = s.max(axis=-1, keepdims=True)
    p = jnp.exp2(s - m)
    l = p.sum(axis=-1, keepdims=True)
    return jnp.dot(p.astype(BF16), v, preferred_element_type=F32) / l


def _norm_qkv_kernel(x_ref, g_ref, w_ref, qkv_ref, kst_ref, vst_ref, h_ref, *,
                     nq, nk, q_scale, tiles_per_batch, tail_rows):
    m = pl.program_id(0)
    n = pl.program_id(1)

    @pl.when(n == 0)
    def _():
        h_ref[...] = (_rms_scale(x_ref[...]) * g_ref[...]).astype(BF16)

    acc = jnp.dot(h_ref[...], w_ref[...], preferred_element_type=F32)
    qkv_ref[...] = (acc * jnp.where(n < nq, q_scale, 1.0)).astype(BF16)

    tm = acc.shape[0]
    is_tail = (m % tiles_per_batch) == tiles_per_batch - 1

    @pl.when(is_tail & (n >= nq) & (n < nq + nk))
    def _():
        kst_ref[0] = acc[tm - tail_rows:, :]

    @pl.when(is_tail & (n >= nq + nk))
    def _():
        vst_ref[0] = acc[tm - tail_rows:, :]


def _norm_qkv(x, g, w, *, tm, tn, q_width, kv_width, q_scale, tiles_per_batch, tail_rows):
    rows, d = x.shape
    n_total = w.shape[1]
    nq = q_width // tn
    nk = kv_width // tn
    nb = rows // (tm * tiles_per_batch)

    def state_map(first):
        def index_map(m, n):
            tail = (m % tiles_per_batch) == tiles_per_batch - 1
            j = jnp.clip(n - first, 0, nk - 1)
            return (m // tiles_per_batch, 0, jnp.where(tail, j, 0))
        return index_map

    kernel = functools.partial(_norm_qkv_kernel, nq=nq, nk=nk, q_scale=q_scale,
                               tiles_per_batch=tiles_per_batch, tail_rows=tail_rows)
    return pl.pallas_call(
        kernel,
        grid=(rows // tm, n_total // tn),
        in_specs=[
            pl.BlockSpec((tm, d), lambda m, n: (m, 0)),
            pl.BlockSpec((1, d), lambda m, n: (0, 0)),
            pl.BlockSpec((d, tn), lambda m, n: (0, n)),
        ],
        out_specs=[
            pl.BlockSpec((tm, tn), lambda m, n: (m, n)),
            pl.BlockSpec((1, tail_rows, tn), state_map(nq)),
            pl.BlockSpec((1, tail_rows, tn), state_map(nq + nk)),
        ],
        out_shape=[
            jax.ShapeDtypeStruct((rows, n_total), BF16),
            jax.ShapeDtypeStruct((nb, tail_rows, kv_width), F32),
            jax.ShapeDtypeStruct((nb, tail_rows, kv_width), F32),
        ],
        scratch_shapes=[pltpu.VMEM((tm, d), BF16)],
        compiler_params=pltpu.CompilerParams(
            dimension_semantics=("arbitrary", "arbitrary"),
            vmem_limit_bytes=VMEM_LIMIT_BYTES),
        name="norm_qkv",
    )(x, g, w)


def _attn_a_prompt_kernel(q_ref, k0_ref, k1_ref, k2_ref, v0_ref, v1_ref, v2_ref,
                          bias_ref, o_ref):
    k_refs = (k0_ref, k1_ref, k2_ref)
    v_refs = (v0_ref, v1_ref, v2_ref)
    for h in range(q_ref.shape[1] // A_HEAD_DIM):
        sl = slice(h * A_HEAD_DIM, (h + 1) * A_HEAD_DIM)
        k = jnp.concatenate([r[:, sl] for r in k_refs], axis=0)
        v = jnp.concatenate([r[:, sl] for r in v_refs], axis=0)
        s = _dot_nt(q_ref[:, sl], k) + bias_ref[0, h]
        o_ref[:, sl] = _softmax_pv(s, v).astype(BF16)


def _attn_a_prompt(qkv, bias, *, batch, seq, d_model):
    tq = A_TQ
    hw = A_HEADS_PER_STEP * A_HEAD_DIM
    nhb = d_model // hw
    nqb = seq // tq

    def kv_spec(back, col0):
        return pl.BlockSpec(
            (tq, hw), lambda b, h, i: (b * nqb + jnp.maximum(i - back, 0), col0 + h))

    return pl.pallas_call(
        _attn_a_prompt_kernel,
        grid=(batch, nhb, nqb),
        in_specs=[
            pl.BlockSpec((tq, hw), lambda b, h, i: (b * nqb + i, h)),
            kv_spec(2, nhb), kv_spec(1, nhb), kv_spec(0, nhb),
            kv_spec(2, 2 * nhb), kv_spec(1, 2 * nhb), kv_spec(0, 2 * nhb),
            pl.BlockSpec((1, A_HEADS_PER_STEP, tq, 3 * tq),
                         lambda b, h, i: (jnp.minimum(i, 2), h, 0, 0)),
        ],
        out_specs=pl.BlockSpec((tq, hw), lambda b, h, i: (b * nqb + i, h)),
        out_shape=jax.ShapeDtypeStruct((batch * seq, d_model), BF16),
        compiler_params=pltpu.CompilerParams(
            dimension_semantics=("parallel", "parallel", "parallel"),
            vmem_limit_bytes=VMEM_LIMIT_BYTES),
        name="attn_a_prompt",
    )(qkv, qkv, qkv, qkv, qkv, qkv, qkv, bias)


def _attn_a_sample_kernel(q_ref, kc_ref, vc_ref, kn_ref, vn_ref, bias_ref,
                          o_ref, ko_ref, vo_ref):
    t = q_ref.shape[0]
    cl = kc_ref.shape[1]
    kc, vc, kn, vn = kc_ref[0], vc_ref[0], kn_ref[0], vn_ref[0]
    ko_ref[0, :cl - t, :] = kc[t:, :]
    ko_ref[0, cl - t:, :] = kn
    vo_ref[0, :cl - t, :] = vc[t:, :]
    vo_ref[0, cl - t:, :] = vn
    pad = jnp.zeros((NEW_KEY_PAD - t, A_HEAD_DIM), BF16)
    for h in range(q_ref.shape[1] // A_HEAD_DIM):
        sl = slice(h * A_HEAD_DIM, (h + 1) * A_HEAD_DIM)
        k = jnp.concatenate([kc[:, sl].astype(BF16), kn[:, sl].astype(BF16), pad], axis=0)
        v = jnp.concatenate([vc[:, sl].astype(BF16), vn[:, sl].astype(BF16), pad], axis=0)
        s = _dot_nt(q_ref[:, sl], k) + bias_ref[h]
        o_ref[:, sl] = _softmax_pv(s, v).astype(BF16)


def _attn_a_sample(qkv, k_new, v_new, cache_k, cache_v, bias, *, d_model):
    nb, cl, _ = cache_k.shape
    t = k_new.shape[1]
    hw = A_HEADS_PER_STEP * A_HEAD_DIM
    nhb = d_model // hw
    cache_spec = pl.BlockSpec((1, cl, hw), lambda b, h: (b, 0, h))
    new_spec = pl.BlockSpec((1, t, hw), lambda b, h: (b, 0, h))
    return pl.pallas_call(
        _attn_a_sample_kernel,
        grid=(nb, nhb),
        in_specs=[
            pl.BlockSpec((t, hw), lambda b, h: (b, h)),
            cache_spec, cache_spec, new_spec, new_spec,
            pl.BlockSpec((A_HEADS_PER_STEP, t, cl + NEW_KEY_PAD), lambda b, h: (h, 0, 0)),
        ],
        out_specs=[pl.BlockSpec((t, hw), lambda b, h: (b, h)), cache_spec, cache_spec],
        out_shape=[
            jax.ShapeDtypeStruct((nb * t, d_model), BF16),
            jax.ShapeDtypeStruct(cache_k.shape, F32),
            jax.ShapeDtypeStruct(cache_v.shape, F32),
        ],
        compiler_params=pltpu.CompilerParams(
            dimension_semantics=("parallel", "parallel"),
            vmem_limit_bytes=VMEM_LIMIT_BYTES),
        name="attn_a_sample",
    )(qkv, cache_k, cache_v, k_new, v_new, bias)


def _attn_b_heads(q_ref, k_blocks, v_blocks, bias_ref, sink_ref, o_ref):
    tq = q_ref.shape[0]
    n_pairs = B_KV_HEADS // 2
    bpp = q_ref.shape[1] // LANES // n_pairs
    half = bpp * tq
    low = lax.broadcasted_iota(jnp.int32, (tq, LANES), 1) < B_HEAD_DIM
    for pair in range(n_pairs):
        pair_sl = slice(pair * LANES, (pair + 1) * LANES)
        k = jnp.concatenate([kb(pair_sl) for kb in k_blocks], axis=0)
        v = jnp.concatenate([vb(pair_sl) for vb in v_blocks], axis=0)
        q_cols = [q_ref[:, (pair * bpp + c) * LANES:(pair * bpp + c + 1) * LANES]
                  for c in range(bpp)]
        zero = jnp.zeros_like(q_cols[0])
        q_stack = jnp.concatenate(
            [jnp.where(low, qc, zero) for qc in q_cols]
            + [jnp.where(low, zero, qc) for qc in q_cols], axis=0)
        s = _dot_nt(q_stack, k) + bias_ref[0, pair]
        m = s.max(axis=-1, keepdims=True)
        p = jnp.exp2(s - m).astype(BF16)
        low_k = lax.broadcasted_iota(jnp.int32, v.shape, 1) < B_HEAD_DIM
        one = jnp.ones_like(v)
        out_even = jnp.dot(p[:half], jnp.where(low_k, v, one), preferred_element_type=F32)
        out_odd = jnp.dot(p[half:], jnp.where(low_k, one, v), preferred_element_type=F32)
        for c in range(bpp):
            rows = slice(c * tq, (c + 1) * tq)
            oe, oo = out_even[rows], out_odd[rows]
            num = jnp.where(low, oe, oo)
            den = pltpu.roll(jnp.where(low, oo, oe), B_HEAD_DIM, axis=1)
            cb = pair * bpp + c
            sink_even = jnp.exp2(sink_ref[0, 2 * cb] - m[rows])
            sink_odd = jnp.exp2(sink_ref[0, 2 * cb + 1] - m[half + c * tq:half + (c + 1) * tq])
            den = den + jnp.where(low, sink_even, sink_odd)
            o_ref[:, cb * LANES:(cb + 1) * LANES] = (num / den).astype(BF16)


def _attn_b_prompt_kernel(q_ref, k0_ref, k1_ref, v0_ref, v1_ref, bias_ref, sink_ref, o_ref):
    _attn_b_heads(
        q_ref,
        [lambda sl: k0_ref[:, sl], lambda sl: k1_ref[:, sl]],
        [lambda sl: v0_ref[:, sl], lambda sl: v1_ref[:, sl]],
        bias_ref, sink_ref, o_ref)


def _attn_b_prompt(qkv, bias, sinks, *, batch, seq, d_model, kv_width):
    tq = B_TQ
    nqb = seq // tq
    nq_cols = d_model // kv_width

    def kv_spec(back, col):
        return pl.BlockSpec(
            (tq, kv_width), lambda b, i: (b * nqb + jnp.maximum(i - back, 0), col))

    return pl.pallas_call(
        _attn_b_prompt_kernel,
        grid=(batch, nqb),
        in_specs=[
            pl.BlockSpec((tq, d_model), lambda b, i: (b * nqb + i, 0)),
            kv_spec(1, nq_cols), kv_spec(0, nq_cols),
            kv_spec(1, nq_cols + 1), kv_spec(0, nq_cols + 1),
            pl.BlockSpec((1,) + bias.shape[1:],
                         lambda b, i: (jnp.where(i == 0, 1, 0), 0, 0, 0)),
            pl.BlockSpec(memory_space=pltpu.SMEM),
        ],
        out_specs=pl.BlockSpec((tq, d_model), lambda b, i: (b * nqb + i, 0)),
        out_shape=jax.ShapeDtypeStruct((batch * seq, d_model), BF16),
        compiler_params=pltpu.CompilerParams(
            dimension_semantics=("parallel", "parallel"),
            vmem_limit_bytes=VMEM_LIMIT_BYTES),
        name="attn_b_prompt",
    )(qkv, qkv, qkv, qkv, qkv, bias, sinks)


def _attn_b_sample_kernel(q_ref, kc_ref, vc_ref, kn_ref, vn_ref, bias_ref, sink_ref,
                          o_ref, ko_ref, vo_ref):
    t = q_ref.shape[0]
    cl = kc_ref.shape[1]
    kc, vc, kn, vn = kc_ref[0], vc_ref[0], kn_ref[0], vn_ref[0]
    ko_ref[0, :cl - t, :] = kc[t:, :]
    ko_ref[0, cl - t:, :] = kn
    vo_ref[0, :cl - t, :] = vc[t:, :]
    vo_ref[0, cl - t:, :] = vn
    pad = jnp.zeros((NEW_KEY_PAD - t, LANES), BF16)
    _attn_b_heads(
        q_ref,
        [lambda sl: kc[:, sl].astype(BF16),
         lambda sl: jnp.concatenate([kn[:, sl].astype(BF16), pad], axis=0)],
        [lambda sl: vc[:, sl].astype(BF16),
         lambda sl: jnp.concatenate([vn[:, sl].astype(BF16), pad], axis=0)],
        bias_ref, sink_ref, o_ref)


def _attn_b_sample(qkv, k_new, v_new, cache_k, cache_v, bias, sinks, *, d_model):
    nb, cl, kv_width = cache_k.shape
    t = k_new.shape[1]
    cache_spec = pl.BlockSpec((1, cl, kv_width), lambda b: (b, 0, 0))
    new_spec = pl.BlockSpec((1, t, kv_width), lambda b: (b, 0, 0))
    return pl.pallas_call(
        _attn_b_sample_kernel,
        grid=(nb,),
        in_specs=[
            pl.BlockSpec((t, d_model), lambda b: (b, 0)),
            cache_spec, cache_spec, new_spec, new_spec,
            pl.BlockSpec(bias.shape, lambda b: (0, 0, 0, 0)),
            pl.BlockSpec(memory_space=pltpu.SMEM),
        ],
        out_specs=[pl.BlockSpec((t, d_model), lambda b: (b, 0)), cache_spec, cache_spec],
        out_shape=[
            jax.ShapeDtypeStruct((nb * t, d_model), BF16),
            jax.ShapeDtypeStruct(cache_k.shape, F32),
            jax.ShapeDtypeStruct(cache_v.shape, F32),
        ],
        compiler_params=pltpu.CompilerParams(
            dimension_semantics=("parallel",),
            vmem_limit_bytes=VMEM_LIMIT_BYTES),
        name="attn_b_sample",
    )(qkv, cache_k, cache_v, k_new, v_new, bias, sinks)


def _oproj_mlp_kernel(x_ref, a_ref, wo_ref, g_ref, wup_ref, wdn_ref, gfin_ref,
                      o_ref, h_ref, *, final_norm):
    f = pl.program_id(1)

    @pl.when(f == 0)
    def _():
        x1 = x_ref[...] + jnp.dot(a_ref[...], wo_ref[...], preferred_element_type=F32)
        o_ref[...] = x1
        h_ref[...] = (_rms_scale(x1) * g_ref[...]).astype(BF16)

    u = jnp.dot(h_ref[...], wup_ref[...], preferred_element_type=F32)
    act = jnp.square(jnp.maximum(u, 0.0)).astype(BF16)
    o_ref[...] += jnp.dot(act, wdn_ref[...], preferred_element_type=F32)

    if final_norm:
        @pl.when(f == pl.num_programs(1) - 1)
        def _():
            o_ref[...] = _rms_scale(o_ref[...]) * gfin_ref[...]


def _oproj_mlp(x, attn, wo, g, wup, wdn, gfin, *, tm, tf, final_norm):
    rows, d = x.shape
    d_ff = wup.shape[1]
    kernel = functools.partial(_oproj_mlp_kernel, final_norm=final_norm)
    return pl.pallas_call(
        kernel,
        grid=(rows // tm, d_ff // tf),
        in_specs=[
            pl.BlockSpec((tm, d), lambda m, f: (m, 0)),
            pl.BlockSpec((tm, d), lambda m, f: (m, 0)),
            pl.BlockSpec((d, d), lambda m, f: (0, 0), pipeline_mode=pl.Buffered(1)),
            pl.BlockSpec((1, d), lambda m, f: (0, 0)),
            pl.BlockSpec((d, tf), lambda m, f: (0, f)),
            pl.BlockSpec((tf, d), lambda m, f: (f, 0)),
            pl.BlockSpec((1, d), lambda m, f: (0, 0)),
        ],
        out_specs=pl.BlockSpec((tm, d), lambda m, f: (m, 0)),
        out_shape=jax.ShapeDtypeStruct((rows, d), F32),
        scratch_shapes=[pltpu.VMEM((tm, d), BF16)],
        compiler_params=pltpu.CompilerParams(
            dimension_semantics=("parallel", "arbitrary"),
            vmem_limit_bytes=VMEM_LIMIT_BYTES),
        name="oproj_mlp",
    )(x, attn, wo, g, wup, wdn, gfin)


def _band_mask(n_q_chunks, n_k_chunks, n_prev):
    ci = np.arange(n_q_chunks * CHUNK)[:, None] // CHUNK
    cj = np.arange(n_k_chunks * CHUNK)[None, :] // CHUNK
    return (cj >= ci) & (cj <= ci + n_prev)


def _rel_bias_tile(table, n_rows, n_cols, key_offset):
    length = n_rows + n_cols
    rel = np.arange(length) - (n_rows - 1) - key_offset
    diag = table[:, np.clip(rel, -A_REL_CLIP, A_REL_CLIP) + A_REL_CLIP]
    flat = jnp.tile(diag, (1, n_rows))[:, :n_rows * (length - 1)]
    return flat.reshape(table.shape[0], n_rows, length - 1)[:, :, n_rows - 1:n_rows - 1 + n_cols]


def _a_prompt_bias(table):
    bias = _rel_bias_tile(table, A_TQ, 3 * A_TQ, 2 * A_TQ)
    band = _band_mask(A_TQ // CHUNK, 3 * A_TQ // CHUNK, A_PREV_CHUNKS)
    col = np.arange(3 * A_TQ)[None, :]
    masks = np.stack([band & (col >= (2 - v) * A_TQ) for v in range(3)])
    return jnp.where(masks[:, None], bias[None], NEG_INF)


def _a_sample_bias(table, t, cl):
    bias = _rel_bias_tile(table, t, cl + NEW_KEY_PAD, cl)
    valid = np.broadcast_to(np.arange(cl + NEW_KEY_PAD)[None, :] < cl + t, (t, cl + NEW_KEY_PAD))
    return jnp.where(valid[None], bias, NEG_INF)


def _b_head_order(n_q_heads):
    group = n_q_heads // B_KV_HEADS
    order = []
    for pair in range(B_KV_HEADS // 2):
        for g in range(group):
            for side in range(2):
                order.append(group * (2 * pair + side) + g)
    return np.asarray(order)


def _alibi_tile(n_q_heads, n_rows, n_cols, key_offset, valid):
    group = n_q_heads // B_KV_HEADS
    slopes = (2.0 ** (-8.0 * np.arange(1, n_q_heads + 1) / n_q_heads)).astype(np.float32)
    qi = np.arange(n_rows)[:, None]
    kj = np.arange(n_cols)[None, :]
    rel = np.abs(kj - key_offset - qi).astype(np.float32)
    alibi = (-slopes[:, None, None] * rel[None]).astype(np.float64)
    bias = np.where(valid[None], alibi * LOG2E, NEG_INF)
    stacked = [[bias[group * (2 * pair + side) + g] for side in range(2) for g in range(group)]
               for pair in range(B_KV_HEADS // 2)]
    return np.asarray(stacked, np.float32).reshape(B_KV_HEADS // 2, 2 * group * n_rows, n_cols)


def _b_prompt_bias(n_q_heads):
    mask = _band_mask(B_TQ // CHUNK, 2 * B_TQ // CHUNK, B_PREV_CHUNKS)
    first = mask & (np.arange(2 * B_TQ)[None, :] >= B_TQ)
    return np.stack([_alibi_tile(n_q_heads, B_TQ, 2 * B_TQ, B_TQ, mask),
                     _alibi_tile(n_q_heads, B_TQ, 2 * B_TQ, B_TQ, first)])


def _b_sample_bias(n_q_heads, t, cl):
    valid = np.broadcast_to(np.arange(cl + NEW_KEY_PAD)[None, :] < cl + t, (t, cl + NEW_KEY_PAD))
    return _alibi_tile(n_q_heads, t, cl + NEW_KEY_PAD, cl, valid)[None]


def kernel(x_prompt, x_sample, cache_a_k, cache_a_v, cache_b_k, cache_b_v, norm_mix, norm_ffn,
           norm_final, a_w_qkv, a_w_o, a_rel_bias, b_w_qkv, b_w_o, b_sinks, w_up, w_down):
    batch, seq, d = x_prompt.shape
    dec_batch, dec_seq, _ = x_sample.shape
    depth = norm_mix.shape[0]
    a_kv_width = cache_a_k.shape[3] * cache_a_k.shape[4]
    b_kv_width = cache_b_k.shape[3] * cache_b_k.shape[4]
    cl_a, cl_b = cache_a_k.shape[2], cache_b_k.shape[2]
    n_b_heads = d // B_HEAD_DIM
    tail_a = min(A_PREV_CHUNKS * CHUNK, seq)
    tail_b = min(B_PREV_CHUNKS * CHUNK, seq)

    tm = 512
    tf = 1024
    rows_s = dec_batch * dec_seq

    order = _b_head_order(n_b_heads)
    q_cols = (order[:, None] * B_HEAD_DIM + np.arange(B_HEAD_DIM)[None, :]).reshape(-1)
    b_cols = np.concatenate([q_cols, np.arange(d, d + 2 * b_kv_width)])
    a_w_qkv16 = a_w_qkv.astype(BF16)
    a_w_o16 = a_w_o.astype(BF16)
    b_w_qkv16 = b_w_qkv[:, :, b_cols].astype(BF16)
    b_w_o16 = b_w_o[:, q_cols, :].astype(BF16)
    w_up16 = w_up.astype(BF16)
    w_down16 = w_down.astype(BF16)
    sinks_perm = b_sinks[:, order] * LOG2E
    a_tables = a_rel_bias * LOG2E

    b_bias_prompt = jnp.asarray(_b_prompt_bias(n_b_heads))
    b_bias_sample = jnp.asarray(_b_sample_bias(n_b_heads, dec_seq, cl_b))

    xp = x_prompt.reshape(batch * seq, d)
    xs = x_sample.reshape(rows_s, d)
    gfin = norm_final.reshape(1, d)

    states = {k: [] for k in ("a_kp", "a_vp", "a_ks", "a_vs", "b_kp", "b_vp", "b_ks", "b_vs")}
    for layer in range(depth):
        slot = layer // N_MIXERS
        g_mix = norm_mix[layer].reshape(1, d)
        g_ffn = norm_ffn[layer].reshape(1, d)
        if layer % N_MIXERS == 0:
            qkv_args = dict(tn=1024, q_width=d, kv_width=a_kv_width,
                            q_scale=A_HEAD_DIM ** -0.5 * LOG2E)
            qkv_p, kp, vp = _norm_qkv(xp, g_mix, a_w_qkv16[slot], tm=tm,
                                      tiles_per_batch=seq // tm, tail_rows=tail_a, **qkv_args)
            qkv_s, kn, vn = _norm_qkv(xs, g_mix, a_w_qkv16[slot], tm=rows_s,
                                      tiles_per_batch=1, tail_rows=rows_s, **qkv_args)
            table = a_tables[slot]
            mp = _attn_a_prompt(qkv_p, _a_prompt_bias(table), batch=batch, seq=seq, d_model=d)
            ms, ks, vs = _attn_a_sample(
                qkv_s, kn.reshape(dec_batch, dec_seq, a_kv_width),
                vn.reshape(dec_batch, dec_seq, a_kv_width),
                cache_a_k[slot].reshape(dec_batch, cl_a, a_kv_width),
                cache_a_v[slot].reshape(dec_batch, cl_a, a_kv_width),
                _a_sample_bias(table, dec_seq, cl_a), d_model=d)
            w_o = a_w_o16[slot]
            states["a_kp"].append(kp.reshape((batch, tail_a) + cache_a_k.shape[3:]))
            states["a_vp"].append(vp.reshape((batch, tail_a) + cache_a_k.shape[3:]))
            states["a_ks"].append(ks.reshape(cache_a_k.shape[1:]))
            states["a_vs"].append(vs.reshape(cache_a_k.shape[1:]))
        else:
            qkv_args = dict(tn=b_kv_width, q_width=d, kv_width=b_kv_width,
                            q_scale=B_HEAD_DIM ** -0.5 * LOG2E)
            qkv_p, kp, vp = _norm_qkv(xp, g_mix, b_w_qkv16[slot], tm=tm,
                                      tiles_per_batch=seq // tm, tail_rows=tail_b, **qkv_args)
            qkv_s, kn, vn = _norm_qkv(xs, g_mix, b_w_qkv16[slot], tm=rows_s,
                                      tiles_per_batch=1, tail_rows=rows_s, **qkv_args)
            sinks = sinks_perm[slot].reshape(1, n_b_heads)
            mp = _attn_b_prompt(qkv_p, b_bias_prompt, sinks, batch=batch, seq=seq,
                                d_model=d, kv_width=b_kv_width)
            ms, ks, vs = _attn_b_sample(
                qkv_s, kn.reshape(dec_batch, dec_seq, b_kv_width),
                vn.reshape(dec_batch, dec_seq, b_kv_width),
                cache_b_k[slot].reshape(dec_batch, cl_b, b_kv_width),
                cache_b_v[slot].reshape(dec_batch, cl_b, b_kv_width),
                b_bias_sample, sinks, d_model=d)
            w_o = b_w_o16[slot]
            states["b_kp"].append(kp.reshape((batch, tail_b) + cache_b_k.shape[3:]))
            states["b_vp"].append(vp.reshape((batch, tail_b) + cache_b_k.shape[3:]))
            states["b_ks"].append(ks.reshape(cache_b_k.shape[1:]))
            states["b_vs"].append(vs.reshape(cache_b_k.shape[1:]))
        last = layer == depth - 1
        xp = _oproj_mlp(xp, mp, w_o, g_ffn, w_up16[layer], w_down16[layer], gfin,
                        tm=tm, tf=tf, final_norm=last)
        xs = _oproj_mlp(xs, ms, w_o, g_ffn, w_up16[layer], w_down16[layer], gfin,
                        tm=rows_s, tf=tf, final_norm=last)

    return (xp.reshape(batch, seq, d), xs.reshape(dec_batch, dec_seq, d),
            jnp.stack(states["a_kp"]), jnp.stack(states["a_vp"]),
            jnp.stack(states["b_kp"]), jnp.stack(states["b_vp"]),
            jnp.stack(states["a_ks"]), jnp.stack(states["a_vs"]),
            jnp.stack(states["b_ks"]), jnp.stack(states["b_vs"]))
```

```python
import functools

import jax
import jax.numpy as jnp
import numpy as np
from jax import lax
from jax.experimental import pallas as pl
from jax.experimental.pallas import tpu as pltpu

F32 = jnp.float32
BF16 = jnp.bfloat16

CHUNK = 64
N_MIXERS = 2
A_HEAD_DIM = 128
A_PREV_CHUNKS = 8
A_REL_CLIP = 128
B_PREV_CHUNKS = 2
B_HEAD_DIM = 64
B_KV_HEADS = 8
RMS_EPS = 1e-6
NEG_INF = -1e30
LOG2E = float(np.log2(np.e))

LANES = 128
VMEM_LIMIT_BYTES = 56 * 1024 * 1024

A_TQ = 256
A_HEADS_PER_STEP = 8
B_TQ = 128
NEW_KEY_PAD = 128


def _rms_scale(x):
    return x * lax.rsqrt(jnp.mean(x * x, axis=-1, keepdims=True) + RMS_EPS)


def _dot_nt(a, b):
    return lax.dot_general(a, b, (((1,), (1,)), ((), ())), preferred_element_type=F32)


def _softmax_pv(s, v):
    m = s.max(axis=-1, keepdims=True)
    p = jnp.exp2(s - m).astype(BF16)
    out = jnp.dot(p, jnp.concatenate([v, jnp.ones_like(v)], axis=1), preferred_element_type=F32)
    width = v.shape[1]
    return out[:, :width] / out[:, width:]


def _norm_qkv_kernel(x_ref, g_ref, w_ref, qkv_ref, kst_ref, vst_ref, h_ref, *,
                     nq, nk, q_scale, tiles_per_batch, tail_rows):
    m = pl.program_id(0)
    n = pl.program_id(1)

    @pl.when(n == 0)
    def _():
        h_ref[...] = (_rms_scale(x_ref[...]) * g_ref[...]).astype(BF16)

    acc = jnp.dot(h_ref[...], w_ref[...], preferred_element_type=F32)
    qkv_ref[...] = (acc * jnp.where(n < nq, q_scale, 1.0)).astype(BF16)

    tm = acc.shape[0]
    is_tail = (m % tiles_per_batch) == tiles_per_batch - 1
    tail = acc[tm - tail_rows:, :]

    if nk == 0:
        @pl.when(is_tail & (n == nq))
        def _():
            kv_width = kst_ref.shape[2]
            kst_ref[0] = tail[:, :kv_width]
            vst_ref[0] = tail[:, kv_width:]
    else:
        @pl.when(is_tail & (n >= nq) & (n < nq + nk))
        def _():
            kst_ref[0] = tail

        @pl.when(is_tail & (n >= nq + nk))
        def _():
            vst_ref[0] = tail


def _norm_qkv(x, g, w, layer, *, tm, tn, q_width, kv_width, q_scale, tiles_per_batch, tail_rows):
    rows, d = x.shape
    n_total = w.shape[2]
    nq = q_width // tn
    nk = kv_width // tn
    assert (nk > 0 and kv_width % tn == 0) or tn == 2 * kv_width
    state_width = min(tn, kv_width)
    nb = rows // (tm * tiles_per_batch)

    def state_map(first):
        def index_map(m, n):
            tail = (m % tiles_per_batch) == tiles_per_batch - 1
            j = jnp.clip(n - first, 0, max(nk - 1, 0))
            return (m // tiles_per_batch, 0, jnp.where(tail, j, 0))
        return index_map

    kernel = functools.partial(_norm_qkv_kernel, nq=nq, nk=nk, q_scale=q_scale,
                               tiles_per_batch=tiles_per_batch, tail_rows=tail_rows)
    return pl.pallas_call(
        kernel,
        grid=(rows // tm, n_total // tn),
        in_specs=[
            pl.BlockSpec((tm, d), lambda m, n: (m, 0)),
            pl.BlockSpec((1, d), lambda m, n: (0, 0)),
            pl.BlockSpec((None, d, tn), lambda m, n: (layer, 0, n)),
        ],
        out_specs=[
            pl.BlockSpec((tm, tn), lambda m, n: (m, n)),
            pl.BlockSpec((1, tail_rows, state_width), state_map(nq)),
            pl.BlockSpec((1, tail_rows, state_width), state_map(nq + nk)),
        ],
        out_shape=[
            jax.ShapeDtypeStruct((rows, n_total), BF16),
            jax.ShapeDtypeStruct((nb, tail_rows, kv_width), F32),
            jax.ShapeDtypeStruct((nb, tail_rows, kv_width), F32),
        ],
        scratch_shapes=[pltpu.VMEM((tm, d), BF16)],
        compiler_params=pltpu.CompilerParams(
            dimension_semantics=("arbitrary", "arbitrary"),
            vmem_limit_bytes=VMEM_LIMIT_BYTES),
        name="norm_qkv",
    )(x, g, w)


def _attn_a_prompt_kernel(q_ref, k0_ref, k1_ref, k2_ref, v0_ref, v1_ref, v2_ref,
                          bias_ref, o_ref):
    k_refs = (k0_ref, k1_ref, k2_ref)
    v_refs = (v0_ref, v1_ref, v2_ref)
    for h in range(q_ref.shape[1] // A_HEAD_DIM):
        sl = slice(h * A_HEAD_DIM, (h + 1) * A_HEAD_DIM)
        k = jnp.concatenate([r[:, sl] for r in k_refs], axis=0)
        v = jnp.concatenate([r[:, sl] for r in v_refs], axis=0)
        s = _dot_nt(q_ref[:, sl], k) + bias_ref[0, h]
        o_ref[:, sl] = _softmax_pv(s, v).astype(BF16)


def _attn_a_prompt(qkv, bias, *, batch, seq, d_model):
    tq = A_TQ
    hw = A_HEADS_PER_STEP * A_HEAD_DIM
    nhb = d_model // hw
    nqb = seq // tq

    def kv_spec(back, col0):
        return pl.BlockSpec(
            (tq, hw), lambda b, h, i: (b * nqb + jnp.maximum(i - back, 0), col0 + h))

    return pl.pallas_call(
        _attn_a_prompt_kernel,
        grid=(batch, nhb, nqb),
        in_specs=[
            pl.BlockSpec((tq, hw), lambda b, h, i: (b * nqb + i, h)),
            kv_spec(2, nhb), kv_spec(1, nhb), kv_spec(0, nhb),
            kv_spec(2, 2 * nhb), kv_spec(1, 2 * nhb), kv_spec(0, 2 * nhb),
            pl.BlockSpec((1, A_HEADS_PER_STEP, tq, 3 * tq),
                         lambda b, h, i: (jnp.minimum(i, 2), h, 0, 0)),
        ],
        out_specs=pl.BlockSpec((tq, hw), lambda b, h, i: (b * nqb + i, h)),
        out_shape=jax.ShapeDtypeStruct((batch * seq, d_model), BF16),
        compiler_params=pltpu.CompilerParams(
            dimension_semantics=("parallel", "parallel", "parallel"),
            vmem_limit_bytes=VMEM_LIMIT_BYTES),
        name="attn_a_prompt",
    )(qkv, qkv, qkv, qkv, qkv, qkv, qkv, bias)


def _attn_a_sample_kernel(q_ref, kc_ref, vc_ref, kn_ref, vn_ref, bias_ref,
                          o_ref, ko_ref, vo_ref):
    t = q_ref.shape[0]
    cl = kc_ref.shape[1]
    kc, vc, kn, vn = kc_ref[0], vc_ref[0], kn_ref[0], vn_ref[0]
    ko_ref[0, :cl - t, :] = kc[t:, :]
    ko_ref[0, cl - t:, :] = kn
    vo_ref[0, :cl - t, :] = vc[t:, :]
    vo_ref[0, cl - t:, :] = vn
    pad = jnp.zeros((NEW_KEY_PAD - t, A_HEAD_DIM), BF16)
    for h in range(q_ref.shape[1] // A_HEAD_DIM):
        sl = slice(h * A_HEAD_DIM, (h + 1) * A_HEAD_DIM)
        k = jnp.concatenate([kc[:, sl].astype(BF16), kn[:, sl].astype(BF16), pad], axis=0)
        v = jnp.concatenate([vc[:, sl].astype(BF16), vn[:, sl].astype(BF16), pad], axis=0)
        s = _dot_nt(q_ref[:, sl], k) + bias_ref[h]
        o_ref[:, sl] = _softmax_pv(s, v).astype(BF16)


def _attn_a_sample(qkv, k_new, v_new, cache_k, cache_v, bias, *, d_model):
    nb, cl, _ = cache_k.shape
    t = k_new.shape[1]
    hw = A_HEADS_PER_STEP * A_HEAD_DIM
    nhb = d_model // hw
    cache_spec = pl.BlockSpec((1, cl, hw), lambda b, h: (b, 0, h))
    new_spec = pl.BlockSpec((1, t, hw), lambda b, h: (b, 0, h))
    return pl.pallas_call(
        _attn_a_sample_kernel,
        grid=(nb, nhb),
        in_specs=[
            pl.BlockSpec((t, hw), lambda b, h: (b, h)),
            cache_spec, cache_spec, new_spec, new_spec,
            pl.BlockSpec((A_HEADS_PER_STEP, t, cl + NEW_KEY_PAD), lambda b, h: (h, 0, 0)),
        ],
        out_specs=[pl.BlockSpec((t, hw), lambda b, h: (b, h)), cache_spec, cache_spec],
        out_shape=[
            jax.ShapeDtypeStruct((nb * t, d_model), BF16),
            jax.ShapeDtypeStruct(cache_k.shape, F32),
            jax.ShapeDtypeStruct(cache_v.shape, F32),
        ],
        compiler_params=pltpu.CompilerParams(
            dimension_semantics=("parallel", "parallel"),
            vmem_limit_bytes=VMEM_LIMIT_BYTES),
        name="attn_a_sample",
    )(qkv, cache_k, cache_v, k_new, v_new, bias)


def _attn_b_heads(q_ref, k_blocks, v_blocks, bias_ref, sink_ref, o_ref):
    tq = q_ref.shape[0]
    n_pairs = B_KV_HEADS // 2
    bpp = q_ref.shape[1] // LANES // n_pairs
    half = bpp * tq
    low = lax.broadcasted_iota(jnp.int32, (tq, LANES), 1) < B_HEAD_DIM
    for pair in range(n_pairs):
        pair_sl = slice(pair * LANES, (pair + 1) * LANES)
        k = jnp.concatenate([kb(pair_sl) for kb in k_blocks], axis=0)
        v = jnp.concatenate([vb(pair_sl) for vb in v_blocks], axis=0)
        q_cols = [q_ref[:, (pair * bpp + c) * LANES:(pair * bpp + c + 1) * LANES]
                  for c in range(bpp)]
        zero = jnp.zeros_like(q_cols[0])
        q_stack = jnp.concatenate(
            [jnp.where(low, qc, zero) for qc in q_cols]
            + [jnp.where(low, zero, qc) for qc in q_cols], axis=0)
        s = _dot_nt(q_stack, k) + bias_ref[0, pair]
        m = s.max(axis=-1, keepdims=True)
        p = jnp.exp2(s - m).astype(BF16)
        low_k = lax.broadcasted_iota(jnp.int32, v.shape, 1) < B_HEAD_DIM
        one = jnp.ones_like(v)
        out_even = jnp.dot(p[:half], jnp.where(low_k, v, one), preferred_element_type=F32)
        out_odd = jnp.dot(p[half:], jnp.where(low_k, one, v), preferred_element_type=F32)
        for c in range(bpp):
            rows = slice(c * tq, (c + 1) * tq)
            oe, oo = out_even[rows], out_odd[rows]
            num = jnp.where(low, oe, oo)
            den = pltpu.roll(jnp.where(low, oo, oe), B_HEAD_DIM, axis=1)
            cb = pair * bpp + c
            sink_even = jnp.exp2(sink_ref[0, 2 * cb] - m[rows])
            sink_odd = jnp.exp2(sink_ref[0, 2 * cb + 1] - m[half + c * tq:half + (c + 1) * tq])
            den = den + jnp.where(low, sink_even, sink_odd)
            o_ref[:, cb * LANES:(cb + 1) * LANES] = (num / den).astype(BF16)


def _attn_b_prompt_kernel(q_ref, k0_ref, k1_ref, v0_ref, v1_ref, bias_ref, sink_ref, o_ref):
    _attn_b_heads(
        q_ref,
        [lambda sl: k0_ref[:, sl], lambda sl: k1_ref[:, sl]],
        [lambda sl: v0_ref[:, sl], lambda sl: v1_ref[:, sl]],
        bias_ref, sink_ref, o_ref)


def _attn_b_prompt(qkv, bias, sinks, *, batch, seq, d_model, kv_width):
    tq = B_TQ
    nqb = seq // tq
    nq_cols = d_model // kv_width

    def kv_spec(back, col):
        return pl.BlockSpec(
            (tq, kv_width), lambda b, i: (b * nqb + jnp.maximum(i - back, 0), col))

    return pl.pallas_call(
        _attn_b_prompt_kernel,
        grid=(batch, nqb),
        in_specs=[
            pl.BlockSpec((tq, d_model), lambda b, i: (b * nqb + i, 0)),
            kv_spec(1, nq_cols), kv_spec(0, nq_cols),
            kv_spec(1, nq_cols + 1), kv_spec(0, nq_cols + 1),
            pl.BlockSpec((1,) + bias.shape[1:],
                         lambda b, i: (jnp.where(i == 0, 1, 0), 0, 0, 0)),
            pl.BlockSpec(memory_space=pltpu.SMEM),
        ],
        out_specs=pl.BlockSpec((tq, d_model), lambda b, i: (b * nqb + i, 0)),
        out_shape=jax.ShapeDtypeStruct((batch * seq, d_model), BF16),
        compiler_params=pltpu.CompilerParams(
            dimension_semantics=("parallel", "parallel"),
            vmem_limit_bytes=VMEM_LIMIT_BYTES),
        name="attn_b_prompt",
    )(qkv, qkv, qkv, qkv, qkv, bias, sinks)


def _attn_b_sample_kernel(q_ref, kc_ref, vc_ref, kn_ref, vn_ref, bias_ref, sink_ref,
                          o_ref, ko_ref, vo_ref):
    t = q_ref.shape[0]
    cl = kc_ref.shape[1]
    kc, vc, kn, vn = kc_ref[0], vc_ref[0], kn_ref[0], vn_ref[0]
    ko_ref[0, :cl - t, :] = kc[t:, :]
    ko_ref[0, cl - t:, :] = kn
    vo_ref[0, :cl - t, :] = vc[t:, :]
    vo_ref[0, cl - t:, :] = vn
    pad = jnp.zeros((NEW_KEY_PAD - t, LANES), BF16)
    _attn_b_heads(
        q_ref,
        [lambda sl: kc[:, sl].astype(BF16),
         lambda sl: jnp.concatenate([kn[:, sl].astype(BF16), pad], axis=0)],
        [lambda sl: vc[:, sl].astype(BF16),
         lambda sl: jnp.concatenate([vn[:, sl].astype(BF16), pad], axis=0)],
        bias_ref, sink_ref, o_ref)


def _attn_b_sample(qkv, k_new, v_new, cache_k, cache_v, bias, sinks, *, d_model):
    nb, cl, kv_width = cache_k.shape
    t = k_new.shape[1]
    cache_spec = pl.BlockSpec((1, cl, kv_width), lambda b: (b, 0, 0))
    new_spec = pl.BlockSpec((1, t, kv_width), lambda b: (b, 0, 0))
    return pl.pallas_call(
        _attn_b_sample_kernel,
        grid=(nb,),
        in_specs=[
            pl.BlockSpec((t, d_model), lambda b: (b, 0)),
            cache_spec, cache_spec, new_spec, new_spec,
            pl.BlockSpec(bias.shape, lambda b: (0, 0, 0, 0)),
            pl.BlockSpec(memory_space=pltpu.SMEM),
        ],
        out_specs=[pl.BlockSpec((t, d_model), lambda b: (b, 0)), cache_spec, cache_spec],
        out_shape=[
            jax.ShapeDtypeStruct((nb * t, d_model), BF16),
            jax.ShapeDtypeStruct(cache_k.shape, F32),
            jax.ShapeDtypeStruct(cache_v.shape, F32),
        ],
        compiler_params=pltpu.CompilerParams(
            dimension_semantics=("parallel",),
            vmem_limit_bytes=VMEM_LIMIT_BYTES),
        name="attn_b_sample",
    )(qkv, cache_k, cache_v, k_new, v_new, bias, sinks)


def _oproj_mlp_kernel(x_ref, a_ref, wo_ref, g_ref, wup_ref, wdn_ref, gfin_ref,
                      o_ref, h_ref, *, final_norm):
    f = pl.program_id(1)

    @pl.when(f == 0)
    def _():
        x1 = x_ref[...] + jnp.dot(a_ref[...], wo_ref[...], preferred_element_type=F32)
        o_ref[...] = x1
        h_ref[...] = (_rms_scale(x1) * g_ref[...]).astype(BF16)

    u = jnp.dot(h_ref[...], wup_ref[...], preferred_element_type=F32)
    act = jnp.square(jnp.maximum(u, 0.0)).astype(BF16)
    o_ref[...] += jnp.dot(act, wdn_ref[...], preferred_element_type=F32)

    if final_norm:
        @pl.when(f == pl.num_programs(1) - 1)
        def _():
            o_ref[...] = _rms_scale(o_ref[...]) * gfin_ref[...]


def _oproj_mlp(x, attn, wo, slot, g, wup, wdn, layer, gfin, *, tm, tf, final_norm):
    rows, d = x.shape
    d_ff = wup.shape[2]
    kernel = functools.partial(_oproj_mlp_kernel, final_norm=final_norm)
    return pl.pallas_call(
        kernel,
        grid=(rows // tm, d_ff // tf),
        in_specs=[
            pl.BlockSpec((tm, d), lambda m, f: (m, 0)),
            pl.BlockSpec((tm, d), lambda m, f: (m, 0)),
            pl.BlockSpec((None, d, d), lambda m, f: (slot, 0, 0), pipeline_mode=pl.Buffered(1)),
            pl.BlockSpec((1, d), lambda m, f: (0, 0)),
            pl.BlockSpec((None, d, tf), lambda m, f: (layer, 0, f)),
            pl.BlockSpec((None, tf, d), lambda m, f: (layer, f, 0)),
            pl.BlockSpec((1, d), lambda m, f: (0, 0)),
        ],
        out_specs=pl.BlockSpec((tm, d), lambda m, f: (m, 0)),
        out_shape=jax.ShapeDtypeStruct((rows, d), F32),
        scratch_shapes=[pltpu.VMEM((tm, d), BF16)],
        compiler_params=pltpu.CompilerParams(
            dimension_semantics=("parallel", "arbitrary"),
            vmem_limit_bytes=VMEM_LIMIT_BYTES),
        name="oproj_mlp",
    )(x, attn, wo, g, wup, wdn, gfin)


def _band_mask(n_q_chunks, n_k_chunks, n_prev):
    ci = np.arange(n_q_chunks * CHUNK)[:, None] // CHUNK
    cj = np.arange(n_k_chunks * CHUNK)[None, :] // CHUNK
    return (cj >= ci) & (cj <= ci + n_prev)


def _rel_bias_tile(table, n_rows, n_cols, key_offset):
    length = n_rows + n_cols
    rel = np.arange(length) - (n_rows - 1) - key_offset
    diag = table[:, np.clip(rel, -A_REL_CLIP, A_REL_CLIP) + A_REL_CLIP]
    flat = jnp.tile(diag, (1, n_rows))[:, :n_rows * (length - 1)]
    return flat.reshape(table.shape[0], n_rows, length - 1)[:, :, n_rows - 1:n_rows - 1 + n_cols]


def _a_prompt_bias(table):
    bias = _rel_bias_tile(table, A_TQ, 3 * A_TQ, 2 * A_TQ)
    band = _band_mask(A_TQ // CHUNK, 3 * A_TQ // CHUNK, A_PREV_CHUNKS)
    col = np.arange(3 * A_TQ)[None, :]
    masks = np.stack([band & (col >= (2 - v) * A_TQ) for v in range(3)])
    return jnp.where(masks[:, None], bias[None], NEG_INF)


def _a_sample_bias(table, t, cl):
    bias = _rel_bias_tile(table, t, cl + NEW_KEY_PAD, cl)
    valid = np.broadcast_to(np.arange(cl + NEW_KEY_PAD)[None, :] < cl + t, (t, cl + NEW_KEY_PAD))
    return jnp.where(valid[None], bias, NEG_INF)


def _pair_heads(x, axis):
    n_heads = x.shape[axis]
    group = n_heads // B_KV_HEADS
    split = x.shape[:axis] + (B_KV_HEADS // 2, 2, group) + x.shape[axis + 1:]
    return jnp.swapaxes(x.reshape(split), axis + 1, axis + 2).reshape(x.shape)


def _alibi_tile(n_q_heads, n_rows, n_cols, key_offset, valid):
    group = n_q_heads // B_KV_HEADS
    slopes = (2.0 ** (-8.0 * np.arange(1, n_q_heads + 1) / n_q_heads)).astype(np.float32)
    qi = np.arange(n_rows)[:, None]
    kj = np.arange(n_cols)[None, :]
    rel = np.abs(kj - key_offset - qi).astype(np.float32)
    alibi = (-slopes[:, None, None] * rel[None]).astype(np.float64)
    bias = np.where(valid[None], alibi * LOG2E, NEG_INF)
    stacked = [[bias[group * (2 * pair + side) + g] for side in range(2) for g in range(group)]
               for pair in range(B_KV_HEADS // 2)]
    return np.asarray(stacked, np.float32).reshape(B_KV_HEADS // 2, 2 * group * n_rows, n_cols)


def _b_prompt_bias(n_q_heads):
    mask = _band_mask(B_TQ // CHUNK, 2 * B_TQ // CHUNK, B_PREV_CHUNKS)
    first = mask & (np.arange(2 * B_TQ)[None, :] >= B_TQ)
    return np.stack([_alibi_tile(n_q_heads, B_TQ, 2 * B_TQ, B_TQ, mask),
                     _alibi_tile(n_q_heads, B_TQ, 2 * B_TQ, B_TQ, first)])


def _b_sample_bias(n_q_heads, t, cl):
    valid = np.broadcast_to(np.arange(cl + NEW_KEY_PAD)[None, :] < cl + t, (t, cl + NEW_KEY_PAD))
    return _alibi_tile(n_q_heads, t, cl + NEW_KEY_PAD, cl, valid)[None]


def kernel(x_prompt, x_sample, cache_a_k, cache_a_v, cache_b_k, cache_b_v, norm_mix, norm_ffn,
           norm_final, a_w_qkv, a_w_o, a_rel_bias, b_w_qkv, b_w_o, b_sinks, w_up, w_down):
    batch, seq, d = x_prompt.shape
    dec_batch, dec_seq, _ = x_sample.shape
    depth = norm_mix.shape[0]
    a_kv_width = cache_a_k.shape[3] * cache_a_k.shape[4]
    b_kv_width = cache_b_k.shape[3] * cache_b_k.shape[4]
    cl_a, cl_b = cache_a_k.shape[2], cache_b_k.shape[2]
    n_b_heads = d // B_HEAD_DIM
    tail_a = min(A_PREV_CHUNKS * CHUNK, seq)
    tail_b = min(B_PREV_CHUNKS * CHUNK, seq)

    tm = 512
    tm_qkv = 1024
    tf = 1024
    rows_s = dec_batch * dec_seq

    n_slots = b_w_qkv.shape[0]
    a_w_qkv16 = a_w_qkv.astype(BF16)
    a_w_o16 = a_w_o.astype(BF16)
    b_q16 = _pair_heads(b_w_qkv[:, :, :d].astype(BF16).reshape(n_slots, d, n_b_heads, B_HEAD_DIM), 2)
    b_w_qkv16 = jnp.concatenate(
        [b_q16.reshape(n_slots, d, d), b_w_qkv[:, :, d:].astype(BF16)], axis=2)
    b_w_o16 = _pair_heads(
        b_w_o.astype(BF16).reshape(n_slots, n_b_heads, B_HEAD_DIM, d), 1).reshape(n_slots, d, d)
    w_up16 = w_up.astype(BF16)
    w_down16 = w_down.astype(BF16)
    sinks_perm = _pair_heads(b_sinks, 1) * LOG2E
    a_tables = a_rel_bias * LOG2E

    b_bias_prompt = jnp.asarray(_b_prompt_bias(n_b_heads))
    b_bias_sample = jnp.asarray(_b_sample_bias(n_b_heads, dec_seq, cl_b))

    xp = x_prompt.reshape(batch * seq, d)
    xs = x_sample.reshape(rows_s, d)
    gfin = norm_final.reshape(1, d)

    states = {k: [] for k in ("a_kp", "a_vp", "a_ks", "a_vs", "b_kp", "b_vp", "b_ks", "b_vs")}
    for layer in range(depth):
        slot = layer // N_MIXERS
        g_mix = norm_mix[layer].reshape(1, d)
        g_ffn = norm_ffn[layer].reshape(1, d)
        if layer % N_MIXERS == 0:
            qkv_args = dict(tn=1024, q_width=d, kv_width=a_kv_width,
                            q_scale=A_HEAD_DIM ** -0.5 * LOG2E)
            qkv_p, kp, vp = _norm_qkv(xp, g_mix, a_w_qkv16, slot, tm=tm_qkv,
                                      tiles_per_batch=seq // tm_qkv, tail_rows=tail_a, **qkv_args)
            qkv_s, kn, vn = _norm_qkv(xs, g_mix, a_w_qkv16, slot, tm=rows_s,
                                      tiles_per_batch=1, tail_rows=rows_s, **qkv_args)
            table = a_tables[slot]
            mp = _attn_a_prompt(qkv_p, _a_prompt_bias(table), batch=batch, seq=seq, d_model=d)
            ms, ks, vs = _attn_a_sample(
                qkv_s, kn.reshape(dec_batch, dec_seq, a_kv_width),
                vn.reshape(dec_batch, dec_seq, a_kv_width),
                cache_a_k[slot].reshape(dec_batch, cl_a, a_kv_width),
                cache_a_v[slot].reshape(dec_batch, cl_a, a_kv_width),
                _a_sample_bias(table, dec_seq, cl_a), d_model=d)
            w_o = a_w_o16
            states["a_kp"].append(kp.reshape((batch, tail_a) + cache_a_k.shape[3:]))
            states["a_vp"].append(vp.reshape((batch, tail_a) + cache_a_k.shape[3:]))
            states["a_ks"].append(ks.reshape(cache_a_k.shape[1:]))
            states["a_vs"].append(vs.reshape(cache_a_k.shape[1:]))
        else:
            qkv_args = dict(tn=2 * b_kv_width, q_width=d, kv_width=b_kv_width,
                            q_scale=B_HEAD_DIM ** -0.5 * LOG2E)
            qkv_p, kp, vp = _norm_qkv(xp, g_mix, b_w_qkv16, slot, tm=tm_qkv,
                                      tiles_per_batch=seq // tm_qkv, tail_rows=tail_b, **qkv_args)
            qkv_s, kn, vn = _norm_qkv(xs, g_mix, b_w_qkv16, slot, tm=rows_s,
                                      tiles_per_batch=1, tail_rows=rows_s, **qkv_args)
            sinks = sinks_perm[slot].reshape(1, n_b_heads)
            mp = _attn_b_prompt(qkv_p, b_bias_prompt, sinks, batch=batch, seq=seq,
                                d_model=d, kv_width=b_kv_width)
            ms, ks, vs = _attn_b_sample(
                qkv_s, kn.reshape(dec_batch, dec_seq, b_kv_width),
                vn.reshape(dec_batch, dec_seq, b_kv_width),
                cache_b_k[slot].reshape(dec_batch, cl_b, b_kv_width),
                cache_b_v[slot].reshape(dec_batch, cl_b, b_kv_width),
                b_bias_sample, sinks, d_model=d)
            w_o = b_w_o16
            states["b_kp"].append(kp.reshape((batch, tail_b) + cache_b_k.shape[3:]))
            states["b_vp"].append(vp.reshape((batch, tail_b) + cache_b_k.shape[3:]))
            states["b_ks"].append(ks.reshape(cache_b_k.shape[1:]))
            states["b_vs"].append(vs.reshape(cache_b_k.shape[1:]))
        last = layer == depth - 1
        xp = _oproj_mlp(xp, mp, w_o, slot, g_ffn, w_up16, w_down16, layer, gfin,
                        tm=tm, tf=tf, final_norm=last)
        xs = _oproj_mlp(xs, ms, w_o, slot, g_ffn, w_up16, w_down16, layer, gfin,
                        tm=rows_s, tf=tf, final_norm=last)

    return (xp.reshape(batch, seq, d), xs.reshape(dec_batch, dec_seq, d),
            jnp.stack(states["a_kp"]), jnp.stack(states["a_vp"]),
            jnp.stack(states["b_kp"]), jnp.stack(states["b_vp"]),
            jnp.stack(states["a_ks"]), jnp.stack(states["a_vs"]),
            jnp.stack(states["b_ks"]), jnp.stack(states["b_vs"]))
```

```python
import functools

import jax
import jax.numpy as jnp
import numpy as np
from jax import lax
from jax.experimental import pallas as pl
from jax.experimental.pallas import tpu as pltpu

F32 = jnp.float32
BF16 = jnp.bfloat16

CHUNK = 64
N_MIXERS = 2
A_HEAD_DIM = 128
A_PREV_CHUNKS = 8
A_REL_CLIP = 128
B_PREV_CHUNKS = 2
B_HEAD_DIM = 64
B_KV_HEADS = 8
RMS_EPS = 1e-6
NEG_INF = -1e30
LOG2E = float(np.log2(np.e))

LANES = 128
VMEM_LIMIT_BYTES = 56 * 1024 * 1024

A_TQ = 256
A_HEADS_PER_STEP = 8
B_TQ = 128
NEW_KEY_PAD = 128


def _rms_scale(x):
    return x * lax.rsqrt(jnp.mean(x * x, axis=-1, keepdims=True) + RMS_EPS)


def _inv_rms(x):
    r = lax.rsqrt(jnp.mean(x * x, axis=-1, keepdims=True) + RMS_EPS)
    return jnp.broadcast_to(r, (x.shape[0], LANES))


def _scale_rows(acc, r):
    return jnp.concatenate(
        [acc[:, c * LANES:(c + 1) * LANES] * r for c in range(acc.shape[1] // LANES)], axis=1)


def _dot_nt(a, b):
    return lax.dot_general(a, b, (((1,), (1,)), ((), ())), preferred_element_type=F32)


def _softmax_pv(s, v):
    m = s.max(axis=-1, keepdims=True)
    p = jnp.exp2(s - m).astype(BF16)
    out = jnp.dot(p, jnp.concatenate([v, jnp.ones_like(v)], axis=1), preferred_element_type=F32)
    width = v.shape[1]
    return out[:, :width] / out[:, width:]


def _norm_qkv_kernel(x_ref, g_ref, w_ref, qkv_ref, kst_ref, vst_ref, h_ref, r_ref, *,
                     nq, nk, q_scale, tiles_per_batch, tail_rows):
    m = pl.program_id(0)
    n = pl.program_id(1)

    @pl.when(n == 0)
    def _():
        x = x_ref[...]
        h_ref[...] = (x * g_ref[...]).astype(BF16)
        r_ref[...] = _inv_rms(x)

    acc = _scale_rows(jnp.dot(h_ref[...], w_ref[...], preferred_element_type=F32), r_ref[...])
    qkv_ref[...] = (acc * jnp.where(n < nq, q_scale, 1.0)).astype(BF16)

    tm = acc.shape[0]
    is_tail = (m % tiles_per_batch) == tiles_per_batch - 1
    tail = acc[tm - tail_rows:, :]

    if nk == 0:
        @pl.when(is_tail & (n == nq))
        def _():
            kv_width = kst_ref.shape[2]
            kst_ref[0] = tail[:, :kv_width]
            vst_ref[0] = tail[:, kv_width:]
    else:
        @pl.when(is_tail & (n >= nq) & (n < nq + nk))
        def _():
            kst_ref[0] = tail

        @pl.when(is_tail & (n >= nq + nk))
        def _():
            vst_ref[0] = tail


def _norm_qkv(x, g, w, layer, *, tm, tn, q_width, kv_width, q_scale, tiles_per_batch, tail_rows):
    rows, d = x.shape
    n_total = w.shape[2]
    nq = q_width // tn
    nk = kv_width // tn
    assert (nk > 0 and kv_width % tn == 0) or tn == 2 * kv_width
    state_width = min(tn, kv_width)
    nb = rows // (tm * tiles_per_batch)

    def state_map(first):
        def index_map(m, n):
            tail = (m % tiles_per_batch) == tiles_per_batch - 1
            j = jnp.clip(n - first, 0, max(nk - 1, 0))
            return (m // tiles_per_batch, 0, jnp.where(tail, j, 0))
        return index_map

    kernel = functools.partial(_norm_qkv_kernel, nq=nq, nk=nk, q_scale=q_scale,
                               tiles_per_batch=tiles_per_batch, tail_rows=tail_rows)
    return pl.pallas_call(
        kernel,
        grid=(rows // tm, n_total // tn),
        in_specs=[
            pl.BlockSpec((tm, d), lambda m, n: (m, 0)),
            pl.BlockSpec((1, d), lambda m, n: (0, 0)),
            pl.BlockSpec((None, d, tn), lambda m, n: (layer, 0, n)),
        ],
        out_specs=[
            pl.BlockSpec((tm, tn), lambda m, n: (m, n)),
            pl.BlockSpec((1, tail_rows, state_width), state_map(nq)),
            pl.BlockSpec((1, tail_rows, state_width), state_map(nq + nk)),
        ],
        out_shape=[
            jax.ShapeDtypeStruct((rows, n_total), BF16),
            jax.ShapeDtypeStruct((nb, tail_rows, kv_width), F32),
            jax.ShapeDtypeStruct((nb, tail_rows, kv_width), F32),
        ],
        scratch_shapes=[pltpu.VMEM((tm, d), BF16), pltpu.VMEM((tm, LANES), F32)],
        compiler_params=pltpu.CompilerParams(
            dimension_semantics=("arbitrary", "arbitrary"),
            vmem_limit_bytes=VMEM_LIMIT_BYTES),
        name="norm_qkv",
    )(x, g, w)


def _attn_a_prompt_kernel(q_ref, k0_ref, k1_ref, k2_ref, v0_ref, v1_ref, v2_ref,
                          bias_ref, o_ref):
    k_refs = (k0_ref, k1_ref, k2_ref)
    v_refs = (v0_ref, v1_ref, v2_ref)
    n_heads = q_ref.shape[1] // A_HEAD_DIM

    def head_slice(h):
        return slice(h * A_HEAD_DIM, (h + 1) * A_HEAD_DIM)

    def scores(h):
        k = jnp.concatenate([r[:, head_slice(h)] for r in k_refs], axis=0)
        return _dot_nt(q_ref[:, head_slice(h)], k) + bias_ref[0, h]

    s_next = scores(0)
    for h in range(n_heads):
        s = s_next
        if h + 1 < n_heads:
            s_next = scores(h + 1)
        v = jnp.concatenate([r[:, head_slice(h)] for r in v_refs], axis=0)
        o_ref[:, head_slice(h)] = _softmax_pv(s, v).astype(BF16)


def _attn_a_prompt(qkv, bias, *, batch, seq, d_model):
    tq = A_TQ
    hw = A_HEADS_PER_STEP * A_HEAD_DIM
    nhb = d_model // hw
    nqb = seq // tq

    def kv_spec(back, col0):
        return pl.BlockSpec(
            (tq, hw), lambda b, h, i: (b * nqb + jnp.maximum(i - back, 0), col0 + h))

    return pl.pallas_call(
        _attn_a_prompt_kernel,
        grid=(batch, nhb, nqb),
        in_specs=[
            pl.BlockSpec((tq, hw), lambda b, h, i: (b * nqb + i, h)),
            kv_spec(2, nhb), kv_spec(1, nhb), kv_spec(0, nhb),
            kv_spec(2, 2 * nhb), kv_spec(1, 2 * nhb), kv_spec(0, 2 * nhb),
            pl.BlockSpec((1, A_HEADS_PER_STEP, tq, 3 * tq),
                         lambda b, h, i: (jnp.minimum(i, 2), h, 0, 0)),
        ],
        out_specs=pl.BlockSpec((tq, hw), lambda b, h, i: (b * nqb + i, h)),
        out_shape=jax.ShapeDtypeStruct((batch * seq, d_model), BF16),
        compiler_params=pltpu.CompilerParams(
            dimension_semantics=("parallel", "parallel", "parallel"),
            vmem_limit_bytes=VMEM_LIMIT_BYTES),
        name="attn_a_prompt",
    )(qkv, qkv, qkv, qkv, qkv, qkv, qkv, bias)


def _attn_a_sample_kernel(q_ref, kc_ref, vc_ref, kn_ref, vn_ref, bias_ref,
                          o_ref, ko_ref, vo_ref):
    t = q_ref.shape[0]
    cl = kc_ref.shape[1]
    kc, vc, kn, vn = kc_ref[0], vc_ref[0], kn_ref[0], vn_ref[0]
    ko_ref[0, :cl - t, :] = kc[t:, :]
    ko_ref[0, cl - t:, :] = kn
    vo_ref[0, :cl - t, :] = vc[t:, :]
    vo_ref[0, cl - t:, :] = vn
    pad = jnp.zeros((NEW_KEY_PAD - t, A_HEAD_DIM), BF16)
    for h in range(q_ref.shape[1] // A_HEAD_DIM):
        sl = slice(h * A_HEAD_DIM, (h + 1) * A_HEAD_DIM)
        k = jnp.concatenate([kc[:, sl].astype(BF16), kn[:, sl].astype(BF16), pad], axis=0)
        v = jnp.concatenate([vc[:, sl].astype(BF16), vn[:, sl].astype(BF16), pad], axis=0)
        s = _dot_nt(q_ref[:, sl], k) + bias_ref[h]
        o_ref[:, sl] = _softmax_pv(s, v).astype(BF16)


def _attn_a_sample(qkv, k_new, v_new, cache_k, cache_v, bias, *, d_model):
    nb, cl, _ = cache_k.shape
    t = k_new.shape[1]
    hw = A_HEADS_PER_STEP * A_HEAD_DIM
    nhb = d_model // hw
    cache_spec = pl.BlockSpec((1, cl, hw), lambda b, h: (b, 0, h))
    new_spec = pl.BlockSpec((1, t, hw), lambda b, h: (b, 0, h))
    return pl.pallas_call(
        _attn_a_sample_kernel,
        grid=(nb, nhb),
        in_specs=[
            pl.BlockSpec((t, hw), lambda b, h: (b, h)),
            cache_spec, cache_spec, new_spec, new_spec,
            pl.BlockSpec((A_HEADS_PER_STEP, t, cl + NEW_KEY_PAD), lambda b, h: (h, 0, 0)),
        ],
        out_specs=[pl.BlockSpec((t, hw), lambda b, h: (b, h)), cache_spec, cache_spec],
        out_shape=[
            jax.ShapeDtypeStruct((nb * t, d_model), BF16),
            jax.ShapeDtypeStruct(cache_k.shape, F32),
            jax.ShapeDtypeStruct(cache_v.shape, F32),
        ],
        compiler_params=pltpu.CompilerParams(
            dimension_semantics=("parallel", "parallel"),
            vmem_limit_bytes=VMEM_LIMIT_BYTES),
        name="attn_a_sample",
    )(qkv, cache_k, cache_v, k_new, v_new, bias)


def _attn_b_heads(q_ref, k_blocks, v_blocks, bias_ref, sink_ref, o_ref):
    tq = q_ref.shape[0]
    n_pairs = B_KV_HEADS // 2
    bpp = q_ref.shape[1] // LANES // n_pairs
    half = bpp * tq
    low = lax.broadcasted_iota(jnp.int32, (tq, LANES), 1) < B_HEAD_DIM

    def scores(pair):
        pair_sl = slice(pair * LANES, (pair + 1) * LANES)
        k = jnp.concatenate([kb(pair_sl) for kb in k_blocks], axis=0)
        q_cols = [q_ref[:, (pair * bpp + c) * LANES:(pair * bpp + c + 1) * LANES]
                  for c in range(bpp)]
        zero = jnp.zeros_like(q_cols[0])
        q_stack = jnp.concatenate(
            [jnp.where(low, qc, zero) for qc in q_cols]
            + [jnp.where(low, zero, qc) for qc in q_cols], axis=0)
        return _dot_nt(q_stack, k) + bias_ref[0, pair]

    def softmax_pv(pair, s):
        pair_sl = slice(pair * LANES, (pair + 1) * LANES)
        v = jnp.concatenate([vb(pair_sl) for vb in v_blocks], axis=0)
        m = s.max(axis=-1, keepdims=True)
        p = jnp.exp2(s - m).astype(BF16)
        low_k = lax.broadcasted_iota(jnp.int32, v.shape, 1) < B_HEAD_DIM
        one = jnp.ones_like(v)
        out_even = jnp.dot(p[:half], jnp.where(low_k, v, one), preferred_element_type=F32)
        out_odd = jnp.dot(p[half:], jnp.where(low_k, one, v), preferred_element_type=F32)
        return m, out_even, out_odd

    def finish(pair, m, out_even, out_odd):
        for c in range(bpp):
            rows = slice(c * tq, (c + 1) * tq)
            oe, oo = out_even[rows], out_odd[rows]
            num = jnp.where(low, oe, oo)
            den = pltpu.roll(jnp.where(low, oo, oe), B_HEAD_DIM, axis=1)
            cb = pair * bpp + c
            sink_even = jnp.exp2(sink_ref[0, 2 * cb] - m[rows])
            sink_odd = jnp.exp2(sink_ref[0, 2 * cb + 1] - m[half + c * tq:half + (c + 1) * tq])
            den = den + jnp.where(low, sink_even, sink_odd)
            o_ref[:, cb * LANES:(cb + 1) * LANES] = (num / den).astype(BF16)

    s_vals, pv_vals = {}, {}
    for t in range(n_pairs + 2):
        if t < n_pairs:
            s_vals[t] = scores(t)
        if 0 <= t - 1 < n_pairs:
            pv_vals[t - 1] = softmax_pv(t - 1, s_vals.pop(t - 1))
        if 0 <= t - 2 < n_pairs:
            finish(t - 2, *pv_vals.pop(t - 2))


def _attn_b_prompt_kernel(q_ref, k0_ref, k1_ref, v0_ref, v1_ref, bias_ref, sink_ref, o_ref):
    _attn_b_heads(
        q_ref,
        [lambda sl: k0_ref[:, sl], lambda sl: k1_ref[:, sl]],
        [lambda sl: v0_ref[:, sl], lambda sl: v1_ref[:, sl]],
        bias_ref, sink_ref, o_ref)


def _attn_b_prompt(qkv, bias, sinks, *, batch, seq, d_model, kv_width):
    tq = B_TQ
    nqb = seq // tq
    nq_cols = d_model // kv_width

    def kv_spec(back, col):
        return pl.BlockSpec(
            (tq, kv_width), lambda b, i: (b * nqb + jnp.maximum(i - back, 0), col))

    return pl.pallas_call(
        _attn_b_prompt_kernel,
        grid=(batch, nqb),
        in_specs=[
            pl.BlockSpec((tq, d_model), lambda b, i: (b * nqb + i, 0)),
            kv_spec(1, nq_cols), kv_spec(0, nq_cols),
            kv_spec(1, nq_cols + 1), kv_spec(0, nq_cols + 1),
            pl.BlockSpec((1,) + bias.shape[1:],
                         lambda b, i: (jnp.where(i == 0, 1, 0), 0, 0, 0)),
            pl.BlockSpec(memory_space=pltpu.SMEM),
        ],
        out_specs=pl.BlockSpec((tq, d_model), lambda b, i: (b * nqb + i, 0)),
        out_shape=jax.ShapeDtypeStruct((batch * seq, d_model), BF16),
        compiler_params=pltpu.CompilerParams(
            dimension_semantics=("parallel", "parallel"),
            vmem_limit_bytes=VMEM_LIMIT_BYTES),
        name="attn_b_prompt",
    )(qkv, qkv, qkv, qkv, qkv, bias, sinks)


def _attn_b_sample_kernel(q_ref, kc_ref, vc_ref, kn_ref, vn_ref, bias_ref, sink_ref,
                          o_ref, ko_ref, vo_ref):
    t = q_ref.shape[0]
    cl = kc_ref.shape[1]
    kc, vc, kn, vn = kc_ref[0], vc_ref[0], kn_ref[0], vn_ref[0]
    ko_ref[0, :cl - t, :] = kc[t:, :]
    ko_ref[0, cl - t:, :] = kn
    vo_ref[0, :cl - t, :] = vc[t:, :]
    vo_ref[0, cl - t:, :] = vn
    pad = jnp.zeros((NEW_KEY_PAD - t, LANES), BF16)
    _attn_b_heads(
        q_ref,
        [lambda sl: kc[:, sl].astype(BF16),
         lambda sl: jnp.concatenate([kn[:, sl].astype(BF16), pad], axis=0)],
        [lambda sl: vc[:, sl].astype(BF16),
         lambda sl: jnp.concatenate([vn[:, sl].astype(BF16), pad], axis=0)],
        bias_ref, sink_ref, o_ref)


def _attn_b_sample(qkv, k_new, v_new, cache_k, cache_v, bias, sinks, *, d_model):
    nb, cl, kv_width = cache_k.shape
    t = k_new.shape[1]
    cache_spec = pl.BlockSpec((1, cl, kv_width), lambda b: (b, 0, 0))
    new_spec = pl.BlockSpec((1, t, kv_width), lambda b: (b, 0, 0))
    return pl.pallas_call(
        _attn_b_sample_kernel,
        grid=(nb,),
        in_specs=[
            pl.BlockSpec((t, d_model), lambda b: (b, 0)),
            cache_spec, cache_spec, new_spec, new_spec,
            pl.BlockSpec(bias.shape, lambda b: (0, 0, 0, 0)),
            pl.BlockSpec(memory_space=pltpu.SMEM),
        ],
        out_specs=[pl.BlockSpec((t, d_model), lambda b: (b, 0)), cache_spec, cache_spec],
        out_shape=[
            jax.ShapeDtypeStruct((nb * t, d_model), BF16),
            jax.ShapeDtypeStruct(cache_k.shape, F32),
            jax.ShapeDtypeStruct(cache_v.shape, F32),
        ],
        compiler_params=pltpu.CompilerParams(
            dimension_semantics=("parallel",),
            vmem_limit_bytes=VMEM_LIMIT_BYTES),
        name="attn_b_sample",
    )(qkv, cache_k, cache_v, k_new, v_new, bias, sinks)


def _oproj_mlp_kernel(x_ref, a_ref, wo_ref, g_ref, wup_ref, wdn_ref, gfin_ref,
                      o_ref, h_ref, r_ref, *, final_norm):
    f = pl.program_id(1)

    @pl.when(f == 0)
    def _():
        x1 = x_ref[...] + jnp.dot(a_ref[...], wo_ref[...], preferred_element_type=F32)
        o_ref[...] = x1
        h_ref[...] = (x1 * g_ref[...]).astype(BF16)
        r_ref[...] = _inv_rms(x1)

    u = _scale_rows(jnp.dot(h_ref[...], wup_ref[...], preferred_element_type=F32), r_ref[...])
    act = jnp.square(jnp.maximum(u, 0.0)).astype(BF16)
    o_ref[...] += jnp.dot(act, wdn_ref[...], preferred_element_type=F32)

    if final_norm:
        @pl.when(f == pl.num_programs(1) - 1)
        def _():
            o_ref[...] = _rms_scale(o_ref[...]) * gfin_ref[...]


def _oproj_mlp(x, attn, wo, slot, g, wup, wdn, layer, gfin, *, tm, tf, final_norm):
    rows, d = x.shape
    d_ff = wup.shape[2]
    kernel = functools.partial(_oproj_mlp_kernel, final_norm=final_norm)
    return pl.pallas_call(
        kernel,
        grid=(rows // tm, d_ff // tf),
        in_specs=[
            pl.BlockSpec((tm, d), lambda m, f: (m, 0)),
            pl.BlockSpec((tm, d), lambda m, f: (m, 0)),
            pl.BlockSpec((None, d, d), lambda m, f: (slot, 0, 0), pipeline_mode=pl.Buffered(1)),
            pl.BlockSpec((1, d), lambda m, f: (0, 0)),
            pl.BlockSpec((None, d, tf), lambda m, f: (layer, 0, f)),
            pl.BlockSpec((None, tf, d), lambda m, f: (layer, f, 0)),
            pl.BlockSpec((1, d), lambda m, f: (0, 0)),
        ],
        out_specs=pl.BlockSpec((tm, d), lambda m, f: (m, 0)),
        out_shape=jax.ShapeDtypeStruct((rows, d), F32),
        scratch_shapes=[pltpu.VMEM((tm, d), BF16), pltpu.VMEM((tm, LANES), F32)],
        compiler_params=pltpu.CompilerParams(
            dimension_semantics=("parallel", "arbitrary"),
            vmem_limit_bytes=VMEM_LIMIT_BYTES),
        name="oproj_mlp",
    )(x, attn, wo, g, wup, wdn, gfin)


def _band_mask(n_q_chunks, n_k_chunks, n_prev):
    ci = np.arange(n_q_chunks * CHUNK)[:, None] // CHUNK
    cj = np.arange(n_k_chunks * CHUNK)[None, :] // CHUNK
    return (cj >= ci) & (cj <= ci + n_prev)


def _rel_bias_tile(table, n_rows, n_cols, key_offset):
    length = n_rows + n_cols
    rel = np.arange(length) - (n_rows - 1) - key_offset
    diag = table[:, np.clip(rel, -A_REL_CLIP, A_REL_CLIP) + A_REL_CLIP]
    flat = jnp.tile(diag, (1, n_rows))[:, :n_rows * (length - 1)]
    return flat.reshape(table.shape[0], n_rows, length - 1)[:, :, n_rows - 1:n_rows - 1 + n_cols]


def _a_prompt_bias(table):
    bias = _rel_bias_tile(table, A_TQ, 3 * A_TQ, 2 * A_TQ)
    band = _band_mask(A_TQ // CHUNK, 3 * A_TQ // CHUNK, A_PREV_CHUNKS)
    col = np.arange(3 * A_TQ)[None, :]
    masks = np.stack([band & (col >= (2 - v) * A_TQ) for v in range(3)])
    return jnp.where(masks[:, None], bias[None], NEG_INF)


def _a_sample_bias(table, t, cl):
    bias = _rel_bias_tile(table, t, cl + NEW_KEY_PAD, cl)
    valid = np.broadcast_to(np.arange(cl + NEW_KEY_PAD)[None, :] < cl + t, (t, cl + NEW_KEY_PAD))
    return jnp.where(valid[None], bias, NEG_INF)


def _pair_heads(x, axis):
    n_heads = x.shape[axis]
    group = n_heads // B_KV_HEADS
    split = x.shape[:axis] + (B_KV_HEADS // 2, 2, group) + x.shape[axis + 1:]
    return jnp.swapaxes(x.reshape(split), axis + 1, axis + 2).reshape(x.shape)


def _alibi_tile(n_q_heads, n_rows, n_cols, key_offset, valid):
    group = n_q_heads // B_KV_HEADS
    slopes = (2.0 ** (-8.0 * np.arange(1, n_q_heads + 1) / n_q_heads)).astype(np.float32)
    qi = np.arange(n_rows)[:, None]
    kj = np.arange(n_cols)[None, :]
    rel = np.abs(kj - key_offset - qi).astype(np.float32)
    alibi = (-slopes[:, None, None] * rel[None]).astype(np.float64)
    bias = np.where(valid[None], alibi * LOG2E, NEG_INF)
    stacked = [[bias[group * (2 * pair + side) + g] for side in range(2) for g in range(group)]
               for pair in range(B_KV_HEADS // 2)]
    return np.asarray(stacked, np.float32).reshape(B_KV_HEADS // 2, 2 * group * n_rows, n_cols)


def _b_prompt_bias(n_q_heads):
    mask = _band_mask(B_TQ // CHUNK, 2 * B_TQ // CHUNK, B_PREV_CHUNKS)
    first = mask & (np.arange(2 * B_TQ)[None, :] >= B_TQ)
    return np.stack([_alibi_tile(n_q_heads, B_TQ, 2 * B_TQ, B_TQ, mask),
                     _alibi_tile(n_q_heads, B_TQ, 2 * B_TQ, B_TQ, first)])


def _b_sample_bias(n_q_heads, t, cl):
    valid = np.broadcast_to(np.arange(cl + NEW_KEY_PAD)[None, :] < cl + t, (t, cl + NEW_KEY_PAD))
    return _alibi_tile(n_q_heads, t, cl + NEW_KEY_PAD, cl, valid)[None]


def kernel(x_prompt, x_sample, cache_a_k, cache_a_v, cache_b_k, cache_b_v, norm_mix, norm_ffn,
           norm_final, a_w_qkv, a_w_o, a_rel_bias, b_w_qkv, b_w_o, b_sinks, w_up, w_down):
    batch, seq, d = x_prompt.shape
    dec_batch, dec_seq, _ = x_sample.shape
    depth = norm_mix.shape[0]
    a_kv_width = cache_a_k.shape[3] * cache_a_k.shape[4]
    b_kv_width = cache_b_k.shape[3] * cache_b_k.shape[4]
    cl_a, cl_b = cache_a_k.shape[2], cache_b_k.shape[2]
    n_b_heads = d // B_HEAD_DIM
    tail_a = min(A_PREV_CHUNKS * CHUNK, seq)
    tail_b = min(B_PREV_CHUNKS * CHUNK, seq)

    tm = 512
    tm_qkv = 1024
    tf = 1024
    rows_s = dec_batch * dec_seq

    n_slots = b_w_qkv.shape[0]
    a_w_qkv16 = a_w_qkv.astype(BF16)
    a_w_o16 = a_w_o.astype(BF16)
    b_q16 = _pair_heads(b_w_qkv[:, :, :d].astype(BF16).reshape(n_slots, d, n_b_heads, B_HEAD_DIM), 2)
    b_w_qkv16 = jnp.concatenate(
        [b_q16.reshape(n_slots, d, d), b_w_qkv[:, :, d:].astype(BF16)], axis=2)
    b_w_o16 = _pair_heads(
        b_w_o.astype(BF16).reshape(n_slots, n_b_heads, B_HEAD_DIM, d), 1).reshape(n_slots, d, d)
    w_up16 = w_up.astype(BF16)
    w_down16 = w_down.astype(BF16)
    sinks_perm = _pair_heads(b_sinks, 1) * LOG2E
    a_tables = a_rel_bias * LOG2E

    b_bias_prompt = jnp.asarray(_b_prompt_bias(n_b_heads))
    b_bias_sample = jnp.asarray(_b_sample_bias(n_b_heads, dec_seq, cl_b))

    xp = x_prompt.reshape(batch * seq, d)
    xs = x_sample.reshape(rows_s, d)
    gfin = norm_final.reshape(1, d)

    states = {k: [] for k in ("a_kp", "a_vp", "a_ks", "a_vs", "b_kp", "b_vp", "b_ks", "b_vs")}
    for layer in range(depth):
        slot = layer // N_MIXERS
        g_mix = norm_mix[layer].reshape(1, d)
        g_ffn = norm_ffn[layer].reshape(1, d)
        if layer % N_MIXERS == 0:
            qkv_args = dict(tn=1024, q_width=d, kv_width=a_kv_width,
                            q_scale=A_HEAD_DIM ** -0.5 * LOG2E)
            qkv_p, kp, vp = _norm_qkv(xp, g_mix, a_w_qkv16, slot, tm=tm_qkv,
                                      tiles_per_batch=seq // tm_qkv, tail_rows=tail_a, **qkv_args)
            qkv_s, kn, vn = _norm_qkv(xs, g_mix, a_w_qkv16, slot, tm=rows_s,
                                      tiles_per_batch=1, tail_rows=rows_s, **qkv_args)
            table = a_tables[slot]
            mp = _attn_a_prompt(qkv_p, _a_prompt_bias(table), batch=batch, seq=seq, d_model=d)
            ms, ks, vs = _attn_a_sample(
                qkv_s, kn.reshape(dec_batch, dec_seq, a_kv_width),
                vn.reshape(dec_batch, dec_seq, a_kv_width),
                cache_a_k[slot].reshape(dec_batch, cl_a, a_kv_width),
                cache_a_v[slot].reshape(dec_batch, cl_a, a_kv_width),
                _a_sample_bias(table, dec_seq, cl_a), d_model=d)
            w_o = a_w_o16
            states["a_kp"].append(kp.reshape((batch, tail_a) + cache_a_k.shape[3:]))
            states["a_vp"].append(vp.reshape((batch, tail_a) + cache_a_k.shape[3:]))
            states["a_ks"].append(ks.reshape(cache_a_k.shape[1:]))
            states["a_vs"].append(vs.reshape(cache_a_k.shape[1:]))
        else:
            qkv_args = dict(tn=2 * b_kv_width, q_width=d, kv_width=b_kv_width,
                            q_scale=B_HEAD_DIM ** -0.5 * LOG2E)
            qkv_p, kp, vp = _norm_qkv(xp, g_mix, b_w_qkv16, slot, tm=tm_qkv,
                                      tiles_per_batch=seq // tm_qkv, tail_rows=tail_b, **qkv_args)
            qkv_s, kn, vn = _norm_qkv(xs, g_mix, b_w_qkv16, slot, tm=rows_s,
                                      tiles_per_batch=1, tail_rows=rows_s, **qkv_args)
            sinks = sinks_perm[slot].reshape(1, n_b_heads)
            mp = _attn_b_prompt(qkv_p, b_bias_prompt, sinks, batch=batch, seq=seq,
                                d_model=d, kv_width=b_kv_width)
            ms, ks, vs = _attn_b_sample(
                qkv_s, kn.reshape(dec_batch, dec_seq, b_kv_width),
                vn.reshape(dec_batch, dec_seq, b_kv_width),
                cache_b_k[slot].reshape(dec_batch, cl_b, b_kv_width),
                cache_b_v[slot].reshape(dec_batch, cl_b, b_kv_width),
                b_bias_sample, sinks, d_model=d)
            w_o = b_w_o16
            states["b_kp"].append(kp.reshape((batch, tail_b) + cache_b_k.shape[3:]))
            states["b_vp"].append(vp.reshape((batch, tail_b) + cache_b_k.shape[3:]))
            states["b_ks"].append(ks.reshape(cache_b_k.shape[1:]))
            states["b_vs"].append(vs.reshape(cache_b_k.shape[1:]))
        last = layer == depth - 1
        xp = _oproj_mlp(xp, mp, w_o, slot, g_ffn, w_up16, w_down16, layer, gfin,
                        tm=tm, tf=tf, final_norm=last)
        xs = _oproj_mlp(xs, ms, w_o, slot, g_ffn, w_up16, w_down16, layer, gfin,
                        tm=rows_s, tf=tf, final_norm=last)

    return (xp.reshape(batch, seq, d), xs.reshape(dec_batch, dec_seq, d),
            jnp.stack(states["a_kp"]), jnp.stack(states["a_vp"]),
            jnp.stack(states["b_kp"]), jnp.stack(states["b_vp"]),
            jnp.stack(states["a_ks"]), jnp.stack(states["a_vs"]),
            jnp.stack(states["b_ks"]), jnp.stack(states["b_vs"]))
```

```python
import functools

import jax
import jax.numpy as jnp
import numpy as np
from jax import lax
from jax.experimental import pallas as pl
from jax.experimental.pallas import tpu as pltpu

F32 = jnp.float32
BF16 = jnp.bfloat16

CHUNK = 64
N_MIXERS = 2
A_HEAD_DIM = 128
A_PREV_CHUNKS = 8
A_REL_CLIP = 128
B_PREV_CHUNKS = 2
B_HEAD_DIM = 64
B_KV_HEADS = 8
RMS_EPS = 1e-6
NEG_INF = -1e30
LOG2E = float(np.log2(np.e))

LANES = 128
VMEM_LIMIT_BYTES = 56 * 1024 * 1024

A_TQ = 256
A_HEADS_PER_STEP = 8
B_TQ = 128
NEW_KEY_PAD = 128


def _rms_scale(x):
    return x * lax.rsqrt(jnp.mean(x * x, axis=-1, keepdims=True) + RMS_EPS)


def _inv_rms(x):
    r = lax.rsqrt(jnp.mean(x * x, axis=-1, keepdims=True) + RMS_EPS)
    return jnp.broadcast_to(r, (x.shape[0], LANES))


def _scale_rows(acc, r):
    return jnp.concatenate(
        [acc[:, c * LANES:(c + 1) * LANES] * r for c in range(acc.shape[1] // LANES)], axis=1)


def _dot_nt(a, b):
    return lax.dot_general(a, b, (((1,), (1,)), ((), ())), preferred_element_type=F32)


def _softmax_pv(s, v):
    m = s.max(axis=-1, keepdims=True)
    p = jnp.exp2(s - m).astype(BF16)
    out = jnp.dot(p, jnp.concatenate([v, jnp.ones_like(v)], axis=1), preferred_element_type=F32)
    width = v.shape[1]
    return out[:, :width] / out[:, width:]


def _norm_qkv_kernel(x_ref, g_ref, w_ref, qkv_ref, kst_ref, vst_ref, h_ref, r_ref, *,
                     nq, nk, q_scale, tiles_per_batch, tail_rows):
    m = pl.program_id(0)
    n = pl.program_id(1)

    @pl.when(n == 0)
    def _():
        x = x_ref[...]
        h_ref[...] = (x * g_ref[...]).astype(BF16)
        r_ref[...] = _inv_rms(x)

    acc = _scale_rows(jnp.dot(h_ref[...], w_ref[...], preferred_element_type=F32), r_ref[...])
    qkv_ref[...] = (acc * jnp.where(n < nq, q_scale, 1.0)).astype(BF16)

    tm = acc.shape[0]
    is_tail = (m % tiles_per_batch) == tiles_per_batch - 1
    tail = acc[tm - tail_rows:, :]

    if nk == 0:
        @pl.when(is_tail & (n == nq))
        def _():
            kv_width = kst_ref.shape[2]
            kst_ref[0] = tail[:, :kv_width]
            vst_ref[0] = tail[:, kv_width:]
    else:
        @pl.when(is_tail & (n >= nq) & (n < nq + nk))
        def _():
            kst_ref[0] = tail

        @pl.when(is_tail & (n >= nq + nk))
        def _():
            vst_ref[0] = tail


def _norm_qkv(x, g, w, layer, *, tm, tn, q_width, kv_width, q_scale, tiles_per_batch, tail_rows):
    rows, d = x.shape
    n_total = w.shape[2]
    nq = q_width // tn
    nk = kv_width // tn
    assert (nk > 0 and kv_width % tn == 0) or tn == 2 * kv_width
    state_width = min(tn, kv_width)
    nb = rows // (tm * tiles_per_batch)

    def state_map(first):
        def index_map(m, n):
            tail = (m % tiles_per_batch) == tiles_per_batch - 1
            j = jnp.clip(n - first, 0, max(nk - 1, 0))
            return (m // tiles_per_batch, 0, jnp.where(tail, j, 0))
        return index_map

    kernel = functools.partial(_norm_qkv_kernel, nq=nq, nk=nk, q_scale=q_scale,
                               tiles_per_batch=tiles_per_batch, tail_rows=tail_rows)
    return pl.pallas_call(
        kernel,
        grid=(rows // tm, n_total // tn),
        in_specs=[
            pl.BlockSpec((tm, d), lambda m, n: (m, 0)),
            pl.BlockSpec((1, d), lambda m, n: (0, 0)),
            pl.BlockSpec((None, d, tn), lambda m, n: (layer, 0, n)),
        ],
        out_specs=[
            pl.BlockSpec((tm, tn), lambda m, n: (m, n)),
            pl.BlockSpec((1, tail_rows, state_width), state_map(nq)),
            pl.BlockSpec((1, tail_rows, state_width), state_map(nq + nk)),
        ],
        out_shape=[
            jax.ShapeDtypeStruct((rows, n_total), BF16),
            jax.ShapeDtypeStruct((nb, tail_rows, kv_width), F32),
            jax.ShapeDtypeStruct((nb, tail_rows, kv_width), F32),
        ],
        scratch_shapes=[pltpu.VMEM((tm, d), BF16), pltpu.VMEM((tm, LANES), F32)],
        compiler_params=pltpu.CompilerParams(
            dimension_semantics=("arbitrary", "arbitrary"),
            vmem_limit_bytes=VMEM_LIMIT_BYTES),
        name="norm_qkv",
    )(x, g, w)


def _attn_a_prompt_kernel(q_ref, k0_ref, k1_ref, k2_ref, v0_ref, v1_ref, v2_ref,
                          bias_ref, o_ref):
    k_refs = (k0_ref, k1_ref, k2_ref)
    v_refs = (v0_ref, v1_ref, v2_ref)
    n_heads = q_ref.shape[1] // A_HEAD_DIM

    def head_slice(h):
        return slice(h * A_HEAD_DIM, (h + 1) * A_HEAD_DIM)

    def scores(h):
        k = jnp.concatenate([r[:, head_slice(h)] for r in k_refs], axis=0)
        return _dot_nt(q_ref[:, head_slice(h)], k) + bias_ref[0, h]

    s_next = scores(0)
    for h in range(n_heads):
        s = s_next
        if h + 1 < n_heads:
            s_next = scores(h + 1)
        v = jnp.concatenate([r[:, head_slice(h)] for r in v_refs], axis=0)
        o_ref[:, head_slice(h)] = _softmax_pv(s, v).astype(BF16)


def _attn_a_prompt(qkv, bias, *, batch, seq, d_model):
    tq = A_TQ
    hw = A_HEADS_PER_STEP * A_HEAD_DIM
    nhb = d_model // hw
    nqb = seq // tq

    def kv_spec(back, col0):
        return pl.BlockSpec(
            (tq, hw), lambda b, h, i: (b * nqb + jnp.maximum(i - back, 0), col0 + h))

    return pl.pallas_call(
        _attn_a_prompt_kernel,
        grid=(batch, nhb, nqb),
        in_specs=[
            pl.BlockSpec((tq, hw), lambda b, h, i: (b * nqb + i, h)),
            kv_spec(2, nhb), kv_spec(1, nhb), kv_spec(0, nhb),
            kv_spec(2, 2 * nhb), kv_spec(1, 2 * nhb), kv_spec(0, 2 * nhb),
            pl.BlockSpec((1, A_HEADS_PER_STEP, tq, 3 * tq),
                         lambda b, h, i: (jnp.minimum(i, 2), h, 0, 0)),
        ],
        out_specs=pl.BlockSpec((tq, hw), lambda b, h, i: (b * nqb + i, h)),
        out_shape=jax.ShapeDtypeStruct((batch * seq, d_model), BF16),
        compiler_params=pltpu.CompilerParams(
            dimension_semantics=("parallel", "parallel", "parallel"),
            vmem_limit_bytes=VMEM_LIMIT_BYTES),
        name="attn_a_prompt",
    )(qkv, qkv, qkv, qkv, qkv, qkv, qkv, bias)


def _attn_a_sample_kernel(q_ref, kc_ref, vc_ref, kn_ref, vn_ref, bias_ref, o_ref):
    t = q_ref.shape[0]
    kn, vn = kn_ref[0], vn_ref[0]
    pad = jnp.zeros((NEW_KEY_PAD - t, A_HEAD_DIM), BF16)
    for h in range(q_ref.shape[1] // A_HEAD_DIM):
        sl = slice(h * A_HEAD_DIM, (h + 1) * A_HEAD_DIM)
        k = jnp.concatenate([kc_ref[0, :, sl], kn[:, sl].astype(BF16), pad], axis=0)
        v = jnp.concatenate([vc_ref[0, :, sl], vn[:, sl].astype(BF16), pad], axis=0)
        s = _dot_nt(q_ref[:, sl], k) + bias_ref[h]
        o_ref[:, sl] = _softmax_pv(s, v).astype(BF16)


def _attn_a_sample(qkv, k_new, v_new, cache_k, cache_v, slot, bias, *, d_model):
    _, nb, cl, _ = cache_k.shape
    t = k_new.shape[1]
    hw = A_HEADS_PER_STEP * A_HEAD_DIM
    nhb = d_model // hw
    cache_spec = pl.BlockSpec((None, 1, cl, hw), lambda b, h: (slot, b, 0, h))
    new_spec = pl.BlockSpec((1, t, hw), lambda b, h: (b, 0, h))
    return pl.pallas_call(
        _attn_a_sample_kernel,
        grid=(nb, nhb),
        in_specs=[
            pl.BlockSpec((t, hw), lambda b, h: (b, h)),
            cache_spec, cache_spec, new_spec, new_spec,
            pl.BlockSpec((A_HEADS_PER_STEP, t, cl + NEW_KEY_PAD), lambda b, h: (h, 0, 0)),
        ],
        out_specs=pl.BlockSpec((t, hw), lambda b, h: (b, h)),
        out_shape=jax.ShapeDtypeStruct((nb * t, d_model), BF16),
        compiler_params=pltpu.CompilerParams(
            dimension_semantics=("parallel", "parallel"),
            vmem_limit_bytes=VMEM_LIMIT_BYTES),
        name="attn_a_sample",
    )(qkv, cache_k, cache_v, k_new, v_new, bias)


def _attn_b_heads(q_ref, k_blocks, v_blocks, bias_ref, sink_ref, o_ref):
    tq = q_ref.shape[0]
    n_pairs = B_KV_HEADS // 2
    bpp = q_ref.shape[1] // LANES // n_pairs
    low = lax.broadcasted_iota(jnp.int32, (tq, LANES), 1) < B_HEAD_DIM

    def scores(pair):
        pair_sl = slice(pair * LANES, (pair + 1) * LANES)
        k = jnp.concatenate([kb(pair_sl) for kb in k_blocks], axis=0)
        low_k = lax.broadcasted_iota(jnp.int32, k.shape, 1) < B_HEAD_DIM
        zero = jnp.zeros_like(k)
        k_diag = jnp.concatenate([jnp.where(low_k, k, zero), jnp.where(low_k, zero, k)], axis=0)
        q_stack = jnp.concatenate(
            [q_ref[:, (pair * bpp + c) * LANES:(pair * bpp + c + 1) * LANES]
             for c in range(bpp)], axis=0)
        return _dot_nt(q_stack, k_diag) + bias_ref[0, pair]

    def softmax_pv(pair, s):
        pair_sl = slice(pair * LANES, (pair + 1) * LANES)
        v = jnp.concatenate([vb(pair_sl) for vb in v_blocks], axis=0)
        keys = v.shape[0]
        m_even = s[:, :keys].max(axis=-1, keepdims=True)
        m_odd = s[:, keys:].max(axis=-1, keepdims=True)
        p = jnp.concatenate(
            [jnp.exp2(s[:, :keys] - m_even), jnp.exp2(s[:, keys:] - m_odd)], axis=1).astype(BF16)
        low_k = lax.broadcasted_iota(jnp.int32, v.shape, 1) < B_HEAD_DIM
        zero = jnp.zeros_like(v)
        ones_low = jnp.where(low_k, 1.0, 0.0).astype(BF16)
        ones_high = jnp.where(low_k, 0.0, 1.0).astype(BF16)
        v_diag = jnp.concatenate(
            [jnp.concatenate([jnp.where(low_k, v, zero), ones_low], axis=1),
             jnp.concatenate([jnp.where(low_k, zero, v), ones_high], axis=1)], axis=0)
        return m_even, m_odd, jnp.dot(p, v_diag, preferred_element_type=F32)

    def finish(pair, m_even, m_odd, out):
        for c in range(bpp):
            rows = slice(c * tq, (c + 1) * tq)
            cb = pair * bpp + c
            sink_even = jnp.exp2(sink_ref[0, 2 * cb] - m_even[rows])
            sink_odd = jnp.exp2(sink_ref[0, 2 * cb + 1] - m_odd[rows])
            den = out[rows, LANES:] + jnp.where(low, sink_even, sink_odd)
            o_ref[:, cb * LANES:(cb + 1) * LANES] = (out[rows, :LANES] / den).astype(BF16)

    s_vals, pv_vals = {}, {}
    for t in range(n_pairs + 2):
        if t < n_pairs:
            s_vals[t] = scores(t)
        if 0 <= t - 1 < n_pairs:
            pv_vals[t - 1] = softmax_pv(t - 1, s_vals.pop(t - 1))
        if 0 <= t - 2 < n_pairs:
            finish(t - 2, *pv_vals.pop(t - 2))


def _attn_b_prompt_kernel(q_ref, k0_ref, k1_ref, v0_ref, v1_ref, bias_ref, sink_ref, o_ref):
    _attn_b_heads(
        q_ref,
        [lambda sl: k0_ref[:, sl], lambda sl: k1_ref[:, sl]],
        [lambda sl: v0_ref[:, sl], lambda sl: v1_ref[:, sl]],
        bias_ref, sink_ref, o_ref)


def _attn_b_prompt(qkv, bias, sinks, *, batch, seq, d_model, kv_width):
    tq = B_TQ
    nqb = seq // tq
    nq_cols = d_model // kv_width

    def kv_spec(back, col):
        return pl.BlockSpec(
            (tq, kv_width), lambda b, i: (b * nqb + jnp.maximum(i - back, 0), col))

    return pl.pallas_call(
        _attn_b_prompt_kernel,
        grid=(batch, nqb),
        in_specs=[
            pl.BlockSpec((tq, d_model), lambda b, i: (b * nqb + i, 0)),
            kv_spec(1, nq_cols), kv_spec(0, nq_cols),
            kv_spec(1, nq_cols + 1), kv_spec(0, nq_cols + 1),
            pl.BlockSpec((1,) + bias.shape[1:],
                         lambda b, i: (jnp.where(i == 0, 1, 0), 0, 0, 0)),
            pl.BlockSpec(memory_space=pltpu.SMEM),
        ],
        out_specs=pl.BlockSpec((tq, d_model), lambda b, i: (b * nqb + i, 0)),
        out_shape=jax.ShapeDtypeStruct((batch * seq, d_model), BF16),
        compiler_params=pltpu.CompilerParams(
            dimension_semantics=("parallel", "parallel"),
            vmem_limit_bytes=VMEM_LIMIT_BYTES),
        name="attn_b_prompt",
    )(qkv, qkv, qkv, qkv, qkv, bias, sinks)


def _attn_b_sample_kernel(q_ref, kc_ref, vc_ref, kn_ref, vn_ref, bias_ref, sink_ref,
                          o_ref, ko_ref, vo_ref):
    t = q_ref.shape[0]
    cl = kc_ref.shape[1]
    kc, vc, kn, vn = kc_ref[0], vc_ref[0], kn_ref[0], vn_ref[0]
    ko_ref[0, :cl - t, :] = kc[t:, :]
    ko_ref[0, cl - t:, :] = kn
    vo_ref[0, :cl - t, :] = vc[t:, :]
    vo_ref[0, cl - t:, :] = vn
    pad = jnp.zeros((NEW_KEY_PAD - t, LANES), BF16)
    _attn_b_heads(
        q_ref,
        [lambda sl: kc[:, sl].astype(BF16),
         lambda sl: jnp.concatenate([kn[:, sl].astype(BF16), pad], axis=0)],
        [lambda sl: vc[:, sl].astype(BF16),
         lambda sl: jnp.concatenate([vn[:, sl].astype(BF16), pad], axis=0)],
        bias_ref, sink_ref, o_ref)


def _attn_b_sample(qkv, k_new, v_new, cache_k, cache_v, bias, sinks, *, d_model):
    nb, cl, kv_width = cache_k.shape
    t = k_new.shape[1]
    cache_spec = pl.BlockSpec((1, cl, kv_width), lambda b: (b, 0, 0))
    new_spec = pl.BlockSpec((1, t, kv_width), lambda b: (b, 0, 0))
    return pl.pallas_call(
        _attn_b_sample_kernel,
        grid=(nb,),
        in_specs=[
            pl.BlockSpec((t, d_model), lambda b: (b, 0)),
            cache_spec, cache_spec, new_spec, new_spec,
            pl.BlockSpec(bias.shape, lambda b: (0, 0, 0, 0)),
            pl.BlockSpec(memory_space=pltpu.SMEM),
        ],
        out_specs=[pl.BlockSpec((t, d_model), lambda b: (b, 0)), cache_spec, cache_spec],
        out_shape=[
            jax.ShapeDtypeStruct((nb * t, d_model), BF16),
            jax.ShapeDtypeStruct(cache_k.shape, F32),
            jax.ShapeDtypeStruct(cache_v.shape, F32),
        ],
        compiler_params=pltpu.CompilerParams(
            dimension_semantics=("parallel",),
            vmem_limit_bytes=VMEM_LIMIT_BYTES),
        name="attn_b_sample",
    )(qkv, cache_k, cache_v, k_new, v_new, bias, sinks)


def _oproj_mlp_kernel(x_ref, a_ref, wo_ref, g_ref, wup_ref, wdn_ref, gfin_ref,
                      o_ref, h_ref, r_ref, *, final_norm):
    f = pl.program_id(1)

    @pl.when(f == 0)
    def _():
        x1 = x_ref[...] + jnp.dot(a_ref[...], wo_ref[...], preferred_element_type=F32)
        o_ref[...] = x1
        h_ref[...] = (x1 * g_ref[...]).astype(BF16)
        r_ref[...] = _inv_rms(x1)

    u = _scale_rows(jnp.dot(h_ref[...], wup_ref[...], preferred_element_type=F32), r_ref[...])
    act = jnp.square(jnp.maximum(u, 0.0)).astype(BF16)
    o_ref[...] += jnp.dot(act, wdn_ref[...], preferred_element_type=F32)

    if final_norm:
        @pl.when(f == pl.num_programs(1) - 1)
        def _():
            o_ref[...] = _rms_scale(o_ref[...]) * gfin_ref[...]


def _oproj_mlp(x, attn, wo, slot, g, wup, wdn, layer, gfin, *, tm, tf, final_norm):
    rows, d = x.shape
    d_ff = wup.shape[2]
    kernel = functools.partial(_oproj_mlp_kernel, final_norm=final_norm)
    return pl.pallas_call(
        kernel,
        grid=(rows // tm, d_ff // tf),
        in_specs=[
            pl.BlockSpec((tm, d), lambda m, f: (m, 0)),
            pl.BlockSpec((tm, d), lambda m, f: (m, 0)),
            pl.BlockSpec((None, d, d), lambda m, f: (slot, 0, 0), pipeline_mode=pl.Buffered(1)),
            pl.BlockSpec((1, d), lambda m, f: (0, 0)),
            pl.BlockSpec((None, d, tf), lambda m, f: (layer, 0, f)),
            pl.BlockSpec((None, tf, d), lambda m, f: (layer, f, 0)),
            pl.BlockSpec((1, d), lambda m, f: (0, 0)),
        ],
        out_specs=pl.BlockSpec((tm, d), lambda m, f: (m, 0)),
        out_shape=jax.ShapeDtypeStruct((rows, d), F32),
        scratch_shapes=[pltpu.VMEM((tm, d), BF16), pltpu.VMEM((tm, LANES), F32)],
        compiler_params=pltpu.CompilerParams(
            dimension_semantics=("parallel", "arbitrary"),
            vmem_limit_bytes=VMEM_LIMIT_BYTES),
        name="oproj_mlp",
    )(x, attn, wo, g, wup, wdn, gfin)


def _band_mask(n_q_chunks, n_k_chunks, n_prev):
    ci = np.arange(n_q_chunks * CHUNK)[:, None] // CHUNK
    cj = np.arange(n_k_chunks * CHUNK)[None, :] // CHUNK
    return (cj >= ci) & (cj <= ci + n_prev)


def _rel_bias_tile(table, n_rows, n_cols, key_offset):
    length = n_rows + n_cols
    rel = np.arange(length) - (n_rows - 1) - key_offset
    diag = table[:, np.clip(rel, -A_REL_CLIP, A_REL_CLIP) + A_REL_CLIP]
    flat = jnp.tile(diag, (1, n_rows))[:, :n_rows * (length - 1)]
    return flat.reshape(table.shape[0], n_rows, length - 1)[:, :, n_rows - 1:n_rows - 1 + n_cols]


def _a_prompt_bias(table):
    bias = _rel_bias_tile(table, A_TQ, 3 * A_TQ, 2 * A_TQ)
    band = _band_mask(A_TQ // CHUNK, 3 * A_TQ // CHUNK, A_PREV_CHUNKS)
    col = np.arange(3 * A_TQ)[None, :]
    masks = np.stack([band & (col >= (2 - v) * A_TQ) for v in range(3)])
    return jnp.where(masks[:, None], bias[None], NEG_INF)


def _a_sample_bias(table, t, cl):
    bias = _rel_bias_tile(table, t, cl + NEW_KEY_PAD, cl)
    valid = np.broadcast_to(np.arange(cl + NEW_KEY_PAD)[None, :] < cl + t, (t, cl + NEW_KEY_PAD))
    return jnp.where(valid[None], bias, NEG_INF)


def _pair_heads(x, axis):
    n_heads = x.shape[axis]
    group = n_heads // B_KV_HEADS
    split = x.shape[:axis] + (B_KV_HEADS // 2, 2, group) + x.shape[axis + 1:]
    return jnp.swapaxes(x.reshape(split), axis + 1, axis + 2).reshape(x.shape)


def _alibi_tile(n_q_heads, n_rows, n_cols, key_offset, valid):
    group = n_q_heads // B_KV_HEADS
    slopes = (2.0 ** (-8.0 * np.arange(1, n_q_heads + 1) / n_q_heads)).astype(np.float32)
    qi = np.arange(n_rows)[:, None]
    kj = np.arange(n_cols)[None, :]
    rel = np.abs(kj - key_offset - qi).astype(np.float32)
    alibi = (-slopes[:, None, None] * rel[None]).astype(np.float64)
    bias = np.where(valid[None], alibi * LOG2E, NEG_INF)
    stacked = [[np.concatenate([bias[group * (2 * pair) + g], bias[group * (2 * pair + 1) + g]], axis=1)
                for g in range(group)] for pair in range(B_KV_HEADS // 2)]
    return np.asarray(stacked, np.float32).reshape(B_KV_HEADS // 2, group * n_rows, 2 * n_cols)


def _b_prompt_bias(n_q_heads):
    mask = _band_mask(B_TQ // CHUNK, 2 * B_TQ // CHUNK, B_PREV_CHUNKS)
    first = mask & (np.arange(2 * B_TQ)[None, :] >= B_TQ)
    return np.stack([_alibi_tile(n_q_heads, B_TQ, 2 * B_TQ, B_TQ, mask),
                     _alibi_tile(n_q_heads, B_TQ, 2 * B_TQ, B_TQ, first)])


def _b_sample_bias(n_q_heads, t, cl):
    valid = np.broadcast_to(np.arange(cl + NEW_KEY_PAD)[None, :] < cl + t, (t, cl + NEW_KEY_PAD))
    return _alibi_tile(n_q_heads, t, cl + NEW_KEY_PAD, cl, valid)[None]


def kernel(x_prompt, x_sample, cache_a_k, cache_a_v, cache_b_k, cache_b_v, norm_mix, norm_ffn,
           norm_final, a_w_qkv, a_w_o, a_rel_bias, b_w_qkv, b_w_o, b_sinks, w_up, w_down):
    batch, seq, d = x_prompt.shape
    dec_batch, dec_seq, _ = x_sample.shape
    depth = norm_mix.shape[0]
    a_kv_width = cache_a_k.shape[3] * cache_a_k.shape[4]
    b_kv_width = cache_b_k.shape[3] * cache_b_k.shape[4]
    cl_a, cl_b = cache_a_k.shape[2], cache_b_k.shape[2]
    n_b_heads = d // B_HEAD_DIM
    tail_a = min(A_PREV_CHUNKS * CHUNK, seq)
    tail_b = min(B_PREV_CHUNKS * CHUNK, seq)

    tm = 512
    tm_qkv = 1024
    tf = 1024
    rows_s = dec_batch * dec_seq

    n_slots = b_w_qkv.shape[0]
    a_w_qkv16 = a_w_qkv.astype(BF16)
    a_w_o16 = a_w_o.astype(BF16)
    b_q16 = _pair_heads(b_w_qkv[:, :, :d].astype(BF16).reshape(n_slots, d, n_b_heads, B_HEAD_DIM), 2)
    b_w_qkv16 = jnp.concatenate(
        [b_q16.reshape(n_slots, d, d), b_w_qkv[:, :, d:].astype(BF16)], axis=2)
    b_w_o16 = _pair_heads(
        b_w_o.astype(BF16).reshape(n_slots, n_b_heads, B_HEAD_DIM, d), 1).reshape(n_slots, d, d)
    w_up16 = w_up.astype(BF16)
    w_down16 = w_down.astype(BF16)
    sinks_perm = _pair_heads(b_sinks, 1) * LOG2E
    a_tables = a_rel_bias * LOG2E
    cache_a_k16 = cache_a_k.astype(BF16).reshape(cache_a_k.shape[:3] + (a_kv_width,))
    cache_a_v16 = cache_a_v.astype(BF16).reshape(cache_a_v.shape[:3] + (a_kv_width,))

    b_bias_prompt = jnp.asarray(_b_prompt_bias(n_b_heads))
    b_bias_sample = jnp.asarray(_b_sample_bias(n_b_heads, dec_seq, cl_b))

    xp = x_prompt.reshape(batch * seq, d)
    xs = x_sample.reshape(rows_s, d)
    gfin = norm_final.reshape(1, d)

    states = {k: [] for k in ("a_kp", "a_vp", "a_ks", "a_vs", "b_kp", "b_vp", "b_ks", "b_vs")}
    for layer in range(depth):
        slot = layer // N_MIXERS
        g_mix = norm_mix[layer].reshape(1, d)
        g_ffn = norm_ffn[layer].reshape(1, d)
        if layer % N_MIXERS == 0:
            qkv_args = dict(tn=1024, q_width=d, kv_width=a_kv_width,
                            q_scale=A_HEAD_DIM ** -0.5 * LOG2E)
            qkv_p, kp, vp = _norm_qkv(xp, g_mix, a_w_qkv16, slot, tm=tm_qkv,
                                      tiles_per_batch=seq // tm_qkv, tail_rows=tail_a, **qkv_args)
            qkv_s, kn, vn = _norm_qkv(xs, g_mix, a_w_qkv16, slot, tm=rows_s,
                                      tiles_per_batch=1, tail_rows=rows_s, **qkv_args)
            table = a_tables[slot]
            mp = _attn_a_prompt(qkv_p, _a_prompt_bias(table), batch=batch, seq=seq, d_model=d)
            ms = _attn_a_sample(
                qkv_s, kn.reshape(dec_batch, dec_seq, a_kv_width),
                vn.reshape(dec_batch, dec_seq, a_kv_width),
                cache_a_k16, cache_a_v16, slot,
                _a_sample_bias(table, dec_seq, cl_a), d_model=d)
            w_o = a_w_o16
            states["a_kp"].append(kp.reshape((batch, tail_a) + cache_a_k.shape[3:]))
            states["a_vp"].append(vp.reshape((batch, tail_a) + cache_a_k.shape[3:]))
            states["a_ks"].append(kn.reshape((dec_batch, dec_seq) + cache_a_k.shape[3:]))
            states["a_vs"].append(vn.reshape((dec_batch, dec_seq) + cache_a_k.shape[3:]))
        else:
            qkv_args = dict(tn=2 * b_kv_width, q_width=d, kv_width=b_kv_width,
                            q_scale=B_HEAD_DIM ** -0.5 * LOG2E)
            qkv_p, kp, vp = _norm_qkv(xp, g_mix, b_w_qkv16, slot, tm=tm_qkv,
                                      tiles_per_batch=seq // tm_qkv, tail_rows=tail_b, **qkv_args)
            qkv_s, kn, vn = _norm_qkv(xs, g_mix, b_w_qkv16, slot, tm=rows_s,
                                      tiles_per_batch=1, tail_rows=rows_s, **qkv_args)
            sinks = sinks_perm[slot].reshape(1, n_b_heads)
            mp = _attn_b_prompt(qkv_p, b_bias_prompt, sinks, batch=batch, seq=seq,
                                d_model=d, kv_width=b_kv_width)
            ms, ks, vs = _attn_b_sample(
                qkv_s, kn.reshape(dec_batch, dec_seq, b_kv_width),
                vn.reshape(dec_batch, dec_seq, b_kv_width),
                cache_b_k[slot].reshape(dec_batch, cl_b, b_kv_width),
                cache_b_v[slot].reshape(dec_batch, cl_b, b_kv_width),
                b_bias_sample, sinks, d_model=d)
            w_o = b_w_o16
            states["b_kp"].append(kp.reshape((batch, tail_b) + cache_b_k.shape[3:]))
            states["b_vp"].append(vp.reshape((batch, tail_b) + cache_b_k.shape[3:]))
            states["b_ks"].append(ks.reshape(cache_b_k.shape[1:]))
            states["b_vs"].append(vs.reshape(cache_b_k.shape[1:]))
        last = layer == depth - 1
        xp = _oproj_mlp(xp, mp, w_o, slot, g_ffn, w_up16, w_down16, layer, gfin,
                        tm=tm, tf=tf, final_norm=last)
        xs = _oproj_mlp(xs, ms, w_o, slot, g_ffn, w_up16, w_down16, layer, gfin,
                        tm=rows_s, tf=tf, final_norm=last)

    def rolled(cache, new_rows):
        return jnp.concatenate([cache[:, :, dec_seq:], jnp.stack(new_rows)], axis=2)

    return (xp.reshape(batch, seq, d), xs.reshape(dec_batch, dec_seq, d),
            jnp.stack(states["a_kp"]), jnp.stack(states["a_vp"]),
            jnp.stack(states["b_kp"]), jnp.stack(states["b_vp"]),
            rolled(cache_a_k, states["a_ks"]), rolled(cache_a_v, states["a_vs"]),
            jnp.stack(states["b_ks"]), jnp.stack(states["b_vs"]))
```

```python
import functools

import jax
import jax.numpy as jnp
import numpy as np
from jax import lax
from jax.experimental import pallas as pl
from jax.experimental.pallas import tpu as pltpu

F32 = jnp.float32
BF16 = jnp.bfloat16

CHUNK = 64
N_MIXERS = 2
A_HEAD_DIM = 128
A_PREV_CHUNKS = 8
A_REL_CLIP = 128
B_PREV_CHUNKS = 2
B_HEAD_DIM = 64
B_KV_HEADS = 8
RMS_EPS = 1e-6
NEG_INF = -1e30
LOG2E = float(np.log2(np.e))

LANES = 128
VMEM_LIMIT_BYTES = 56 * 1024 * 1024

A_TQ = 256
A_HEADS_PER_STEP = 8
B_TQ = 128
NEW_KEY_PAD = 128


def _rms_scale(x):
    return x * lax.rsqrt(jnp.mean(x * x, axis=-1, keepdims=True) + RMS_EPS)


def _inv_rms(x):
    r = lax.rsqrt(jnp.mean(x * x, axis=-1, keepdims=True) + RMS_EPS)
    return jnp.broadcast_to(r, (x.shape[0], LANES))


def _scale_rows(acc, r):
    return jnp.concatenate(
        [acc[:, c * LANES:(c + 1) * LANES] * r for c in range(acc.shape[1] // LANES)], axis=1)


def _dot_nt(a, b):
    return lax.dot_general(a, b, (((1,), (1,)), ((), ())), preferred_element_type=F32)


def _softmax_pv(s, v):
    m = s.max(axis=-1, keepdims=True)
    p = jnp.exp2(s - m).astype(BF16)
    out = jnp.dot(p, jnp.concatenate([v, jnp.ones_like(v)], axis=1), preferred_element_type=F32)
    width = v.shape[1]
    return out[:, :width] / out[:, width:]


def _norm_qkv_kernel(x_ref, g_ref, w_ref, qkv_ref, kst_ref, vst_ref, h_ref, r_ref, *,
                     nq, nk, q_scale, tiles_per_batch, tail_rows):
    m = pl.program_id(0)
    n = pl.program_id(1)

    @pl.when(n == 0)
    def _():
        x = x_ref[...]
        h_ref[...] = (x * g_ref[...]).astype(BF16)
        r_ref[...] = _inv_rms(x)

    acc = _scale_rows(jnp.dot(h_ref[...], w_ref[...], preferred_element_type=F32), r_ref[...])
    qkv_ref[...] = (acc * jnp.where(n < nq, q_scale, 1.0)).astype(BF16)

    tm = acc.shape[0]
    is_tail = (m % tiles_per_batch) == tiles_per_batch - 1
    tail = acc[tm - tail_rows:, :]

    if nk == 0:
        @pl.when(is_tail & (n == nq))
        def _():
            kv_width = kst_ref.shape[2]
            kst_ref[0] = tail[:, :kv_width]
            vst_ref[0] = tail[:, kv_width:]
    else:
        @pl.when(is_tail & (n >= nq) & (n < nq + nk))
        def _():
            kst_ref[0] = tail

        @pl.when(is_tail & (n >= nq + nk))
        def _():
            vst_ref[0] = tail


def _norm_qkv(x, g, w, layer, *, tm, tn, q_width, kv_width, q_scale, tiles_per_batch, tail_rows):
    rows, d = x.shape
    n_total = w.shape[2]
    nq = q_width // tn
    nk = kv_width // tn
    assert (nk > 0 and kv_width % tn == 0) or tn == 2 * kv_width
    state_width = min(tn, kv_width)
    nb = rows // (tm * tiles_per_batch)

    def state_map(first):
        def index_map(m, n):
            tail = (m % tiles_per_batch) == tiles_per_batch - 1
            j = jnp.clip(n - first, 0, max(nk - 1, 0))
            return (m // tiles_per_batch, 0, jnp.where(tail, j, 0))
        return index_map

    kernel = functools.partial(_norm_qkv_kernel, nq=nq, nk=nk, q_scale=q_scale,
                               tiles_per_batch=tiles_per_batch, tail_rows=tail_rows)
    n_m, n_n = rows // tm, n_total // tn
    assert n_n > 1

    def row_tile_ahead(m, n):
        return (jnp.minimum(m + (n == n_n - 1), n_m - 1), 0)

    return pl.pallas_call(
        kernel,
        grid=(n_m, n_n),
        in_specs=[
            pl.BlockSpec((tm, d), row_tile_ahead),
            pl.BlockSpec((1, d), lambda m, n: (0, 0)),
            pl.BlockSpec((None, d, tn), lambda m, n: (layer, 0, n)),
        ],
        out_specs=[
            pl.BlockSpec((tm, tn), lambda m, n: (m, n)),
            pl.BlockSpec((1, tail_rows, state_width), state_map(nq)),
            pl.BlockSpec((1, tail_rows, state_width), state_map(nq + nk)),
        ],
        out_shape=[
            jax.ShapeDtypeStruct((rows, n_total), BF16),
            jax.ShapeDtypeStruct((nb, tail_rows, kv_width), F32),
            jax.ShapeDtypeStruct((nb, tail_rows, kv_width), F32),
        ],
        scratch_shapes=[pltpu.VMEM((tm, d), BF16), pltpu.VMEM((tm, LANES), F32)],
        compiler_params=pltpu.CompilerParams(
            dimension_semantics=("arbitrary", "arbitrary"),
            vmem_limit_bytes=VMEM_LIMIT_BYTES),
        name="norm_qkv",
    )(x, g, w)


def _attn_a_prompt_kernel(q_ref, k0_ref, k1_ref, k2_ref, v0_ref, v1_ref, v2_ref,
                          bias_ref, o_ref):
    k_refs = (k0_ref, k1_ref, k2_ref)
    v_refs = (v0_ref, v1_ref, v2_ref)
    n_heads = q_ref.shape[1] // A_HEAD_DIM

    def head_slice(h):
        return slice(h * A_HEAD_DIM, (h + 1) * A_HEAD_DIM)

    def scores(h):
        k = jnp.concatenate([r[:, head_slice(h)] for r in k_refs], axis=0)
        return _dot_nt(q_ref[:, head_slice(h)], k) + bias_ref[0, h]

    s_next = scores(0)
    for h in range(n_heads):
        s = s_next
        if h + 1 < n_heads:
            s_next = scores(h + 1)
        v = jnp.concatenate([r[:, head_slice(h)] for r in v_refs], axis=0)
        o_ref[:, head_slice(h)] = _softmax_pv(s, v).astype(BF16)


def _attn_a_prompt(qkv, bias, *, batch, seq, d_model):
    tq = A_TQ
    hw = A_HEADS_PER_STEP * A_HEAD_DIM
    nhb = d_model // hw
    nqb = seq // tq

    def kv_spec(back, col0):
        return pl.BlockSpec(
            (tq, hw), lambda b, h, i: (b * nqb + jnp.maximum(i - back, 0), col0 + h))

    return pl.pallas_call(
        _attn_a_prompt_kernel,
        grid=(batch, nhb, nqb),
        in_specs=[
            pl.BlockSpec((tq, hw), lambda b, h, i: (b * nqb + i, h)),
            kv_spec(2, nhb), kv_spec(1, nhb), kv_spec(0, nhb),
            kv_spec(2, 2 * nhb), kv_spec(1, 2 * nhb), kv_spec(0, 2 * nhb),
            pl.BlockSpec((1, A_HEADS_PER_STEP, tq, 3 * tq),
                         lambda b, h, i: (jnp.minimum(i, 2), h, 0, 0)),
        ],
        out_specs=pl.BlockSpec((tq, hw), lambda b, h, i: (b * nqb + i, h)),
        out_shape=jax.ShapeDtypeStruct((batch * seq, d_model), BF16),
        compiler_params=pltpu.CompilerParams(
            dimension_semantics=("parallel", "parallel", "parallel"),
            vmem_limit_bytes=VMEM_LIMIT_BYTES),
        name="attn_a_prompt",
    )(qkv, qkv, qkv, qkv, qkv, qkv, qkv, bias)


def _attn_a_sample_kernel(q_ref, kc_ref, vc_ref, kn_ref, vn_ref, bias_ref, o_ref):
    t = q_ref.shape[0]
    kn, vn = kn_ref[0], vn_ref[0]
    pad = jnp.zeros((NEW_KEY_PAD - t, A_HEAD_DIM), BF16)
    for h in range(q_ref.shape[1] // A_HEAD_DIM):
        sl = slice(h * A_HEAD_DIM, (h + 1) * A_HEAD_DIM)
        k = jnp.concatenate([kc_ref[0, :, sl], kn[:, sl].astype(BF16), pad], axis=0)
        v = jnp.concatenate([vc_ref[0, :, sl], vn[:, sl].astype(BF16), pad], axis=0)
        s = _dot_nt(q_ref[:, sl], k) + bias_ref[h]
        o_ref[:, sl] = _softmax_pv(s, v).astype(BF16)


def _attn_a_sample(qkv, k_new, v_new, cache_k, cache_v, slot, bias, *, d_model):
    _, nb, cl, _ = cache_k.shape
    t = k_new.shape[1]
    hw = A_HEADS_PER_STEP * A_HEAD_DIM
    nhb = d_model // hw
    cache_spec = pl.BlockSpec((None, 1, cl, hw), lambda b, h: (slot, b, 0, h))
    new_spec = pl.BlockSpec((1, t, hw), lambda b, h: (b, 0, h))
    return pl.pallas_call(
        _attn_a_sample_kernel,
        grid=(nb, nhb),
        in_specs=[
            pl.BlockSpec((t, hw), lambda b, h: (b, h)),
            cache_spec, cache_spec, new_spec, new_spec,
            pl.BlockSpec((A_HEADS_PER_STEP, t, cl + NEW_KEY_PAD), lambda b, h: (h, 0, 0)),
        ],
        out_specs=pl.BlockSpec((t, hw), lambda b, h: (b, h)),
        out_shape=jax.ShapeDtypeStruct((nb * t, d_model), BF16),
        compiler_params=pltpu.CompilerParams(
            dimension_semantics=("parallel", "parallel"),
            vmem_limit_bytes=VMEM_LIMIT_BYTES),
        name="attn_a_sample",
    )(qkv, cache_k, cache_v, k_new, v_new, bias)


def _attn_b_heads(q_ref, k_blocks, v_blocks, bias_ref, sink_ref, o_ref):
    tq = q_ref.shape[0]
    n_pairs = B_KV_HEADS // 2
    bpp = q_ref.shape[1] // LANES // n_pairs
    low = lax.broadcasted_iota(jnp.int32, (tq, LANES), 1) < B_HEAD_DIM

    def scores(pair):
        pair_sl = slice(pair * LANES, (pair + 1) * LANES)
        k = jnp.concatenate([kb(pair_sl) for kb in k_blocks], axis=0)
        low_k = lax.broadcasted_iota(jnp.int32, k.shape, 1) < B_HEAD_DIM
        zero = jnp.zeros_like(k)
        k_diag = jnp.concatenate([jnp.where(low_k, k, zero), jnp.where(low_k, zero, k)], axis=0)
        q_stack = jnp.concatenate(
            [q_ref[:, (pair * bpp + c) * LANES:(pair * bpp + c + 1) * LANES]
             for c in range(bpp)], axis=0)
        return _dot_nt(q_stack, k_diag) + bias_ref[0, pair]

    def softmax_pv(pair, s):
        pair_sl = slice(pair * LANES, (pair + 1) * LANES)
        v = jnp.concatenate([vb(pair_sl) for vb in v_blocks], axis=0)
        keys = v.shape[0]
        m_even = s[:, :keys].max(axis=-1, keepdims=True)
        m_odd = s[:, keys:].max(axis=-1, keepdims=True)
        p = jnp.concatenate(
            [jnp.exp2(s[:, :keys] - m_even), jnp.exp2(s[:, keys:] - m_odd)], axis=1).astype(BF16)
        low_k = lax.broadcasted_iota(jnp.int32, v.shape, 1) < B_HEAD_DIM
        zero = jnp.zeros_like(v)
        ones_low = jnp.where(low_k, 1.0, 0.0).astype(BF16)
        ones_high = jnp.where(low_k, 0.0, 1.0).astype(BF16)
        v_diag = jnp.concatenate(
            [jnp.concatenate([jnp.where(low_k, v, zero), ones_low], axis=1),
             jnp.concatenate([jnp.where(low_k, zero, v), ones_high], axis=1)], axis=0)
        return m_even, m_odd, jnp.dot(p, v_diag, preferred_element_type=F32)

    def finish(pair, m_even, m_odd, out):
        for c in range(bpp):
            rows = slice(c * tq, (c + 1) * tq)
            cb = pair * bpp + c
            sink_even = jnp.exp2(sink_ref[0, 2 * cb] - m_even[rows])
            sink_odd = jnp.exp2(sink_ref[0, 2 * cb + 1] - m_odd[rows])
            den = out[rows, LANES:] + jnp.where(low, sink_even, sink_odd)
            o_ref[:, cb * LANES:(cb + 1) * LANES] = (out[rows, :LANES] / den).astype(BF16)

    s_vals, pv_vals = {}, {}
    for t in range(n_pairs + 2):
        if t < n_pairs:
            s_vals[t] = scores(t)
        if 0 <= t - 1 < n_pairs:
            pv_vals[t - 1] = softmax_pv(t - 1, s_vals.pop(t - 1))
        if 0 <= t - 2 < n_pairs:
            finish(t - 2, *pv_vals.pop(t - 2))


def _attn_b_prompt_kernel(q_ref, k0_ref, k1_ref, v0_ref, v1_ref, bias_ref, sink_ref, o_ref):
    _attn_b_heads(
        q_ref,
        [lambda sl: k0_ref[:, sl], lambda sl: k1_ref[:, sl]],
        [lambda sl: v0_ref[:, sl], lambda sl: v1_ref[:, sl]],
        bias_ref, sink_ref, o_ref)


def _attn_b_prompt(qkv, bias, sinks, *, batch, seq, d_model, kv_width):
    tq = B_TQ
    nqb = seq // tq
    nq_cols = d_model // kv_width

    def kv_spec(back, col):
        return pl.BlockSpec(
            (tq, kv_width), lambda b, i: (b * nqb + jnp.maximum(i - back, 0), col))

    return pl.pallas_call(
        _attn_b_prompt_kernel,
        grid=(batch, nqb),
        in_specs=[
            pl.BlockSpec((tq, d_model), lambda b, i: (b * nqb + i, 0)),
            kv_spec(1, nq_cols), kv_spec(0, nq_cols),
            kv_spec(1, nq_cols + 1), kv_spec(0, nq_cols + 1),
            pl.BlockSpec((1,) + bias.shape[1:],
                         lambda b, i: (jnp.where(i == 0, 1, 0), 0, 0, 0)),
            pl.BlockSpec(memory_space=pltpu.SMEM),
        ],
        out_specs=pl.BlockSpec((tq, d_model), lambda b, i: (b * nqb + i, 0)),
        out_shape=jax.ShapeDtypeStruct((batch * seq, d_model), BF16),
        compiler_params=pltpu.CompilerParams(
            dimension_semantics=("parallel", "parallel"),
            vmem_limit_bytes=VMEM_LIMIT_BYTES),
        name="attn_b_prompt",
    )(qkv, qkv, qkv, qkv, qkv, bias, sinks)


def _attn_b_sample_kernel(q_ref, kc_ref, vc_ref, kn_ref, vn_ref, bias_ref, sink_ref,
                          o_ref, ko_ref, vo_ref):
    t = q_ref.shape[0]
    cl = kc_ref.shape[1]
    kc, vc, kn, vn = kc_ref[0], vc_ref[0], kn_ref[0], vn_ref[0]
    ko_ref[0, :cl - t, :] = kc[t:, :]
    ko_ref[0, cl - t:, :] = kn
    vo_ref[0, :cl - t, :] = vc[t:, :]
    vo_ref[0, cl - t:, :] = vn
    pad = jnp.zeros((NEW_KEY_PAD - t, LANES), BF16)
    _attn_b_heads(
        q_ref,
        [lambda sl: kc[:, sl].astype(BF16),
         lambda sl: jnp.concatenate([kn[:, sl].astype(BF16), pad], axis=0)],
        [lambda sl: vc[:, sl].astype(BF16),
         lambda sl: jnp.concatenate([vn[:, sl].astype(BF16), pad], axis=0)],
        bias_ref, sink_ref, o_ref)


def _attn_b_sample(qkv, k_new, v_new, cache_k, cache_v, bias, sinks, *, d_model):
    nb, cl, kv_width = cache_k.shape
    t = k_new.shape[1]
    cache_spec = pl.BlockSpec((1, cl, kv_width), lambda b: (b, 0, 0))
    new_spec = pl.BlockSpec((1, t, kv_width), lambda b: (b, 0, 0))
    return pl.pallas_call(
        _attn_b_sample_kernel,
        grid=(nb,),
        in_specs=[
            pl.BlockSpec((t, d_model), lambda b: (b, 0)),
            cache_spec, cache_spec, new_spec, new_spec,
            pl.BlockSpec(bias.shape, lambda b: (0, 0, 0, 0)),
            pl.BlockSpec(memory_space=pltpu.SMEM),
        ],
        out_specs=[pl.BlockSpec((t, d_model), lambda b: (b, 0)), cache_spec, cache_spec],
        out_shape=[
            jax.ShapeDtypeStruct((nb * t, d_model), BF16),
            jax.ShapeDtypeStruct(cache_k.shape, F32),
            jax.ShapeDtypeStruct(cache_v.shape, F32),
        ],
        compiler_params=pltpu.CompilerParams(
            dimension_semantics=("parallel",),
            vmem_limit_bytes=VMEM_LIMIT_BYTES),
        name="attn_b_sample",
    )(qkv, cache_k, cache_v, k_new, v_new, bias, sinks)


def _oproj_mlp_kernel(x_ref, a_ref, wo_ref, g_ref, wup_ref, wdn_ref, gfin_ref,
                      o_ref, h_ref, r_ref, *, final_norm):
    f = pl.program_id(1)

    @pl.when(f == 0)
    def _():
        x1 = x_ref[...] + jnp.dot(a_ref[...], wo_ref[...], preferred_element_type=F32)
        o_ref[...] = x1
        h_ref[...] = (x1 * g_ref[...]).astype(BF16)
        r_ref[...] = _inv_rms(x1)

    u = _scale_rows(jnp.dot(h_ref[...], wup_ref[...], preferred_element_type=F32), r_ref[...])
    act = jnp.square(jnp.maximum(u, 0.0)).astype(BF16)
    o_ref[...] += jnp.dot(act, wdn_ref[...], preferred_element_type=F32)

    if final_norm:
        @pl.when(f == pl.num_programs(1) - 1)
        def _():
            o_ref[...] = _rms_scale(o_ref[...]) * gfin_ref[...]


def _oproj_mlp(x, attn, wo, slot, g, wup, wdn, layer, gfin, *, tm, tf, final_norm):
    rows, d = x.shape
    d_ff = wup.shape[2]
    kernel = functools.partial(_oproj_mlp_kernel, final_norm=final_norm)
    n_m, n_f = rows // tm, d_ff // tf

    assert n_f > 1

    def row_tile_ahead(m, f):
        return (jnp.minimum(m + (f == n_f - 1), n_m - 1), 0)

    return pl.pallas_call(
        kernel,
        grid=(n_m, n_f),
        in_specs=[
            pl.BlockSpec((tm, d), row_tile_ahead),
            pl.BlockSpec((tm, d), row_tile_ahead),
            pl.BlockSpec((None, d, d), lambda m, f: (slot, 0, 0), pipeline_mode=pl.Buffered(1)),
            pl.BlockSpec((1, d), lambda m, f: (0, 0)),
            pl.BlockSpec((None, d, tf), lambda m, f: (layer, 0, f)),
            pl.BlockSpec((None, tf, d), lambda m, f: (layer, f, 0)),
            pl.BlockSpec((1, d), lambda m, f: (0, 0)),
        ],
        out_specs=pl.BlockSpec((tm, d), lambda m, f: (m, 0)),
        out_shape=jax.ShapeDtypeStruct((rows, d), F32),
        scratch_shapes=[pltpu.VMEM((tm, d), BF16), pltpu.VMEM((tm, LANES), F32)],
        compiler_params=pltpu.CompilerParams(
            dimension_semantics=("arbitrary", "arbitrary"),
            vmem_limit_bytes=VMEM_LIMIT_BYTES),
        name="oproj_mlp",
    )(x, attn, wo, g, wup, wdn, gfin)


def _band_mask(n_q_chunks, n_k_chunks, n_prev):
    ci = np.arange(n_q_chunks * CHUNK)[:, None] // CHUNK
    cj = np.arange(n_k_chunks * CHUNK)[None, :] // CHUNK
    return (cj >= ci) & (cj <= ci + n_prev)


def _rel_bias_tile(table, n_rows, n_cols, key_offset):
    length = n_rows + n_cols
    rel = np.arange(length) - (n_rows - 1) - key_offset
    diag = table[:, np.clip(rel, -A_REL_CLIP, A_REL_CLIP) + A_REL_CLIP]
    flat = jnp.tile(diag, (1, n_rows))[:, :n_rows * (length - 1)]
    return flat.reshape(table.shape[0], n_rows, length - 1)[:, :, n_rows - 1:n_rows - 1 + n_cols]


def _a_prompt_bias(table):
    bias = _rel_bias_tile(table, A_TQ, 3 * A_TQ, 2 * A_TQ)
    band = _band_mask(A_TQ // CHUNK, 3 * A_TQ // CHUNK, A_PREV_CHUNKS)
    col = np.arange(3 * A_TQ)[None, :]
    masks = np.stack([band & (col >= (2 - v) * A_TQ) for v in range(3)])
    return jnp.where(masks[:, None], bias[None], NEG_INF)


def _a_sample_bias(table, t, cl):
    bias = _rel_bias_tile(table, t, cl + NEW_KEY_PAD, cl)
    valid = np.broadcast_to(np.arange(cl + NEW_KEY_PAD)[None, :] < cl + t, (t, cl + NEW_KEY_PAD))
    return jnp.where(valid[None], bias, NEG_INF)


def _pair_heads(x, axis):
    n_heads = x.shape[axis]
    group = n_heads // B_KV_HEADS
    split = x.shape[:axis] + (B_KV_HEADS // 2, 2, group) + x.shape[axis + 1:]
    return jnp.swapaxes(x.reshape(split), axis + 1, axis + 2).reshape(x.shape)


def _alibi_tile(n_q_heads, n_rows, n_cols, key_offset, valid):
    group = n_q_heads // B_KV_HEADS
    slopes = (2.0 ** (-8.0 * np.arange(1, n_q_heads + 1) / n_q_heads)).astype(np.float32)
    qi = np.arange(n_rows)[:, None]
    kj = np.arange(n_cols)[None, :]
    rel = np.abs(kj - key_offset - qi).astype(np.float32)
    alibi = (-slopes[:, None, None] * rel[None]).astype(np.float64)
    bias = np.where(valid[None], alibi * LOG2E, NEG_INF)
    stacked = [[np.concatenate([bias[group * (2 * pair) + g], bias[group * (2 * pair + 1) + g]], axis=1)
                for g in range(group)] for pair in range(B_KV_HEADS // 2)]
    return np.asarray(stacked, np.float32).reshape(B_KV_HEADS // 2, group * n_rows, 2 * n_cols)


def _b_prompt_bias(n_q_heads):
    mask = _band_mask(B_TQ // CHUNK, 2 * B_TQ // CHUNK, B_PREV_CHUNKS)
    first = mask & (np.arange(2 * B_TQ)[None, :] >= B_TQ)
    return np.stack([_alibi_tile(n_q_heads, B_TQ, 2 * B_TQ, B_TQ, mask),
                     _alibi_tile(n_q_heads, B_TQ, 2 * B_TQ, B_TQ, first)])


def _b_sample_bias(n_q_heads, t, cl):
    valid = np.broadcast_to(np.arange(cl + NEW_KEY_PAD)[None, :] < cl + t, (t, cl + NEW_KEY_PAD))
    return _alibi_tile(n_q_heads, t, cl + NEW_KEY_PAD, cl, valid)[None]


def kernel(x_prompt, x_sample, cache_a_k, cache_a_v, cache_b_k, cache_b_v, norm_mix, norm_ffn,
           norm_final, a_w_qkv, a_w_o, a_rel_bias, b_w_qkv, b_w_o, b_sinks, w_up, w_down):
    batch, seq, d = x_prompt.shape
    dec_batch, dec_seq, _ = x_sample.shape
    depth = norm_mix.shape[0]
    a_kv_width = cache_a_k.shape[3] * cache_a_k.shape[4]
    b_kv_width = cache_b_k.shape[3] * cache_b_k.shape[4]
    cl_a, cl_b = cache_a_k.shape[2], cache_b_k.shape[2]
    n_b_heads = d // B_HEAD_DIM
    tail_a = min(A_PREV_CHUNKS * CHUNK, seq)
    tail_b = min(B_PREV_CHUNKS * CHUNK, seq)

    tm = 512
    tm_qkv = 1024
    tf = 1024
    rows_s = dec_batch * dec_seq

    n_slots = b_w_qkv.shape[0]
    a_w_qkv16 = a_w_qkv.astype(BF16)
    a_w_o16 = a_w_o.astype(BF16)
    b_q16 = _pair_heads(b_w_qkv[:, :, :d].astype(BF16).reshape(n_slots, d, n_b_heads, B_HEAD_DIM), 2)
    b_w_qkv16 = jnp.concatenate(
        [b_q16.reshape(n_slots, d, d), b_w_qkv[:, :, d:].astype(BF16)], axis=2)
    b_w_o16 = _pair_heads(
        b_w_o.astype(BF16).reshape(n_slots, n_b_heads, B_HEAD_DIM, d), 1).reshape(n_slots, d, d)
    w_up16 = w_up.astype(BF16)
    w_down16 = w_down.astype(BF16)
    sinks_perm = _pair_heads(b_sinks, 1) * LOG2E
    a_tables = a_rel_bias * LOG2E
    cache_a_k16 = cache_a_k.astype(BF16).reshape(cache_a_k.shape[:3] + (a_kv_width,))
    cache_a_v16 = cache_a_v.astype(BF16).reshape(cache_a_v.shape[:3] + (a_kv_width,))

    b_bias_prompt = jnp.asarray(_b_prompt_bias(n_b_heads))
    b_bias_sample = jnp.asarray(_b_sample_bias(n_b_heads, dec_seq, cl_b))

    xp = x_prompt.reshape(batch * seq, d)
    xs = x_sample.reshape(rows_s, d)
    gfin = norm_final.reshape(1, d)

    states = {k: [] for k in ("a_kp", "a_vp", "a_ks", "a_vs", "b_kp", "b_vp", "b_ks", "b_vs")}
    for layer in range(depth):
        slot = layer // N_MIXERS
        g_mix = norm_mix[layer].reshape(1, d)
        g_ffn = norm_ffn[layer].reshape(1, d)
        if layer % N_MIXERS == 0:
            qkv_args = dict(tn=1024, q_width=d, kv_width=a_kv_width,
                            q_scale=A_HEAD_DIM ** -0.5 * LOG2E)
            qkv_p, kp, vp = _norm_qkv(xp, g_mix, a_w_qkv16, slot, tm=tm_qkv,
                                      tiles_per_batch=seq // tm_qkv, tail_rows=tail_a, **qkv_args)
            qkv_s, kn, vn = _norm_qkv(xs, g_mix, a_w_qkv16, slot, tm=rows_s,
                                      tiles_per_batch=1, tail_rows=rows_s, **qkv_args)
            table = a_tables[slot]
            mp = _attn_a_prompt(qkv_p, _a_prompt_bias(table), batch=batch, seq=seq, d_model=d)
            ms = _attn_a_sample(
                qkv_s, kn.reshape(dec_batch, dec_seq, a_kv_width),
                vn.reshape(dec_batch, dec_seq, a_kv_width),
                cache_a_k16, cache_a_v16, slot,
                _a_sample_bias(table, dec_seq, cl_a), d_model=d)
            w_o = a_w_o16
            states["a_kp"].append(kp.reshape((batch, tail_a) + cache_a_k.shape[3:]))
            states["a_vp"].append(vp.reshape((batch, tail_a) + cache_a_k.shape[3:]))
            states["a_ks"].append(kn.reshape((dec_batch, dec_seq) + cache_a_k.shape[3:]))
            states["a_vs"].append(vn.reshape((dec_batch, dec_seq) + cache_a_k.shape[3:]))
        else:
            qkv_args = dict(tn=2 * b_kv_width, q_width=d, kv_width=b_kv_width,
                            q_scale=B_HEAD_DIM ** -0.5 * LOG2E)
            qkv_p, kp, vp = _norm_qkv(xp, g_mix, b_w_qkv16, slot, tm=tm_qkv,
                                      tiles_per_batch=seq // tm_qkv, tail_rows=tail_b, **qkv_args)
            qkv_s, kn, vn = _norm_qkv(xs, g_mix, b_w_qkv16, slot, tm=rows_s,
                                      tiles_per_batch=1, tail_rows=rows_s, **qkv_args)
            sinks = sinks_perm[slot].reshape(1, n_b_heads)
            mp = _attn_b_prompt(qkv_p, b_bias_prompt, sinks, batch=batch, seq=seq,
                                d_model=d, kv_width=b_kv_width)
            ms, ks, vs = _attn_b_sample(
                qkv_s, kn.reshape(dec_batch, dec_seq, b_kv_width),
                vn.reshape(dec_batch, dec_seq, b_kv_width),
                cache_b_k[slot].reshape(dec_batch, cl_b, b_kv_width),
                cache_b_v[slot].reshape(dec_batch, cl_b, b_kv_width),
                b_bias_sample, sinks, d_model=d)
            w_o = b_w_o16
            states["b_kp"].append(kp.reshape((batch, tail_b) + cache_b_k.shape[3:]))
            states["b_vp"].append(vp.reshape((batch, tail_b) + cache_b_k.shape[3:]))
            states["b_ks"].append(ks.reshape(cache_b_k.shape[1:]))
            states["b_vs"].append(vs.reshape(cache_b_k.shape[1:]))
        last = layer == depth - 1
        xp = _oproj_mlp(xp, mp, w_o, slot, g_ffn, w_up16, w_down16, layer, gfin,
                        tm=tm, tf=tf, final_norm=last)
        xs = _oproj_mlp(xs, ms, w_o, slot, g_ffn, w_up16, w_down16, layer, gfin,
                        tm=rows_s, tf=tf, final_norm=last)

    def rolled(cache, new_rows):
        return jnp.concatenate([cache, jnp.stack(new_rows)], axis=2)[:, :, dec_seq:]

    return (xp.reshape(batch, seq, d), xs.reshape(dec_batch, dec_seq, d),
            jnp.stack(states["a_kp"]), jnp.stack(states["a_vp"]),
            jnp.stack(states["b_kp"]), jnp.stack(states["b_vp"]),
            rolled(cache_a_k, states["a_ks"]), rolled(cache_a_v, states["a_vs"]),
            jnp.stack(states["b_ks"]), jnp.stack(states["b_vs"]))
```

```python
import functools

import jax
import jax.numpy as jnp
import numpy as np
from jax import lax
from jax.experimental import pallas as pl
from jax.experimental.pallas import tpu as pltpu

F32 = jnp.float32
BF16 = jnp.bfloat16

CHUNK = 64
N_MIXERS = 2
A_HEAD_DIM = 128
A_PREV_CHUNKS = 8
A_REL_CLIP = 128
B_PREV_CHUNKS = 2
B_HEAD_DIM = 64
B_KV_HEADS = 8
RMS_EPS = 1e-6
NEG_INF = -1e30
LOG2E = float(np.log2(np.e))

LANES = 128
VMEM_LIMIT_BYTES = 56 * 1024 * 1024

A_TQ = 256
A_HEADS_PER_STEP = 8
B_TQ = 128
NEW_KEY_PAD = 128


def _rms_scale(x):
    return x * lax.rsqrt(jnp.mean(x * x, axis=-1, keepdims=True) + RMS_EPS)


def _scale_rows(acc, r):
    return jnp.concatenate(
        [acc[:, c * LANES:(c + 1) * LANES] * r for c in range(acc.shape[1] // LANES)], axis=1)


def _staggered_row_specs(tm, width, n_parts, n_m, first_step):
    assert first_step >= 1 and width % n_parts == 0

    def spec(j):
        return pl.BlockSpec(
            (tm, width // n_parts),
            lambda m, s: (jnp.minimum(m + (s >= first_step + j), n_m - 1), j))

    return [spec(j) for j in range(n_parts)]


def _dot_nt(a, b):
    return lax.dot_general(a, b, (((1,), (1,)), ((), ())), preferred_element_type=F32)


def _softmax_pv(s, v):
    m = s.max(axis=-1, keepdims=True)
    p = jnp.exp2(s - m).astype(BF16)
    out = jnp.dot(p, jnp.concatenate([v, jnp.ones_like(v)], axis=1), preferred_element_type=F32)
    width = v.shape[1]
    return out[:, :width] / out[:, width:]


def _norm_qkv_kernel(*refs, n_x, nq, nk, q_scale, tiles_per_batch, tail_rows):
    x_refs = refs[:n_x]
    g_ref, w_ref, qkv_ref, kst_ref, vst_ref, h_ref, r_ref = refs[n_x:]
    m = pl.program_id(0)
    n = pl.program_id(1)

    @pl.when(n == 0)
    def _():
        width = x_refs[0].shape[1]
        sum_sq = None
        for j, x_ref in enumerate(x_refs):
            cols = slice(j * width, (j + 1) * width)
            x = x_ref[...]
            h_ref[:, cols] = (x * g_ref[:, cols]).astype(BF16)
            part = jnp.sum(x * x, axis=-1, keepdims=True)
            sum_sq = part if sum_sq is None else sum_sq + part
        r = lax.rsqrt(sum_sq / h_ref.shape[1] + RMS_EPS)
        r_ref[...] = jnp.broadcast_to(r, r_ref.shape)

    acc = jnp.dot(h_ref[...], w_ref[...], preferred_element_type=F32)
    r = r_ref[...]
    r_out = r * jnp.where(n < nq, q_scale, 1.0)
    tm, tn = acc.shape
    for c in range(tn // LANES):
        cols = slice(c * LANES, (c + 1) * LANES)
        qkv_ref[:, cols] = (acc[:, cols] * r_out).astype(BF16)

    is_tail = (m % tiles_per_batch) == tiles_per_batch - 1

    def tail():
        return _scale_rows(acc[tm - tail_rows:, :], r[tm - tail_rows:, :])

    if nk == 0:
        @pl.when(is_tail & (n == nq))
        def _():
            kv_width = kst_ref.shape[2]
            kst_ref[0] = tail()[:, :kv_width]
            vst_ref[0] = tail()[:, kv_width:]
    else:
        @pl.when(is_tail & (n >= nq) & (n < nq + nk))
        def _():
            kst_ref[0] = tail()

        @pl.when(is_tail & (n >= nq + nk))
        def _():
            vst_ref[0] = tail()


def _norm_qkv(x, g, w, layer, *, tm, tn, q_width, kv_width, q_scale, tiles_per_batch, tail_rows):
    rows, d = x.shape
    n_total = w.shape[2]
    nq = q_width // tn
    nk = kv_width // tn
    assert (nk > 0 and kv_width % tn == 0) or tn == 2 * kv_width
    state_width = min(tn, kv_width)
    nb = rows // (tm * tiles_per_batch)

    def state_map(first):
        def index_map(m, n):
            tail = (m % tiles_per_batch) == tiles_per_batch - 1
            j = jnp.clip(n - first, 0, max(nk - 1, 0))
            return (m // tiles_per_batch, 0, jnp.where(tail, j, 0))
        return index_map

    n_m, n_n = rows // tm, n_total // tn
    n_x = min(4, n_n - 1)
    kernel = functools.partial(_norm_qkv_kernel, n_x=n_x, nq=nq, nk=nk, q_scale=q_scale,
                               tiles_per_batch=tiles_per_batch, tail_rows=tail_rows)
    return pl.pallas_call(
        kernel,
        grid=(n_m, n_n),
        in_specs=_staggered_row_specs(tm, d, n_x, n_m, n_n - n_x) + [
            pl.BlockSpec((1, d), lambda m, n: (0, 0)),
            pl.BlockSpec((None, d, tn), lambda m, n: (layer, 0, n)),
        ],
        out_specs=[
            pl.BlockSpec((tm, tn), lambda m, n: (m, n)),
            pl.BlockSpec((1, tail_rows, state_width), state_map(nq)),
            pl.BlockSpec((1, tail_rows, state_width), state_map(nq + nk)),
        ],
        out_shape=[
            jax.ShapeDtypeStruct((rows, n_total), BF16),
            jax.ShapeDtypeStruct((nb, tail_rows, kv_width), F32),
            jax.ShapeDtypeStruct((nb, tail_rows, kv_width), F32),
        ],
        scratch_shapes=[pltpu.VMEM((tm, d), BF16), pltpu.VMEM((tm, LANES), F32)],
        compiler_params=pltpu.CompilerParams(
            dimension_semantics=("arbitrary", "arbitrary"),
            vmem_limit_bytes=VMEM_LIMIT_BYTES),
        name="norm_qkv",
    )(*([x] * n_x), g, w)


def _attn_a_prompt_kernel(q_ref, k0_ref, k1_ref, k2_ref, v0_ref, v1_ref, v2_ref,
                          bias_ref, o_ref):
    k_refs = (k0_ref, k1_ref, k2_ref)
    v_refs = (v0_ref, v1_ref, v2_ref)
    n_heads = q_ref.shape[1] // A_HEAD_DIM

    def head_slice(h):
        return slice(h * A_HEAD_DIM, (h + 1) * A_HEAD_DIM)

    def scores(h):
        k = jnp.concatenate([r[:, head_slice(h)] for r in k_refs], axis=0)
        return _dot_nt(q_ref[:, head_slice(h)], k) + bias_ref[0, h]

    s_next = scores(0)
    for h in range(n_heads):
        s = s_next
        if h + 1 < n_heads:
            s_next = scores(h + 1)
        v = jnp.concatenate([r[:, head_slice(h)] for r in v_refs], axis=0)
        o_ref[:, head_slice(h)] = _softmax_pv(s, v).astype(BF16)


def _attn_a_prompt(qkv, bias, *, batch, seq, d_model):
    tq = A_TQ
    hw = A_HEADS_PER_STEP * A_HEAD_DIM
    nhb = d_model // hw
    nqb = seq // tq

    def kv_spec(back, col0):
        return pl.BlockSpec(
            (tq, hw), lambda b, h, i: (b * nqb + jnp.maximum(i - back, 0), col0 + h))

    return pl.pallas_call(
        _attn_a_prompt_kernel,
        grid=(batch, nhb, nqb),
        in_specs=[
            pl.BlockSpec((tq, hw), lambda b, h, i: (b * nqb + i, h)),
            kv_spec(2, nhb), kv_spec(1, nhb), kv_spec(0, nhb),
            kv_spec(2, 2 * nhb), kv_spec(1, 2 * nhb), kv_spec(0, 2 * nhb),
            pl.BlockSpec((1, A_HEADS_PER_STEP, tq, 3 * tq),
                         lambda b, h, i: (jnp.minimum(i, 2), h, 0, 0)),
        ],
        out_specs=pl.BlockSpec((tq, hw), lambda b, h, i: (b * nqb + i, h)),
        out_shape=jax.ShapeDtypeStruct((batch * seq, d_model), BF16),
        compiler_params=pltpu.CompilerParams(
            dimension_semantics=("parallel", "parallel", "parallel"),
            vmem_limit_bytes=VMEM_LIMIT_BYTES),
        name="attn_a_prompt",
    )(qkv, qkv, qkv, qkv, qkv, qkv, qkv, bias)


def _attn_a_sample_kernel(q_ref, kc_ref, vc_ref, kn_ref, vn_ref, bias_ref, o_ref):
    t = q_ref.shape[0]
    kn, vn = kn_ref[0], vn_ref[0]
    pad = jnp.zeros((NEW_KEY_PAD - t, A_HEAD_DIM), BF16)
    for h in range(q_ref.shape[1] // A_HEAD_DIM):
        sl = slice(h * A_HEAD_DIM, (h + 1) * A_HEAD_DIM)
        k = jnp.concatenate([kc_ref[0, :, sl], kn[:, sl].astype(BF16), pad], axis=0)
        v = jnp.concatenate([vc_ref[0, :, sl], vn[:, sl].astype(BF16), pad], axis=0)
        s = _dot_nt(q_ref[:, sl], k) + bias_ref[h]
        o_ref[:, sl] = _softmax_pv(s, v).astype(BF16)


def _attn_a_sample(qkv, k_new, v_new, cache_k, cache_v, slot, bias, *, d_model):
    _, nb, cl, _ = cache_k.shape
    t = k_new.shape[1]
    hw = A_HEADS_PER_STEP * A_HEAD_DIM
    nhb = d_model // hw
    cache_spec = pl.BlockSpec((None, 1, cl, hw), lambda b, h: (slot, b, 0, h))
    new_spec = pl.BlockSpec((1, t, hw), lambda b, h: (b, 0, h))
    return pl.pallas_call(
        _attn_a_sample_kernel,
        grid=(nb, nhb),
        in_specs=[
            pl.BlockSpec((t, hw), lambda b, h: (b, h)),
            cache_spec, cache_spec, new_spec, new_spec,
            pl.BlockSpec((A_HEADS_PER_STEP, t, cl + NEW_KEY_PAD), lambda b, h: (h, 0, 0)),
        ],
        out_specs=pl.BlockSpec((t, hw), lambda b, h: (b, h)),
        out_shape=jax.ShapeDtypeStruct((nb * t, d_model), BF16),
        compiler_params=pltpu.CompilerParams(
            dimension_semantics=("parallel", "parallel"),
            vmem_limit_bytes=VMEM_LIMIT_BYTES),
        name="attn_a_sample",
    )(qkv, cache_k, cache_v, k_new, v_new, bias)


def _attn_b_heads(q_ref, k_blocks, v_blocks, bias_ref, sink_ref, o_ref):
    tq = q_ref.shape[0]
    n_pairs = B_KV_HEADS // 2
    bpp = q_ref.shape[1] // LANES // n_pairs
    low = lax.broadcasted_iota(jnp.int32, (tq, LANES), 1) < B_HEAD_DIM

    def scores(pair):
        pair_sl = slice(pair * LANES, (pair + 1) * LANES)
        k = jnp.concatenate([kb(pair_sl) for kb in k_blocks], axis=0)
        low_k = lax.broadcasted_iota(jnp.int32, k.shape, 1) < B_HEAD_DIM
        zero = jnp.zeros_like(k)
        k_diag = jnp.concatenate([jnp.where(low_k, k, zero), jnp.where(low_k, zero, k)], axis=0)
        q_stack = jnp.concatenate(
            [q_ref[:, (pair * bpp + c) * LANES:(pair * bpp + c + 1) * LANES]
             for c in range(bpp)], axis=0)
        return _dot_nt(q_stack, k_diag) + bias_ref[0, pair]

    def softmax_pv(pair, s):
        pair_sl = slice(pair * LANES, (pair + 1) * LANES)
        v = jnp.concatenate([vb(pair_sl) for vb in v_blocks], axis=0)
        keys = v.shape[0]
        m_even = s[:, :keys].max(axis=-1, keepdims=True)
        m_odd = s[:, keys:].max(axis=-1, keepdims=True)
        p = jnp.concatenate(
            [jnp.exp2(s[:, :keys] - m_even), jnp.exp2(s[:, keys:] - m_odd)], axis=1).astype(BF16)
        low_k = lax.broadcasted_iota(jnp.int32, v.shape, 1) < B_HEAD_DIM
        zero = jnp.zeros_like(v)
        ones_low = jnp.where(low_k, 1.0, 0.0).astype(BF16)
        ones_high = jnp.where(low_k, 0.0, 1.0).astype(BF16)
        v_diag = jnp.concatenate(
            [jnp.concatenate([jnp.where(low_k, v, zero), ones_low], axis=1),
             jnp.concatenate([jnp.where(low_k, zero, v), ones_high], axis=1)], axis=0)
        return m_even, m_odd, jnp.dot(p, v_diag, preferred_element_type=F32)

    def finish(pair, m_even, m_odd, out):
        for c in range(bpp):
            rows = slice(c * tq, (c + 1) * tq)
            cb = pair * bpp + c
            sink_even = jnp.exp2(sink_ref[0, 2 * cb] - m_even[rows])
            sink_odd = jnp.exp2(sink_ref[0, 2 * cb + 1] - m_odd[rows])
            den = out[rows, LANES:] + jnp.where(low, sink_even, sink_odd)
            o_ref[:, cb * LANES:(cb + 1) * LANES] = (out[rows, :LANES] / den).astype(BF16)

    s_vals, pv_vals = {}, {}
    for t in range(n_pairs + 2):
        if t < n_pairs:
            s_vals[t] = scores(t)
        if 0 <= t - 1 < n_pairs:
            pv_vals[t - 1] = softmax_pv(t - 1, s_vals.pop(t - 1))
        if 0 <= t - 2 < n_pairs:
            finish(t - 2, *pv_vals.pop(t - 2))


def _attn_b_prompt_kernel(q_ref, k0_ref, k1_ref, v0_ref, v1_ref, bias_ref, sink_ref, o_ref):
    _attn_b_heads(
        q_ref,
        [lambda sl: k0_ref[:, sl], lambda sl: k1_ref[:, sl]],
        [lambda sl: v0_ref[:, sl], lambda sl: v1_ref[:, sl]],
        bias_ref, sink_ref, o_ref)


def _attn_b_prompt(qkv, bias, sinks, *, batch, seq, d_model, kv_width):
    tq = B_TQ
    nqb = seq // tq
    nq_cols = d_model // kv_width

    def kv_spec(back, col):
        return pl.BlockSpec(
            (tq, kv_width), lambda b, i: (b * nqb + jnp.maximum(i - back, 0), col))

    return pl.pallas_call(
        _attn_b_prompt_kernel,
        grid=(batch, nqb),
        in_specs=[
            pl.BlockSpec((tq, d_model), lambda b, i: (b * nqb + i, 0)),
            kv_spec(1, nq_cols), kv_spec(0, nq_cols),
            kv_spec(1, nq_cols + 1), kv_spec(0, nq_cols + 1),
            pl.BlockSpec((1,) + bias.shape[1:],
                         lambda b, i: (jnp.where(i == 0, 1, 0), 0, 0, 0)),
            pl.BlockSpec(memory_space=pltpu.SMEM),
        ],
        out_specs=pl.BlockSpec((tq, d_model), lambda b, i: (b * nqb + i, 0)),
        out_shape=jax.ShapeDtypeStruct((batch * seq, d_model), BF16),
        compiler_params=pltpu.CompilerParams(
            dimension_semantics=("parallel", "parallel"),
            vmem_limit_bytes=VMEM_LIMIT_BYTES),
        name="attn_b_prompt",
    )(qkv, qkv, qkv, qkv, qkv, bias, sinks)


def _attn_b_sample_kernel(q_ref, kc_ref, vc_ref, kn_ref, vn_ref, bias_ref, sink_ref,
                          o_ref, ko_ref, vo_ref):
    t = q_ref.shape[0]
    cl = kc_ref.shape[1]
    kc, vc, kn, vn = kc_ref[0], vc_ref[0], kn_ref[0], vn_ref[0]
    ko_ref[0, :cl - t, :] = kc[t:, :]
    ko_ref[0, cl - t:, :] = kn
    vo_ref[0, :cl - t, :] = vc[t:, :]
    vo_ref[0, cl - t:, :] = vn
    pad = jnp.zeros((NEW_KEY_PAD - t, LANES), BF16)
    _attn_b_heads(
        q_ref,
        [lambda sl: kc[:, sl].astype(BF16),
         lambda sl: jnp.concatenate([kn[:, sl].astype(BF16), pad], axis=0)],
        [lambda sl: vc[:, sl].astype(BF16),
         lambda sl: jnp.concatenate([vn[:, sl].astype(BF16), pad], axis=0)],
        bias_ref, sink_ref, o_ref)


def _attn_b_sample(qkv, k_new, v_new, cache_k, cache_v, bias, sinks, *, d_model):
    nb, cl, kv_width = cache_k.shape
    t = k_new.shape[1]
    cache_spec = pl.BlockSpec((1, cl, kv_width), lambda b: (b, 0, 0))
    new_spec = pl.BlockSpec((1, t, kv_width), lambda b: (b, 0, 0))
    return pl.pallas_call(
        _attn_b_sample_kernel,
        grid=(nb,),
        in_specs=[
            pl.BlockSpec((t, d_model), lambda b: (b, 0)),
            cache_spec, cache_spec, new_spec, new_spec,
            pl.BlockSpec(bias.shape, lambda b: (0, 0, 0, 0)),
            pl.BlockSpec(memory_space=pltpu.SMEM),
        ],
        out_specs=[pl.BlockSpec((t, d_model), lambda b: (b, 0)), cache_spec, cache_spec],
        out_shape=[
            jax.ShapeDtypeStruct((nb * t, d_model), BF16),
            jax.ShapeDtypeStruct(cache_k.shape, F32),
            jax.ShapeDtypeStruct(cache_v.shape, F32),
        ],
        compiler_params=pltpu.CompilerParams(
            dimension_semantics=("parallel",),
            vmem_limit_bytes=VMEM_LIMIT_BYTES),
        name="attn_b_sample",
    )(qkv, cache_k, cache_v, k_new, v_new, bias, sinks)


def _oproj_mlp_kernel(*refs, n_x, n_a, final_norm):
    x_refs = refs[:n_x]
    a_refs = refs[n_x:n_x + n_a]
    wo_ref, g_ref, wup_ref, wdn_ref, gfin_ref, o_ref, h_ref, r_ref = refs[n_x + n_a:]
    f = pl.program_id(1)

    @pl.when(f == 0)
    def _():
        ka = a_refs[0].shape[1]
        proj = None
        for j, a_ref in enumerate(a_refs):
            part = jnp.dot(a_ref[...], wo_ref[j * ka:(j + 1) * ka, :], preferred_element_type=F32)
            proj = part if proj is None else proj + part
        width = x_refs[0].shape[1]
        sum_sq = None
        for j, x_ref in enumerate(x_refs):
            cols = slice(j * width, (j + 1) * width)
            x1 = x_ref[...] + proj[:, cols]
            o_ref[:, cols] = x1
            h_ref[:, cols] = (x1 * g_ref[:, cols]).astype(BF16)
            part = jnp.sum(x1 * x1, axis=-1, keepdims=True)
            sum_sq = part if sum_sq is None else sum_sq + part
        r_sq = 1.0 / (sum_sq / o_ref.shape[1] + RMS_EPS)
        r_ref[...] = jnp.broadcast_to(r_sq, r_ref.shape)

    u = jnp.dot(h_ref[...], wup_ref[...], preferred_element_type=F32)
    act = jnp.square(jnp.maximum(u, 0.0)).astype(BF16)
    down = jnp.dot(act, wdn_ref[...], preferred_element_type=F32)
    r_sq = r_ref[...]
    for c in range(down.shape[1] // LANES):
        cols = slice(c * LANES, (c + 1) * LANES)
        o_ref[:, cols] += down[:, cols] * r_sq

    if final_norm:
        @pl.when(f == pl.num_programs(1) - 1)
        def _():
            o_ref[...] = _rms_scale(o_ref[...]) * gfin_ref[...]


def _oproj_mlp(x, attn, wo, slot, g, wup, wdn, layer, gfin, *, tm, tf, final_norm):
    rows, d = x.shape
    d_ff = wup.shape[2]
    n_m, n_f = rows // tm, d_ff // tf
    n_x, n_a = 4, 2
    assert n_f > n_x + n_a
    kernel = functools.partial(_oproj_mlp_kernel, n_x=n_x, n_a=n_a, final_norm=final_norm)
    return pl.pallas_call(
        kernel,
        grid=(n_m, n_f),
        in_specs=_staggered_row_specs(tm, d, n_x, n_m, n_f - n_x)
        + _staggered_row_specs(tm, d, n_a, n_m, n_f - n_x - n_a) + [
            pl.BlockSpec((None, d, d), lambda m, f: (slot, 0, 0), pipeline_mode=pl.Buffered(1)),
            pl.BlockSpec((1, d), lambda m, f: (0, 0)),
            pl.BlockSpec((None, d, tf), lambda m, f: (layer, 0, f)),
            pl.BlockSpec((None, tf, d), lambda m, f: (layer, f, 0)),
            pl.BlockSpec((1, d), lambda m, f: (0, 0)),
        ],
        out_specs=pl.BlockSpec((tm, d), lambda m, f: (m, 0)),
        out_shape=jax.ShapeDtypeStruct((rows, d), F32),
        scratch_shapes=[pltpu.VMEM((tm, d), BF16), pltpu.VMEM((tm, LANES), F32)],
        compiler_params=pltpu.CompilerParams(
            dimension_semantics=("arbitrary", "arbitrary"),
            vmem_limit_bytes=VMEM_LIMIT_BYTES),
        name="oproj_mlp",
    )(*([x] * n_x), *([attn] * n_a), wo, g, wup, wdn, gfin)


def _band_mask(n_q_chunks, n_k_chunks, n_prev):
    ci = np.arange(n_q_chunks * CHUNK)[:, None] // CHUNK
    cj = np.arange(n_k_chunks * CHUNK)[None, :] // CHUNK
    return (cj >= ci) & (cj <= ci + n_prev)


def _rel_bias_tile(table, n_rows, n_cols, key_offset):
    length = n_rows + n_cols
    rel = np.arange(length) - (n_rows - 1) - key_offset
    diag = table[:, np.clip(rel, -A_REL_CLIP, A_REL_CLIP) + A_REL_CLIP]
    flat = jnp.tile(diag, (1, n_rows))[:, :n_rows * (length - 1)]
    return flat.reshape(table.shape[0], n_rows, length - 1)[:, :, n_rows - 1:n_rows - 1 + n_cols]


def _a_prompt_bias(table):
    bias = _rel_bias_tile(table, A_TQ, 3 * A_TQ, 2 * A_TQ)
    band = _band_mask(A_TQ // CHUNK, 3 * A_TQ // CHUNK, A_PREV_CHUNKS)
    col = np.arange(3 * A_TQ)[None, :]
    masks = np.stack([band & (col >= (2 - v) * A_TQ) for v in range(3)])
    return jnp.where(masks[:, None], bias[None], NEG_INF)


def _a_sample_bias(table, t, cl):
    bias = _rel_bias_tile(table, t, cl + NEW_KEY_PAD, cl)
    valid = np.broadcast_to(np.arange(cl + NEW_KEY_PAD)[None, :] < cl + t, (t, cl + NEW_KEY_PAD))
    return jnp.where(valid[None], bias, NEG_INF)


def _pair_heads(x, axis):
    n_heads = x.shape[axis]
    group = n_heads // B_KV_HEADS
    split = x.shape[:axis] + (B_KV_HEADS // 2, 2, group) + x.shape[axis + 1:]
    return jnp.swapaxes(x.reshape(split), axis + 1, axis + 2).reshape(x.shape)


def _alibi_tile(n_q_heads, n_rows, n_cols, key_offset, valid):
    group = n_q_heads // B_KV_HEADS
    slopes = (2.0 ** (-8.0 * np.arange(1, n_q_heads + 1) / n_q_heads)).astype(np.float32)
    qi = np.arange(n_rows)[:, None]
    kj = np.arange(n_cols)[None, :]
    rel = np.abs(kj - key_offset - qi).astype(np.float32)
    alibi = (-slopes[:, None, None] * rel[None]).astype(np.float64)
    bias = np.where(valid[None], alibi * LOG2E, NEG_INF)
    stacked = [[np.concatenate([bias[group * (2 * pair) + g], bias[group * (2 * pair + 1) + g]], axis=1)
                for g in range(group)] for pair in range(B_KV_HEADS // 2)]
    return np.asarray(stacked, np.float32).reshape(B_KV_HEADS // 2, group * n_rows, 2 * n_cols)


def _b_prompt_bias(n_q_heads):
    mask = _band_mask(B_TQ // CHUNK, 2 * B_TQ // CHUNK, B_PREV_CHUNKS)
    first = mask & (np.arange(2 * B_TQ)[None, :] >= B_TQ)
    return np.stack([_alibi_tile(n_q_heads, B_TQ, 2 * B_TQ, B_TQ, mask),
                     _alibi_tile(n_q_heads, B_TQ, 2 * B_TQ, B_TQ, first)])


def _b_sample_bias(n_q_heads, t, cl):
    valid = np.broadcast_to(np.arange(cl + NEW_KEY_PAD)[None, :] < cl + t, (t, cl + NEW_KEY_PAD))
    return _alibi_tile(n_q_heads, t, cl + NEW_KEY_PAD, cl, valid)[None]


def kernel(x_prompt, x_sample, cache_a_k, cache_a_v, cache_b_k, cache_b_v, norm_mix, norm_ffn,
           norm_final, a_w_qkv, a_w_o, a_rel_bias, b_w_qkv, b_w_o, b_sinks, w_up, w_down):
    batch, seq, d = x_prompt.shape
    dec_batch, dec_seq, _ = x_sample.shape
    depth = norm_mix.shape[0]
    a_kv_width = cache_a_k.shape[3] * cache_a_k.shape[4]
    b_kv_width = cache_b_k.shape[3] * cache_b_k.shape[4]
    cl_a, cl_b = cache_a_k.shape[2], cache_b_k.shape[2]
    n_b_heads = d // B_HEAD_DIM
    tail_a = min(A_PREV_CHUNKS * CHUNK, seq)
    tail_b = min(B_PREV_CHUNKS * CHUNK, seq)

    tm = 512
    tm_qkv = 1024
    tf = 1024
    rows_s = dec_batch * dec_seq

    n_slots = b_w_qkv.shape[0]
    a_w_qkv16 = a_w_qkv.astype(BF16)
    a_w_o16 = a_w_o.astype(BF16)
    b_q16 = _pair_heads(b_w_qkv[:, :, :d].astype(BF16).reshape(n_slots, d, n_b_heads, B_HEAD_DIM), 2)
    b_w_qkv16 = jnp.concatenate(
        [b_q16.reshape(n_slots, d, d), b_w_qkv[:, :, d:].astype(BF16)], axis=2)
    b_w_o16 = _pair_heads(
        b_w_o.astype(BF16).reshape(n_slots, n_b_heads, B_HEAD_DIM, d), 1).reshape(n_slots, d, d)
    w_up16 = w_up.astype(BF16)
    w_down16 = w_down.astype(BF16)
    sinks_perm = _pair_heads(b_sinks, 1) * LOG2E
    a_tables = a_rel_bias * LOG2E
    cache_a_k16 = cache_a_k.astype(BF16).reshape(cache_a_k.shape[:3] + (a_kv_width,))
    cache_a_v16 = cache_a_v.astype(BF16).reshape(cache_a_v.shape[:3] + (a_kv_width,))

    b_bias_prompt = jnp.asarray(_b_prompt_bias(n_b_heads))
    b_bias_sample = jnp.asarray(_b_sample_bias(n_b_heads, dec_seq, cl_b))

    xp = x_prompt.reshape(batch * seq, d)
    xs = x_sample.reshape(rows_s, d)
    gfin = norm_final.reshape(1, d)

    states = {k: [] for k in ("a_kp", "a_vp", "a_ks", "a_vs", "b_kp", "b_vp", "b_ks", "b_vs")}
    for layer in range(depth):
        slot = layer // N_MIXERS
        g_mix = norm_mix[layer].reshape(1, d)
        g_ffn = norm_ffn[layer].reshape(1, d)
        if layer % N_MIXERS == 0:
            qkv_args = dict(tn=1024, q_width=d, kv_width=a_kv_width,
                            q_scale=A_HEAD_DIM ** -0.5 * LOG2E)
            qkv_p, kp, vp = _norm_qkv(xp, g_mix, a_w_qkv16, slot, tm=tm_qkv,
                                      tiles_per_batch=seq // tm_qkv, tail_rows=tail_a, **qkv_args)
            qkv_s, kn, vn = _norm_qkv(xs, g_mix, a_w_qkv16, slot, tm=rows_s,
                                      tiles_per_batch=1, tail_rows=rows_s, **qkv_args)
            table = a_tables[slot]
            mp = _attn_a_prompt(qkv_p, _a_prompt_bias(table), batch=batch, seq=seq, d_model=d)
            ms = _attn_a_sample(
                qkv_s, kn.reshape(dec_batch, dec_seq, a_kv_width),
                vn.reshape(dec_batch, dec_seq, a_kv_width),
                cache_a_k16, cache_a_v16, slot,
                _a_sample_bias(table, dec_seq, cl_a), d_model=d)
            w_o = a_w_o16
            states["a_kp"].append(kp.reshape((batch, tail_a) + cache_a_k.shape[3:]))
            states["a_vp"].append(vp.reshape((batch, tail_a) + cache_a_k.shape[3:]))
            states["a_ks"].append(kn.reshape((dec_batch, dec_seq) + cache_a_k.shape[3:]))
            states["a_vs"].append(vn.reshape((dec_batch, dec_seq) + cache_a_k.shape[3:]))
        else:
            qkv_args = dict(tn=2 * b_kv_width, q_width=d, kv_width=b_kv_width,
                            q_scale=B_HEAD_DIM ** -0.5 * LOG2E)
            qkv_p, kp, vp = _norm_qkv(xp, g_mix, b_w_qkv16, slot, tm=tm_qkv,
                                      tiles_per_batch=seq // tm_qkv, tail_rows=tail_b, **qkv_args)
            qkv_s, kn, vn = _norm_qkv(xs, g_mix, b_w_qkv16, slot, tm=rows_s,
                                      tiles_per_batch=1, tail_rows=rows_s, **qkv_args)
            sinks = sinks_perm[slot].reshape(1, n_b_heads)
            mp = _attn_b_prompt(qkv_p, b_bias_prompt, sinks, batch=batch, seq=seq,
                                d_model=d, kv_width=b_kv_width)
            ms, ks, vs = _attn_b_sample(
                qkv_s, kn.reshape(dec_batch, dec_seq, b_kv_width),
                vn.reshape(dec_batch, dec_seq, b_kv_width),
                cache_b_k[slot].reshape(dec_batch, cl_b, b_kv_width),
                cache_b_v[slot].reshape(dec_batch, cl_b, b_kv_width),
                b_bias_sample, sinks, d_model=d)
            w_o = b_w_o16
            states["b_kp"].append(kp.reshape((batch, tail_b) + cache_b_k.shape[3:]))
            states["b_vp"].append(vp.reshape((batch, tail_b) + cache_b_k.shape[3:]))
            states["b_ks"].append(ks.reshape(cache_b_k.shape[1:]))
            states["b_vs"].append(vs.reshape(cache_b_k.shape[1:]))
        last = layer == depth - 1
        xp = _oproj_mlp(xp, mp, w_o, slot, g_ffn, w_up16, w_down16, layer, gfin,
                        tm=tm, tf=tf, final_norm=last)
        xs = _oproj_mlp(xs, ms, w_o, slot, g_ffn, w_up16, w_down16, layer, gfin,
                        tm=rows_s, tf=tf, final_norm=last)

    def rolled(cache, new_rows):
        return jnp.concatenate([cache, jnp.stack(new_rows)], axis=2)[:, :, dec_seq:]

    return (xp.reshape(batch, seq, d), xs.reshape(dec_batch, dec_seq, d),
            jnp.stack(states["a_kp"]), jnp.stack(states["a_vp"]),
            jnp.stack(states["b_kp"]), jnp.stack(states["b_vp"]),
            rolled(cache_a_k, states["a_ks"]), rolled(cache_a_v, states["a_vs"]),
            jnp.stack(states["b_ks"]), jnp.stack(states["b_vs"]))
```

```python
import functools

import jax
import jax.numpy as jnp
import numpy as np
from jax import lax
from jax.experimental import pallas as pl
from jax.experimental.pallas import tpu as pltpu

F32 = jnp.float32
BF16 = jnp.bfloat16

CHUNK = 64
N_MIXERS = 2
A_HEAD_DIM = 128
A_PREV_CHUNKS = 8
A_REL_CLIP = 128
B_PREV_CHUNKS = 2
B_HEAD_DIM = 64
B_KV_HEADS = 8
RMS_EPS = 1e-6
NEG_INF = -1e30
LOG2E = float(np.log2(np.e))

LANES = 128
VMEM_LIMIT_BYTES = 56 * 1024 * 1024

A_TQ = 256
A_HEADS_PER_STEP = 8
B_TQ = 128
NEW_KEY_PAD = 128


def _rms_scale(x):
    return x * lax.rsqrt(jnp.mean(x * x, axis=-1, keepdims=True) + RMS_EPS)


def _scale_rows(acc, r):
    return jnp.concatenate(
        [acc[:, c * LANES:(c + 1) * LANES] * r for c in range(acc.shape[1] // LANES)], axis=1)


def _staggered_row_specs(tm, width, n_parts, n_m, first_step):
    assert first_step >= 1 and width % n_parts == 0

    def spec(j):
        return pl.BlockSpec(
            (tm, width // n_parts),
            lambda m, s: (jnp.minimum(m + (s >= first_step + j), n_m - 1), j))

    return [spec(j) for j in range(n_parts)]


def _dot_nt(a, b):
    return lax.dot_general(a, b, (((1,), (1,)), ((), ())), preferred_element_type=F32)


def _softmax_pv(s, v):
    m = s.max(axis=-1, keepdims=True)
    p = jnp.exp2(s - m).astype(BF16)
    out = jnp.dot(p, jnp.concatenate([v, jnp.ones_like(v)], axis=1), preferred_element_type=F32)
    width = v.shape[1]
    return out[:, :width] / out[:, width:]


def _norm_qkv_kernel(*refs, n_x, nq, nk, q_scale, tiles_per_batch, tail_rows):
    x_refs = refs[:n_x]
    g_ref, w_ref, qkv_ref, kst_ref, vst_ref, h_ref, r_ref = refs[n_x:]
    m = pl.program_id(0)
    n = pl.program_id(1)

    @pl.when(n == 0)
    def _():
        width = x_refs[0].shape[1]
        sum_sq = None
        for j, x_ref in enumerate(x_refs):
            cols = slice(j * width, (j + 1) * width)
            x = x_ref[...]
            h_ref[:, cols] = (x * g_ref[:, cols]).astype(BF16)
            part = jnp.sum(x * x, axis=-1, keepdims=True)
            sum_sq = part if sum_sq is None else sum_sq + part
        r = lax.rsqrt(sum_sq / h_ref.shape[1] + RMS_EPS)
        r_ref[...] = jnp.broadcast_to(r, r_ref.shape)

    acc = jnp.dot(h_ref[...], w_ref[...], preferred_element_type=F32)
    r = r_ref[...]
    r_out = r * jnp.where(n < nq, q_scale, 1.0)
    tm, tn = acc.shape
    for c in range(tn // LANES):
        cols = slice(c * LANES, (c + 1) * LANES)
        qkv_ref[:, cols] = (acc[:, cols] * r_out).astype(BF16)

    is_tail = (m % tiles_per_batch) == tiles_per_batch - 1

    def tail():
        return _scale_rows(acc[tm - tail_rows:, :], r[tm - tail_rows:, :])

    if nk == 0:
        @pl.when(is_tail & (n == nq))
        def _():
            kv_width = kst_ref.shape[2]
            kst_ref[0] = tail()[:, :kv_width]
            vst_ref[0] = tail()[:, kv_width:]
    else:
        @pl.when(is_tail & (n >= nq) & (n < nq + nk))
        def _():
            kst_ref[0] = tail()

        @pl.when(is_tail & (n >= nq + nk))
        def _():
            vst_ref[0] = tail()


def _norm_qkv(x, g, w, layer, *, tm, tn, q_width, kv_width, q_scale, tiles_per_batch, tail_rows):
    rows, d = x.shape
    n_total = w.shape[2]
    nq = q_width // tn
    nk = kv_width // tn
    assert (nk > 0 and kv_width % tn == 0) or tn == 2 * kv_width
    state_width = min(tn, kv_width)
    nb = rows // (tm * tiles_per_batch)

    def state_map(first):
        def index_map(m, n):
            tail = (m % tiles_per_batch) == tiles_per_batch - 1
            j = jnp.clip(n - first, 0, max(nk - 1, 0))
            return (m // tiles_per_batch, 0, jnp.where(tail, j, 0))
        return index_map

    n_m, n_n = rows // tm, n_total // tn
    n_x = min(4, n_n - 1)
    kernel = functools.partial(_norm_qkv_kernel, n_x=n_x, nq=nq, nk=nk, q_scale=q_scale,
                               tiles_per_batch=tiles_per_batch, tail_rows=tail_rows)
    return pl.pallas_call(
        kernel,
        grid=(n_m, n_n),
        in_specs=_staggered_row_specs(tm, d, n_x, n_m, n_n - n_x) + [
            pl.BlockSpec((1, d), lambda m, n: (0, 0)),
            pl.BlockSpec((None, d, tn), lambda m, n: (layer, 0, n)),
        ],
        out_specs=[
            pl.BlockSpec((tm, tn), lambda m, n: (m, n)),
            pl.BlockSpec((1, tail_rows, state_width), state_map(nq)),
            pl.BlockSpec((1, tail_rows, state_width), state_map(nq + nk)),
        ],
        out_shape=[
            jax.ShapeDtypeStruct((rows, n_total), BF16),
            jax.ShapeDtypeStruct((nb, tail_rows, kv_width), F32),
            jax.ShapeDtypeStruct((nb, tail_rows, kv_width), F32),
        ],
        scratch_shapes=[pltpu.VMEM((tm, d), BF16), pltpu.VMEM((tm, LANES), F32)],
        compiler_params=pltpu.CompilerParams(
            dimension_semantics=("arbitrary", "arbitrary"),
            vmem_limit_bytes=VMEM_LIMIT_BYTES),
        name="norm_qkv",
    )(*([x] * n_x), g, w)


def _attn_a_prompt_kernel(q_ref, k0_ref, k1_ref, k2_ref, v0_ref, v1_ref, v2_ref,
                          bias_ref, o_ref):
    k_refs = (k0_ref, k1_ref, k2_ref)
    v_refs = (v0_ref, v1_ref, v2_ref)
    n_heads = q_ref.shape[1] // A_HEAD_DIM

    def head_slice(h):
        return slice(h * A_HEAD_DIM, (h + 1) * A_HEAD_DIM)

    def scores(h):
        k = jnp.concatenate([r[:, head_slice(h)] for r in k_refs], axis=0)
        return _dot_nt(q_ref[:, head_slice(h)], k) + bias_ref[0, h]

    s_next = scores(0)
    for h in range(n_heads):
        s = s_next
        if h + 1 < n_heads:
            s_next = scores(h + 1)
        v = jnp.concatenate([r[:, head_slice(h)] for r in v_refs], axis=0)
        o_ref[:, head_slice(h)] = _softmax_pv(s, v).astype(BF16)


def _attn_a_prompt(qkv, bias, *, batch, seq, d_model):
    tq = A_TQ
    hw = A_HEADS_PER_STEP * A_HEAD_DIM
    nhb = d_model // hw
    nqb = seq // tq

    def kv_spec(back, col0):
        return pl.BlockSpec(
            (tq, hw), lambda b, h, i: (b * nqb + jnp.maximum(i - back, 0), col0 + h))

    return pl.pallas_call(
        _attn_a_prompt_kernel,
        grid=(batch, nhb, nqb),
        in_specs=[
            pl.BlockSpec((tq, hw), lambda b, h, i: (b * nqb + i, h)),
            kv_spec(2, nhb), kv_spec(1, nhb), kv_spec(0, nhb),
            kv_spec(2, 2 * nhb), kv_spec(1, 2 * nhb), kv_spec(0, 2 * nhb),
            pl.BlockSpec((1, A_HEADS_PER_STEP, tq, 3 * tq),
                         lambda b, h, i: (jnp.minimum(i, 2), h, 0, 0)),
        ],
        out_specs=pl.BlockSpec((tq, hw), lambda b, h, i: (b * nqb + i, h)),
        out_shape=jax.ShapeDtypeStruct((batch * seq, d_model), BF16),
        compiler_params=pltpu.CompilerParams(
            dimension_semantics=("parallel", "parallel", "parallel"),
            vmem_limit_bytes=VMEM_LIMIT_BYTES),
        name="attn_a_prompt",
    )(qkv, qkv, qkv, qkv, qkv, qkv, qkv, bias)


def _attn_a_sample_kernel(q_ref, kc_ref, vc_ref, kn_ref, vn_ref, bias_ref, o_ref):
    t = q_ref.shape[0]
    kn, vn = kn_ref[0], vn_ref[0]
    pad = jnp.zeros((NEW_KEY_PAD - t, A_HEAD_DIM), BF16)
    for h in range(q_ref.shape[1] // A_HEAD_DIM):
        sl = slice(h * A_HEAD_DIM, (h + 1) * A_HEAD_DIM)
        k = jnp.concatenate([kc_ref[0, :, sl], kn[:, sl].astype(BF16), pad], axis=0)
        v = jnp.concatenate([vc_ref[0, :, sl], vn[:, sl].astype(BF16), pad], axis=0)
        s = _dot_nt(q_ref[:, sl], k) + bias_ref[h]
        o_ref[:, sl] = _softmax_pv(s, v).astype(BF16)


def _attn_a_sample(qkv, k_new, v_new, cache_k, cache_v, slot, bias, *, d_model):
    _, nb, cl, _ = cache_k.shape
    t = k_new.shape[1]
    hw = A_HEADS_PER_STEP * A_HEAD_DIM
    nhb = d_model // hw
    cache_spec = pl.BlockSpec((None, 1, cl, hw), lambda b, h: (slot, b, 0, h))
    new_spec = pl.BlockSpec((1, t, hw), lambda b, h: (b, 0, h))
    return pl.pallas_call(
        _attn_a_sample_kernel,
        grid=(nb, nhb),
        in_specs=[
            pl.BlockSpec((t, hw), lambda b, h: (b, h)),
            cache_spec, cache_spec, new_spec, new_spec,
            pl.BlockSpec((A_HEADS_PER_STEP, t, cl + NEW_KEY_PAD), lambda b, h: (h, 0, 0)),
        ],
        out_specs=pl.BlockSpec((t, hw), lambda b, h: (b, h)),
        out_shape=jax.ShapeDtypeStruct((nb * t, d_model), BF16),
        compiler_params=pltpu.CompilerParams(
            dimension_semantics=("parallel", "parallel"),
            vmem_limit_bytes=VMEM_LIMIT_BYTES),
        name="attn_a_sample",
    )(qkv, cache_k, cache_v, k_new, v_new, bias)


def _attn_b_heads(q_ref, k_blocks, v_blocks, bias_ref, sink_ref, o_ref):
    tq = q_ref.shape[0]
    n_pairs = B_KV_HEADS // 2
    bpp = q_ref.shape[1] // LANES // n_pairs
    low = lax.broadcasted_iota(jnp.int32, (tq, LANES), 1) < B_HEAD_DIM

    def scores(pair):
        pair_sl = slice(pair * LANES, (pair + 1) * LANES)
        k = jnp.concatenate([kb(pair_sl) for kb in k_blocks], axis=0)
        low_k = lax.broadcasted_iota(jnp.int32, k.shape, 1) < B_HEAD_DIM
        zero = jnp.zeros_like(k)
        k_diag = jnp.concatenate([jnp.where(low_k, k, zero), jnp.where(low_k, zero, k)], axis=0)
        q_stack = jnp.concatenate(
            [q_ref[:, (pair * bpp + c) * LANES:(pair * bpp + c + 1) * LANES]
             for c in range(bpp)], axis=0)
        return _dot_nt(q_stack, k_diag) + bias_ref[0, pair]

    def softmax_pv(pair, s):
        pair_sl = slice(pair * LANES, (pair + 1) * LANES)
        v = jnp.concatenate([vb(pair_sl) for vb in v_blocks], axis=0)
        keys = v.shape[0]
        m_even = s[:, :keys].max(axis=-1, keepdims=True)
        m_odd = s[:, keys:].max(axis=-1, keepdims=True)
        p = jnp.concatenate(
            [jnp.exp2(s[:, :keys] - m_even), jnp.exp2(s[:, keys:] - m_odd)], axis=1).astype(BF16)
        low_k = lax.broadcasted_iota(jnp.int32, v.shape, 1) < B_HEAD_DIM
        zero = jnp.zeros_like(v)
        ones_low = jnp.where(low_k, 1.0, 0.0).astype(BF16)
        ones_high = jnp.where(low_k, 0.0, 1.0).astype(BF16)
        v_diag = jnp.concatenate(
            [jnp.concatenate([jnp.where(low_k, v, zero), ones_low], axis=1),
             jnp.concatenate([jnp.where(low_k, zero, v), ones_high], axis=1)], axis=0)
        return m_even, m_odd, jnp.dot(p, v_diag, preferred_element_type=F32)

    def finish(pair, m_even, m_odd, out):
        for c in range(bpp):
            rows = slice(c * tq, (c + 1) * tq)
            cb = pair * bpp + c
            sink_even = jnp.exp2(sink_ref[0, 2 * cb] - m_even[rows])
            sink_odd = jnp.exp2(sink_ref[0, 2 * cb + 1] - m_odd[rows])
            den = out[rows, LANES:] + jnp.where(low, sink_even, sink_odd)
            o_ref[:, cb * LANES:(cb + 1) * LANES] = (out[rows, :LANES] / den).astype(BF16)

    s_vals, pv_vals = {}, {}
    for t in range(n_pairs + 2):
        if t < n_pairs:
            s_vals[t] = scores(t)
        if 0 <= t - 1 < n_pairs:
            pv_vals[t - 1] = softmax_pv(t - 1, s_vals.pop(t - 1))
        if 0 <= t - 2 < n_pairs:
            finish(t - 2, *pv_vals.pop(t - 2))


def _attn_b_prompt_kernel(q_ref, k0_ref, k1_ref, v0_ref, v1_ref, bias_ref, sink_ref, o_ref):
    _attn_b_heads(
        q_ref,
        [lambda sl: k0_ref[:, sl], lambda sl: k1_ref[:, sl]],
        [lambda sl: v0_ref[:, sl], lambda sl: v1_ref[:, sl]],
        bias_ref, sink_ref, o_ref)


def _attn_b_prompt(qkv, bias, sinks, *, batch, seq, d_model, kv_width):
    tq = B_TQ
    nqb = seq // tq
    nq_cols = d_model // kv_width

    def kv_spec(back, col):
        return pl.BlockSpec(
            (tq, kv_width), lambda b, i: (b * nqb + jnp.maximum(i - back, 0), col))

    return pl.pallas_call(
        _attn_b_prompt_kernel,
        grid=(batch, nqb),
        in_specs=[
            pl.BlockSpec((tq, d_model), lambda b, i: (b * nqb + i, 0)),
            kv_spec(1, nq_cols), kv_spec(0, nq_cols),
            kv_spec(1, nq_cols + 1), kv_spec(0, nq_cols + 1),
            pl.BlockSpec((1,) + bias.shape[1:],
                         lambda b, i: (jnp.where(i == 0, 1, 0), 0, 0, 0)),
            pl.BlockSpec(memory_space=pltpu.SMEM),
        ],
        out_specs=pl.BlockSpec((tq, d_model), lambda b, i: (b * nqb + i, 0)),
        out_shape=jax.ShapeDtypeStruct((batch * seq, d_model), BF16),
        compiler_params=pltpu.CompilerParams(
            dimension_semantics=("parallel", "parallel"),
            vmem_limit_bytes=VMEM_LIMIT_BYTES),
        name="attn_b_prompt",
    )(qkv, qkv, qkv, qkv, qkv, bias, sinks)


def _attn_b_sample_kernel(q_ref, kc_ref, vc_ref, kn_ref, vn_ref, bias_ref, sink_ref,
                          o_ref, ko_ref, vo_ref):
    t = q_ref.shape[0]
    cl = kc_ref.shape[1]
    kc, vc, kn, vn = kc_ref[0], vc_ref[0], kn_ref[0], vn_ref[0]
    ko_ref[0, :cl - t, :] = kc[t:, :]
    ko_ref[0, cl - t:, :] = kn
    vo_ref[0, :cl - t, :] = vc[t:, :]
    vo_ref[0, cl - t:, :] = vn
    pad = jnp.zeros((NEW_KEY_PAD - t, LANES), BF16)
    _attn_b_heads(
        q_ref,
        [lambda sl: kc[:, sl].astype(BF16),
         lambda sl: jnp.concatenate([kn[:, sl].astype(BF16), pad], axis=0)],
        [lambda sl: vc[:, sl].astype(BF16),
         lambda sl: jnp.concatenate([vn[:, sl].astype(BF16), pad], axis=0)],
        bias_ref, sink_ref, o_ref)


def _attn_b_sample(qkv, k_new, v_new, cache_k, cache_v, bias, sinks, *, d_model):
    nb, cl, kv_width = cache_k.shape
    t = k_new.shape[1]
    cache_spec = pl.BlockSpec((1, cl, kv_width), lambda b: (b, 0, 0))
    new_spec = pl.BlockSpec((1, t, kv_width), lambda b: (b, 0, 0))
    return pl.pallas_call(
        _attn_b_sample_kernel,
        grid=(nb,),
        in_specs=[
            pl.BlockSpec((t, d_model), lambda b: (b, 0)),
            cache_spec, cache_spec, new_spec, new_spec,
            pl.BlockSpec(bias.shape, lambda b: (0, 0, 0, 0)),
            pl.BlockSpec(memory_space=pltpu.SMEM),
        ],
        out_specs=[pl.BlockSpec((t, d_model), lambda b: (b, 0)), cache_spec, cache_spec],
        out_shape=[
            jax.ShapeDtypeStruct((nb * t, d_model), BF16),
            jax.ShapeDtypeStruct(cache_k.shape, F32),
            jax.ShapeDtypeStruct(cache_v.shape, F32),
        ],
        compiler_params=pltpu.CompilerParams(
            dimension_semantics=("parallel",),
            vmem_limit_bytes=VMEM_LIMIT_BYTES),
        name="attn_b_sample",
    )(qkv, cache_k, cache_v, k_new, v_new, bias, sinks)


def _oproj_mlp_kernel(x_ref, a_ref, wo_ref, g_ref, wup_hbm, wdn_hbm, gfin_ref, o_ref,
                      h_ref, r_ref, wup_buf, wdn_buf, sem, *, layer, tf, final_norm):
    m = pl.program_id(0)
    n_f = wup_hbm.shape[2] // tf

    def weight_copies(f, slot):
        return (
            pltpu.make_async_copy(wup_hbm.at[layer, :, pl.ds(f * tf, tf)], wup_buf.at[slot],
                                  sem.at[0, slot]),
            pltpu.make_async_copy(wdn_hbm.at[layer, pl.ds(f * tf, tf), :], wdn_buf.at[slot],
                                  sem.at[1, slot]))

    def start(f, slot):
        for copy in weight_copies(f, slot):
            copy.start()

    def wait(f, slot):
        for copy in weight_copies(f, slot):
            copy.wait()

    @pl.when(m == 0)
    def _():
        start(0, 0)

    x1 = x_ref[...] + jnp.dot(a_ref[...], wo_ref[...], preferred_element_type=F32)
    o_ref[...] = x1
    h_ref[...] = (x1 * g_ref[...]).astype(BF16)
    r = lax.rsqrt(jnp.mean(x1 * x1, axis=-1, keepdims=True) + RMS_EPS)
    r_ref[...] = jnp.broadcast_to(r, r_ref.shape)

    for f in range(n_f):
        slot = f % 2
        if f + 1 < n_f:
            start(f + 1, 1 - slot)
        else:
            @pl.when(m + 1 < pl.num_programs(0))
            def _():
                start(0, 0)
        wait(f, slot)
        u = _scale_rows(jnp.dot(h_ref[...], wup_buf[slot], preferred_element_type=F32), r_ref[...])
        act = jnp.square(jnp.maximum(u, 0.0)).astype(BF16)
        o_ref[...] += jnp.dot(act, wdn_buf[slot], preferred_element_type=F32)

    if final_norm:
        o_ref[...] = _rms_scale(o_ref[...]) * gfin_ref[...]


def _oproj_mlp(x, attn, wo, slot, g, wup, wdn, layer, gfin, *, tm, tf, final_norm):
    rows, d = x.shape
    d_ff = wup.shape[2]
    assert d_ff % tf == 0 and (d_ff // tf) % 2 == 0
    kernel = functools.partial(_oproj_mlp_kernel, layer=layer, tf=tf, final_norm=final_norm)
    return pl.pallas_call(
        kernel,
        grid=(rows // tm,),
        in_specs=[
            pl.BlockSpec((tm, d), lambda m: (m, 0)),
            pl.BlockSpec((tm, d), lambda m: (m, 0)),
            pl.BlockSpec((None, d, d), lambda m: (slot, 0, 0), pipeline_mode=pl.Buffered(1)),
            pl.BlockSpec((1, d), lambda m: (0, 0)),
            pl.BlockSpec(memory_space=pl.ANY),
            pl.BlockSpec(memory_space=pl.ANY),
            pl.BlockSpec((1, d), lambda m: (0, 0)),
        ],
        out_specs=pl.BlockSpec((tm, d), lambda m: (m, 0)),
        out_shape=jax.ShapeDtypeStruct((rows, d), F32),
        scratch_shapes=[
            pltpu.VMEM((tm, d), BF16), pltpu.VMEM((tm, LANES), F32),
            pltpu.VMEM((2, d, tf), BF16), pltpu.VMEM((2, tf, d), BF16),
            pltpu.SemaphoreType.DMA((2, 2)),
        ],
        compiler_params=pltpu.CompilerParams(
            dimension_semantics=("arbitrary",),
            vmem_limit_bytes=VMEM_LIMIT_BYTES),
        name="oproj_mlp",
    )(x, attn, wo, g, wup, wdn, gfin)


def _band_mask(n_q_chunks, n_k_chunks, n_prev):
    ci = np.arange(n_q_chunks * CHUNK)[:, None] // CHUNK
    cj = np.arange(n_k_chunks * CHUNK)[None, :] // CHUNK
    return (cj >= ci) & (cj <= ci + n_prev)


def _rel_bias_tile(table, n_rows, n_cols, key_offset):
    length = n_rows + n_cols
    rel = np.arange(length) - (n_rows - 1) - key_offset
    diag = table[:, np.clip(rel, -A_REL_CLIP, A_REL_CLIP) + A_REL_CLIP]
    flat = jnp.tile(diag, (1, n_rows))[:, :n_rows * (length - 1)]
    return flat.reshape(table.shape[0], n_rows, length - 1)[:, :, n_rows - 1:n_rows - 1 + n_cols]


def _a_prompt_bias(table):
    bias = _rel_bias_tile(table, A_TQ, 3 * A_TQ, 2 * A_TQ)
    band = _band_mask(A_TQ // CHUNK, 3 * A_TQ // CHUNK, A_PREV_CHUNKS)
    col = np.arange(3 * A_TQ)[None, :]
    masks = np.stack([band & (col >= (2 - v) * A_TQ) for v in range(3)])
    return jnp.where(masks[:, None], bias[None], NEG_INF)


def _a_sample_bias(table, t, cl):
    bias = _rel_bias_tile(table, t, cl + NEW_KEY_PAD, cl)
    valid = np.broadcast_to(np.arange(cl + NEW_KEY_PAD)[None, :] < cl + t, (t, cl + NEW_KEY_PAD))
    return jnp.where(valid[None], bias, NEG_INF)


def _pair_heads(x, axis):
    n_heads = x.shape[axis]
    group = n_heads // B_KV_HEADS
    split = x.shape[:axis] + (B_KV_HEADS // 2, 2, group) + x.shape[axis + 1:]
    return jnp.swapaxes(x.reshape(split), axis + 1, axis + 2).reshape(x.shape)


def _alibi_tile(n_q_heads, n_rows, n_cols, key_offset, valid):
    group = n_q_heads // B_KV_HEADS
    slopes = (2.0 ** (-8.0 * np.arange(1, n_q_heads + 1) / n_q_heads)).astype(np.float32)
    qi = np.arange(n_rows)[:, None]
    kj = np.arange(n_cols)[None, :]
    rel = np.abs(kj - key_offset - qi).astype(np.float32)
    alibi = (-slopes[:, None, None] * rel[None]).astype(np.float64)
    bias = np.where(valid[None], alibi * LOG2E, NEG_INF)
    stacked = [[np.concatenate([bias[group * (2 * pair) + g], bias[group * (2 * pair + 1) + g]], axis=1)
                for g in range(group)] for pair in range(B_KV_HEADS // 2)]
    return np.asarray(stacked, np.float32).reshape(B_KV_HEADS // 2, group * n_rows, 2 * n_cols)


def _b_prompt_bias(n_q_heads):
    mask = _band_mask(B_TQ // CHUNK, 2 * B_TQ // CHUNK, B_PREV_CHUNKS)
    first = mask & (np.arange(2 * B_TQ)[None, :] >= B_TQ)
    return np.stack([_alibi_tile(n_q_heads, B_TQ, 2 * B_TQ, B_TQ, mask),
                     _alibi_tile(n_q_heads, B_TQ, 2 * B_TQ, B_TQ, first)])


def _b_sample_bias(n_q_heads, t, cl):
    valid = np.broadcast_to(np.arange(cl + NEW_KEY_PAD)[None, :] < cl + t, (t, cl + NEW_KEY_PAD))
    return _alibi_tile(n_q_heads, t, cl + NEW_KEY_PAD, cl, valid)[None]


def kernel(x_prompt, x_sample, cache_a_k, cache_a_v, cache_b_k, cache_b_v, norm_mix, norm_ffn,
           norm_final, a_w_qkv, a_w_o, a_rel_bias, b_w_qkv, b_w_o, b_sinks, w_up, w_down):
    batch, seq, d = x_prompt.shape
    dec_batch, dec_seq, _ = x_sample.shape
    depth = norm_mix.shape[0]
    a_kv_width = cache_a_k.shape[3] * cache_a_k.shape[4]
    b_kv_width = cache_b_k.shape[3] * cache_b_k.shape[4]
    cl_a, cl_b = cache_a_k.shape[2], cache_b_k.shape[2]
    n_b_heads = d // B_HEAD_DIM
    tail_a = min(A_PREV_CHUNKS * CHUNK, seq)
    tail_b = min(B_PREV_CHUNKS * CHUNK, seq)

    tm = 512
    tm_qkv = 1024
    tf = 1024
    rows_s = dec_batch * dec_seq

    n_slots = b_w_qkv.shape[0]
    a_w_qkv16 = a_w_qkv.astype(BF16)
    a_w_o16 = a_w_o.astype(BF16)
    b_q16 = _pair_heads(b_w_qkv[:, :, :d].astype(BF16).reshape(n_slots, d, n_b_heads, B_HEAD_DIM), 2)
    b_w_qkv16 = jnp.concatenate(
        [b_q16.reshape(n_slots, d, d), b_w_qkv[:, :, d:].astype(BF16)], axis=2)
    b_w_o16 = _pair_heads(
        b_w_o.astype(BF16).reshape(n_slots, n_b_heads, B_HEAD_DIM, d), 1).reshape(n_slots, d, d)
    w_up16 = w_up.astype(BF16)
    w_down16 = w_down.astype(BF16)
    sinks_perm = _pair_heads(b_sinks, 1) * LOG2E
    a_tables = a_rel_bias * LOG2E
    cache_a_k16 = cache_a_k.astype(BF16).reshape(cache_a_k.shape[:3] + (a_kv_width,))
    cache_a_v16 = cache_a_v.astype(BF16).reshape(cache_a_v.shape[:3] + (a_kv_width,))

    b_bias_prompt = jnp.asarray(_b_prompt_bias(n_b_heads))
    b_bias_sample = jnp.asarray(_b_sample_bias(n_b_heads, dec_seq, cl_b))

    xp = x_prompt.reshape(batch * seq, d)
    xs = x_sample.reshape(rows_s, d)
    gfin = norm_final.reshape(1, d)

    states = {k: [] for k in ("a_kp", "a_vp", "a_ks", "a_vs", "b_kp", "b_vp", "b_ks", "b_vs")}
    for layer in range(depth):
        slot = layer // N_MIXERS
        g_mix = norm_mix[layer].reshape(1, d)
        g_ffn = norm_ffn[layer].reshape(1, d)
        if layer % N_MIXERS == 0:
            qkv_args = dict(tn=1024, q_width=d, kv_width=a_kv_width,
                            q_scale=A_HEAD_DIM ** -0.5 * LOG2E)
            qkv_p, kp, vp = _norm_qkv(xp, g_mix, a_w_qkv16, slot, tm=tm_qkv,
                                      tiles_per_batch=seq // tm_qkv, tail_rows=tail_a, **qkv_args)
            qkv_s, kn, vn = _norm_qkv(xs, g_mix, a_w_qkv16, slot, tm=rows_s,
                                      tiles_per_batch=1, tail_rows=rows_s, **qkv_args)
            table = a_tables[slot]
            mp = _attn_a_prompt(qkv_p, _a_prompt_bias(table), batch=batch, seq=seq, d_model=d)
            ms = _attn_a_sample(
                qkv_s, kn.reshape(dec_batch, dec_seq, a_kv_width),
                vn.reshape(dec_batch, dec_seq, a_kv_width),
                cache_a_k16, cache_a_v16, slot,
                _a_sample_bias(table, dec_seq, cl_a), d_model=d)
            w_o = a_w_o16
            states["a_kp"].append(kp.reshape((batch, tail_a) + cache_a_k.shape[3:]))
            states["a_vp"].append(vp.reshape((batch, tail_a) + cache_a_k.shape[3:]))
            states["a_ks"].append(kn.reshape((dec_batch, dec_seq) + cache_a_k.shape[3:]))
            states["a_vs"].append(vn.reshape((dec_batch, dec_seq) + cache_a_k.shape[3:]))
        else:
            qkv_args = dict(tn=2 * b_kv_width, q_width=d, kv_width=b_kv_width,
                            q_scale=B_HEAD_DIM ** -0.5 * LOG2E)
            qkv_p, kp, vp = _norm_qkv(xp, g_mix, b_w_qkv16, slot, tm=tm_qkv,
                                      tiles_per_batch=seq // tm_qkv, tail_rows=tail_b, **qkv_args)
            qkv_s, kn, vn = _norm_qkv(xs, g_mix, b_w_qkv16, slot, tm=rows_s,
                                      tiles_per_batch=1, tail_rows=rows_s, **qkv_args)
            sinks = sinks_perm[slot].reshape(1, n_b_heads)
            mp = _attn_b_prompt(qkv_p, b_bias_prompt, sinks, batch=batch, seq=seq,
                                d_model=d, kv_width=b_kv_width)
            ms, ks, vs = _attn_b_sample(
                qkv_s, kn.reshape(dec_batch, dec_seq, b_kv_width),
                vn.reshape(dec_batch, dec_seq, b_kv_width),
                cache_b_k[slot].reshape(dec_batch, cl_b, b_kv_width),
                cache_b_v[slot].reshape(dec_batch, cl_b, b_kv_width),
                b_bias_sample, sinks, d_model=d)
            w_o = b_w_o16
            states["b_kp"].append(kp.reshape((batch, tail_b) + cache_b_k.shape[3:]))
            states["b_vp"].append(vp.reshape((batch, tail_b) + cache_b_k.shape[3:]))
            states["b_ks"].append(ks.reshape(cache_b_k.shape[1:]))
            states["b_vs"].append(vs.reshape(cache_b_k.shape[1:]))
        last = layer == depth - 1
        xp = _oproj_mlp(xp, mp, w_o, slot, g_ffn, w_up16, w_down16, layer, gfin,
                        tm=tm, tf=tf, final_norm=last)
        xs = _oproj_mlp(xs, ms, w_o, slot, g_ffn, w_up16, w_down16, layer, gfin,
                        tm=rows_s, tf=tf, final_norm=last)

    def rolled(cache, new_rows):
        return jnp.concatenate([cache, jnp.stack(new_rows)], axis=2)[:, :, dec_seq:]

    return (xp.reshape(batch, seq, d), xs.reshape(dec_batch, dec_seq, d),
            jnp.stack(states["a_kp"]), jnp.stack(states["a_vp"]),
            jnp.stack(states["b_kp"]), jnp.stack(states["b_vp"]),
            rolled(cache_a_k, states["a_ks"]), rolled(cache_a_v, states["a_vs"]),
            jnp.stack(states["b_ks"]), jnp.stack(states["b_vs"]))
```

```python
import functools
from typing import Callable, NamedTuple, Sequence

import jax
import jax.numpy as jnp
import numpy as np
from jax import lax
from jax.experimental import pallas as pl
from jax.experimental.pallas import tpu as pltpu

F32 = jnp.float32
BF16 = jnp.bfloat16

CHUNK = 64
N_MIXERS = 2
A_HEAD_DIM = 128
A_PREV_CHUNKS = 8
A_REL_CLIP = 128
B_PREV_CHUNKS = 2
B_HEAD_DIM = 64
B_KV_HEADS = 8
RMS_EPS = 1e-6
NEG_INF = -1e30
LOG2E = float(np.log2(np.e))

LANES = 128
VMEM_LIMIT_BYTES = 56 * 1024 * 1024

A_TQ = 256
A_HEADS_PER_STEP = 16
B_TQ = 128
B_ROWS_PER_STEP = 512
NEW_KEY_PAD = 128
MLP_WEIGHT_SLOTS = 2


def _rms_scale(x):
    return x * lax.rsqrt(jnp.mean(x * x, axis=-1, keepdims=True) + RMS_EPS)


def _scale_rows(acc, r):
    return jnp.concatenate(
        [acc[:, c * LANES:(c + 1) * LANES] * r for c in range(acc.shape[1] // LANES)], axis=1)


def _staggered_row_specs(tm, width, n_parts, n_m, first_step):
    assert first_step >= 1 and width % n_parts == 0

    def spec(j):
        return pl.BlockSpec(
            (tm, width // n_parts),
            lambda m, s: (jnp.minimum(m + (s >= first_step + j), n_m - 1), j))

    return [spec(j) for j in range(n_parts)]


def _dot_nt(a, b):
    return lax.dot_general(a, b, (((1,), (1,)), ((), ())), preferred_element_type=F32)


def _softmax_pv(s, v):
    m = s.max(axis=-1, keepdims=True)
    p = jnp.exp2(s - m).astype(BF16)
    out = jnp.dot(p, jnp.concatenate([v, jnp.ones_like(v)], axis=1), preferred_element_type=F32)
    width = v.shape[1]
    return out[:, :width] / out[:, width:]


def _norm_qkv_kernel(*refs, n_x, nq, nk, q_scale, tiles_per_batch, tail_rows):
    x_refs = refs[:n_x]
    g_ref, w_ref, qkv_ref, kst_ref, vst_ref, h_ref, r_ref = refs[n_x:]
    m = pl.program_id(0)
    n = pl.program_id(1)

    @pl.when(n == 0)
    def _():
        width = x_refs[0].shape[1]
        sum_sq = None
        for j, x_ref in enumerate(x_refs):
            cols = slice(j * width, (j + 1) * width)
            x = x_ref[...]
            h_ref[:, cols] = (x * g_ref[:, cols]).astype(BF16)
            part = jnp.sum(x * x, axis=-1, keepdims=True)
            sum_sq = part if sum_sq is None else sum_sq + part
        r = lax.rsqrt(sum_sq / h_ref.shape[1] + RMS_EPS)
        r_ref[...] = jnp.broadcast_to(r, r_ref.shape)

    acc = jnp.dot(h_ref[...], w_ref[...], preferred_element_type=F32)
    r = r_ref[...]
    r_out = r * jnp.where(n < nq, q_scale, 1.0)
    tm, tn = acc.shape
    for c in range(tn // LANES):
        cols = slice(c * LANES, (c + 1) * LANES)
        qkv_ref[:, cols] = (acc[:, cols] * r_out).astype(BF16)

    is_tail = (m % tiles_per_batch) == tiles_per_batch - 1

    def tail():
        return _scale_rows(acc[tm - tail_rows:, :], r[tm - tail_rows:, :])

    if nk == 0:
        @pl.when(is_tail & (n == nq))
        def _():
            kv_width = kst_ref.shape[2]
            kst_ref[0] = tail()[:, :kv_width]
            vst_ref[0] = tail()[:, kv_width:]
    else:
        @pl.when(is_tail & (n >= nq) & (n < nq + nk))
        def _():
            kst_ref[0] = tail()

        @pl.when(is_tail & (n >= nq + nk))
        def _():
            vst_ref[0] = tail()


def _norm_qkv(x, g, w, layer, *, tm, tn, q_width, kv_width, q_scale, tiles_per_batch, tail_rows):
    rows, d = x.shape
    n_total = w.shape[2]
    nq = q_width // tn
    nk = kv_width // tn
    assert (nk > 0 and kv_width % tn == 0) or tn == 2 * kv_width
    state_width = min(tn, kv_width)
    nb = rows // (tm * tiles_per_batch)

    def state_map(first):
        def index_map(m, n):
            tail = (m % tiles_per_batch) == tiles_per_batch - 1
            j = jnp.clip(n - first, 0, max(nk - 1, 0))
            return (m // tiles_per_batch, 0, jnp.where(tail, j, 0))
        return index_map

    n_m, n_n = rows // tm, n_total // tn
    n_x = min(4, n_n - 1)
    kernel = functools.partial(_norm_qkv_kernel, n_x=n_x, nq=nq, nk=nk, q_scale=q_scale,
                               tiles_per_batch=tiles_per_batch, tail_rows=tail_rows)
    return pl.pallas_call(
        kernel,
        grid=(n_m, n_n),
        in_specs=_staggered_row_specs(tm, d, n_x, n_m, n_n - n_x) + [
            pl.BlockSpec((1, d), lambda m, n: (0, 0)),
            pl.BlockSpec((None, d, tn), lambda m, n: (layer, 0, n)),
        ],
        out_specs=[
            pl.BlockSpec((tm, tn), lambda m, n: (m, n)),
            pl.BlockSpec((1, tail_rows, state_width), state_map(nq)),
            pl.BlockSpec((1, tail_rows, state_width), state_map(nq + nk)),
        ],
        out_shape=[
            jax.ShapeDtypeStruct((rows, n_total), BF16),
            jax.ShapeDtypeStruct((nb, tail_rows, kv_width), F32),
            jax.ShapeDtypeStruct((nb, tail_rows, kv_width), F32),
        ],
        scratch_shapes=[pltpu.VMEM((tm, d), BF16), pltpu.VMEM((tm, LANES), F32)],
        compiler_params=pltpu.CompilerParams(
            dimension_semantics=("arbitrary", "arbitrary"),
            vmem_limit_bytes=VMEM_LIMIT_BYTES),
        name="norm_qkv",
    )(*([x] * n_x), g, w)


def _attn_a_prompt_kernel(q_ref, k0_ref, k1_ref, k2_ref, v0_ref, v1_ref, v2_ref,
                          bias_ref, o_ref):
    k_refs = (k0_ref, k1_ref, k2_ref)
    v_refs = (v0_ref, v1_ref, v2_ref)
    n_heads = q_ref.shape[1] // A_HEAD_DIM

    def head_slice(h):
        return slice(h * A_HEAD_DIM, (h + 1) * A_HEAD_DIM)

    def scores(h):
        k = jnp.concatenate([r[:, head_slice(h)] for r in k_refs], axis=0)
        return _dot_nt(q_ref[:, head_slice(h)], k) + bias_ref[0, h]

    s_next = scores(0)
    for h in range(n_heads):
        s = s_next
        if h + 1 < n_heads:
            s_next = scores(h + 1)
        v = jnp.concatenate([r[:, head_slice(h)] for r in v_refs], axis=0)
        o_ref[:, head_slice(h)] = _softmax_pv(s, v).astype(BF16)


def _attn_a_prompt(qkv, bias, *, batch, seq, d_model):
    tq = A_TQ
    hw = A_HEADS_PER_STEP * A_HEAD_DIM
    nhb = d_model // hw
    nqb = seq // tq

    def kv_spec(back, col0):
        return pl.BlockSpec(
            (tq, hw), lambda b, h, i: (b * nqb + jnp.maximum(i - back, 0), col0 + h))

    return pl.pallas_call(
        _attn_a_prompt_kernel,
        grid=(batch, nhb, nqb),
        in_specs=[
            pl.BlockSpec((tq, hw), lambda b, h, i: (b * nqb + i, h)),
            kv_spec(2, nhb), kv_spec(1, nhb), kv_spec(0, nhb),
            kv_spec(2, 2 * nhb), kv_spec(1, 2 * nhb), kv_spec(0, 2 * nhb),
            pl.BlockSpec((1, A_HEADS_PER_STEP, tq, 3 * tq),
                         lambda b, h, i: (jnp.minimum(i, 2), h, 0, 0)),
        ],
        out_specs=pl.BlockSpec((tq, hw), lambda b, h, i: (b * nqb + i, h)),
        out_shape=jax.ShapeDtypeStruct((batch * seq, d_model), BF16),
        compiler_params=pltpu.CompilerParams(
            dimension_semantics=("parallel", "parallel", "parallel"),
            vmem_limit_bytes=VMEM_LIMIT_BYTES),
        name="attn_a_prompt",
    )(qkv, qkv, qkv, qkv, qkv, qkv, qkv, bias)


def _attn_a_sample_kernel(q_ref, kc_ref, vc_ref, kn_ref, vn_ref, bias_ref, o_ref):
    t = q_ref.shape[0]
    kn, vn = kn_ref[0], vn_ref[0]
    pad = jnp.zeros((NEW_KEY_PAD - t, A_HEAD_DIM), BF16)
    for h in range(q_ref.shape[1] // A_HEAD_DIM):
        sl = slice(h * A_HEAD_DIM, (h + 1) * A_HEAD_DIM)
        k = jnp.concatenate([kc_ref[0, :, sl], kn[:, sl].astype(BF16), pad], axis=0)
        v = jnp.concatenate([vc_ref[0, :, sl], vn[:, sl].astype(BF16), pad], axis=0)
        s = _dot_nt(q_ref[:, sl], k) + bias_ref[h]
        o_ref[:, sl] = _softmax_pv(s, v).astype(BF16)


def _attn_a_sample(qkv, k_new, v_new, cache_k, cache_v, slot, bias, *, d_model):
    _, nb, cl, _ = cache_k.shape
    t = k_new.shape[1]
    hw = A_HEADS_PER_STEP * A_HEAD_DIM
    nhb = d_model // hw
    cache_spec = pl.BlockSpec((None, 1, cl, hw), lambda b, h: (slot, b, 0, h))
    new_spec = pl.BlockSpec((1, t, hw), lambda b, h: (b, 0, h))
    return pl.pallas_call(
        _attn_a_sample_kernel,
        grid=(nb, nhb),
        in_specs=[
            pl.BlockSpec((t, hw), lambda b, h: (b, h)),
            cache_spec, cache_spec, new_spec, new_spec,
            pl.BlockSpec((A_HEADS_PER_STEP, t, cl + NEW_KEY_PAD), lambda b, h: (h, 0, 0)),
        ],
        out_specs=pl.BlockSpec((t, hw), lambda b, h: (b, h)),
        out_shape=jax.ShapeDtypeStruct((nb * t, d_model), BF16),
        compiler_params=pltpu.CompilerParams(
            dimension_semantics=("parallel", "parallel"),
            vmem_limit_bytes=VMEM_LIMIT_BYTES),
        name="attn_a_sample",
    )(qkv, cache_k, cache_v, k_new, v_new, bias)


class _QueryBlock(NamedTuple):
    q: Callable
    k_blocks: Sequence
    v_blocks: Sequence
    bias: Callable
    store: Callable


def _attn_b_heads(query_blocks, sink_ref, n_col_blocks):
    n_pairs = B_KV_HEADS // 2
    bpp = n_col_blocks // n_pairs
    items = [(qb, pair) for qb in query_blocks for pair in range(n_pairs)]

    def scores(qb, pair):
        pair_sl = slice(pair * LANES, (pair + 1) * LANES)
        k = jnp.concatenate([kb(pair_sl) for kb in qb.k_blocks], axis=0)
        low_k = lax.broadcasted_iota(jnp.int32, k.shape, 1) < B_HEAD_DIM
        zero = jnp.zeros_like(k)
        k_diag = jnp.concatenate([jnp.where(low_k, k, zero), jnp.where(low_k, zero, k)], axis=0)
        q_stack = jnp.concatenate([qb.q(pair * bpp + c) for c in range(bpp)], axis=0)
        return _dot_nt(q_stack, k_diag) + qb.bias(pair)

    def softmax_pv(qb, pair, s):
        pair_sl = slice(pair * LANES, (pair + 1) * LANES)
        v = jnp.concatenate([vb(pair_sl) for vb in qb.v_blocks], axis=0)
        keys = v.shape[0]
        m_even = s[:, :keys].max(axis=-1, keepdims=True)
        m_odd = s[:, keys:].max(axis=-1, keepdims=True)
        p = jnp.concatenate(
            [jnp.exp2(s[:, :keys] - m_even), jnp.exp2(s[:, keys:] - m_odd)], axis=1).astype(BF16)
        low_k = lax.broadcasted_iota(jnp.int32, v.shape, 1) < B_HEAD_DIM
        zero = jnp.zeros_like(v)
        ones_low = jnp.where(low_k, 1.0, 0.0).astype(BF16)
        ones_high = jnp.where(low_k, 0.0, 1.0).astype(BF16)
        v_diag = jnp.concatenate(
            [jnp.concatenate([jnp.where(low_k, v, zero), ones_low], axis=1),
             jnp.concatenate([jnp.where(low_k, zero, v), ones_high], axis=1)], axis=0)
        return m_even, m_odd, jnp.dot(p, v_diag, preferred_element_type=F32)

    def finish(qb, pair, m_even, m_odd, out):
        tq = out.shape[0] // bpp
        low = lax.broadcasted_iota(jnp.int32, (tq, LANES), 1) < B_HEAD_DIM
        for c in range(bpp):
            rows = slice(c * tq, (c + 1) * tq)
            cb = pair * bpp + c
            sink_even = jnp.exp2(sink_ref[0, 2 * cb] - m_even[rows])
            sink_odd = jnp.exp2(sink_ref[0, 2 * cb + 1] - m_odd[rows])
            den = out[rows, LANES:] + jnp.where(low, sink_even, sink_odd)
            qb.store(cb, (out[rows, :LANES] / den).astype(BF16))

    s_vals, pv_vals = {}, {}
    for t in range(len(items) + 2):
        if t < len(items):
            s_vals[t] = scores(*items[t])
        if 0 <= t - 1 < len(items):
            pv_vals[t - 1] = softmax_pv(*items[t - 1], s_vals.pop(t - 1))
        if 0 <= t - 2 < len(items):
            finish(*items[t - 2], *pv_vals.pop(t - 2))


def _attn_b_prompt_kernel(q_ref, kp_ref, kc_ref, vp_ref, vc_ref, bias_first_ref, bias_ref,
                          sink_ref, o_ref):
    n_sub = q_ref.shape[0] // B_TQ

    def query_block(j):
        rows = slice(j * B_TQ, (j + 1) * B_TQ)
        prev = slice((j - 1) * B_TQ, j * B_TQ)

        def store(cb, value):
            o_ref[rows, cb * LANES:(cb + 1) * LANES] = value

        if j == 0:
            k_blocks = [lambda sl: kp_ref[:, sl], lambda sl: kc_ref[rows, sl]]
            v_blocks = [lambda sl: vp_ref[:, sl], lambda sl: vc_ref[rows, sl]]
            bias = bias_first_ref
        else:
            k_blocks = [lambda sl: kc_ref[prev, sl], lambda sl: kc_ref[rows, sl]]
            v_blocks = [lambda sl: vc_ref[prev, sl], lambda sl: vc_ref[rows, sl]]
            bias = bias_ref
        return _QueryBlock(
            q=lambda cb: q_ref[rows, cb * LANES:(cb + 1) * LANES],
            k_blocks=k_blocks, v_blocks=v_blocks,
            bias=lambda pair: bias[0, pair], store=store)

    _attn_b_heads([query_block(j) for j in range(n_sub)], sink_ref, q_ref.shape[1] // LANES)


def _attn_b_prompt(qkv, bias, sinks, *, batch, seq, d_model, kv_width):
    rows = B_ROWS_PER_STEP
    n_steps = seq // rows
    sub_per_step = rows // B_TQ
    nq_cols = d_model // kv_width

    def cur_spec(col):
        return pl.BlockSpec((rows, kv_width), lambda b, i: (b * n_steps + i, col))

    def prev_spec(col):
        return pl.BlockSpec(
            (B_TQ, kv_width),
            lambda b, i: ((b * n_steps + i) * sub_per_step - jnp.where(i == 0, 0, 1), col))

    bias_block = (1,) + bias.shape[1:]
    return pl.pallas_call(
        _attn_b_prompt_kernel,
        grid=(batch, n_steps),
        in_specs=[
            pl.BlockSpec((rows, d_model), lambda b, i: (b * n_steps + i, 0)),
            prev_spec(nq_cols), cur_spec(nq_cols),
            prev_spec(nq_cols + 1), cur_spec(nq_cols + 1),
            pl.BlockSpec(bias_block, lambda b, i: (jnp.where(i == 0, 1, 0), 0, 0, 0)),
            pl.BlockSpec(bias_block, lambda b, i: (0, 0, 0, 0)),
            pl.BlockSpec(memory_space=pltpu.SMEM),
        ],
        out_specs=pl.BlockSpec((rows, d_model), lambda b, i: (b * n_steps + i, 0)),
        out_shape=jax.ShapeDtypeStruct((batch * seq, d_model), BF16),
        compiler_params=pltpu.CompilerParams(
            dimension_semantics=("parallel", "parallel"),
            vmem_limit_bytes=VMEM_LIMIT_BYTES),
        name="attn_b_prompt",
    )(qkv, qkv, qkv, qkv, qkv, bias, bias, sinks)


def _attn_b_sample_kernel(q_ref, kc_ref, vc_ref, kn_ref, vn_ref, bias_ref, sink_ref,
                          o_ref, ko_ref, vo_ref):
    t = q_ref.shape[0]
    cl = kc_ref.shape[1]
    kc, vc, kn, vn = kc_ref[0], vc_ref[0], kn_ref[0], vn_ref[0]
    ko_ref[0, :cl - t, :] = kc[t:, :]
    ko_ref[0, cl - t:, :] = kn
    vo_ref[0, :cl - t, :] = vc[t:, :]
    vo_ref[0, cl - t:, :] = vn
    pad = jnp.zeros((NEW_KEY_PAD - t, LANES), BF16)

    def store(cb, value):
        o_ref[:, cb * LANES:(cb + 1) * LANES] = value

    block = _QueryBlock(
        q=lambda cb: q_ref[:, cb * LANES:(cb + 1) * LANES],
        k_blocks=[lambda sl: kc[:, sl].astype(BF16),
                  lambda sl: jnp.concatenate([kn[:, sl].astype(BF16), pad], axis=0)],
        v_blocks=[lambda sl: vc[:, sl].astype(BF16),
                  lambda sl: jnp.concatenate([vn[:, sl].astype(BF16), pad], axis=0)],
        bias=lambda pair: bias_ref[0, pair], store=store)
    _attn_b_heads([block], sink_ref, q_ref.shape[1] // LANES)


def _attn_b_sample(qkv, k_new, v_new, cache_k, cache_v, bias, sinks, *, d_model):
    nb, cl, kv_width = cache_k.shape
    t = k_new.shape[1]
    cache_spec = pl.BlockSpec((1, cl, kv_width), lambda b: (b, 0, 0))
    new_spec = pl.BlockSpec((1, t, kv_width), lambda b: (b, 0, 0))
    return pl.pallas_call(
        _attn_b_sample_kernel,
        grid=(nb,),
        in_specs=[
            pl.BlockSpec((t, d_model), lambda b: (b, 0)),
            cache_spec, cache_spec, new_spec, new_spec,
            pl.BlockSpec(bias.shape, lambda b: (0, 0, 0, 0)),
            pl.BlockSpec(memory_space=pltpu.SMEM),
        ],
        out_specs=[pl.BlockSpec((t, d_model), lambda b: (b, 0)), cache_spec, cache_spec],
        out_shape=[
            jax.ShapeDtypeStruct((nb * t, d_model), BF16),
            jax.ShapeDtypeStruct(cache_k.shape, F32),
            jax.ShapeDtypeStruct(cache_v.shape, F32),
        ],
        compiler_params=pltpu.CompilerParams(
            dimension_semantics=("parallel",),
            vmem_limit_bytes=VMEM_LIMIT_BYTES),
        name="attn_b_sample",
    )(qkv, cache_k, cache_v, k_new, v_new, bias, sinks)


def _oproj_mlp_kernel(x_ref, a_ref, wo_ref, g_ref, wup_hbm, wdn_hbm, gfin_ref, o_ref,
                      h_ref, r_ref, wup_buf, wdn_buf, sem, *, layer, tf, final_norm):
    m = pl.program_id(0)
    n_f = wup_hbm.shape[2] // tf

    def weight_copies(f, slot):
        return (
            pltpu.make_async_copy(wup_hbm.at[layer, :, pl.ds(f * tf, tf)], wup_buf.at[slot],
                                  sem.at[0, slot]),
            pltpu.make_async_copy(wdn_hbm.at[layer, pl.ds(f * tf, tf), :], wdn_buf.at[slot],
                                  sem.at[1, slot]))

    def start(f, slot):
        for copy in weight_copies(f, slot):
            copy.start()

    def wait(f, slot):
        for copy in weight_copies(f, slot):
            copy.wait()

    n_slots = wup_buf.shape[0]
    ahead = n_slots - 1

    @pl.when(m == 0)
    def _():
        for f in range(ahead):
            start(f, f)

    x1 = x_ref[...] + jnp.dot(a_ref[...], wo_ref[...], preferred_element_type=F32)
    o_ref[...] = x1
    h_ref[...] = (x1 * g_ref[...]).astype(BF16)
    r = lax.rsqrt(jnp.mean(x1 * x1, axis=-1, keepdims=True) + RMS_EPS)
    r_ref[...] = jnp.broadcast_to(r, r_ref.shape)

    for f in range(n_f):
        slot = f % n_slots
        nxt = f + ahead
        if nxt < n_f:
            start(nxt, nxt % n_slots)
        else:
            @pl.when(m + 1 < pl.num_programs(0))
            def _():
                start(nxt - n_f, nxt - n_f)
        wait(f, slot)
        u = _scale_rows(jnp.dot(h_ref[...], wup_buf[slot], preferred_element_type=F32), r_ref[...])
        act = jnp.square(jnp.maximum(u, 0.0)).astype(BF16)
        o_ref[...] += jnp.dot(act, wdn_buf[slot], preferred_element_type=F32)

    if final_norm:
        o_ref[...] = _rms_scale(o_ref[...]) * gfin_ref[...]


def _oproj_mlp(x, attn, wo, slot, g, wup, wdn, layer, gfin, *, tm, tf, final_norm):
    rows, d = x.shape
    d_ff = wup.shape[2]
    assert d_ff % tf == 0 and (d_ff // tf) % MLP_WEIGHT_SLOTS == 0
    kernel = functools.partial(_oproj_mlp_kernel, layer=layer, tf=tf, final_norm=final_norm)
    return pl.pallas_call(
        kernel,
        grid=(rows // tm,),
        in_specs=[
            pl.BlockSpec((tm, d), lambda m: (m, 0)),
            pl.BlockSpec((tm, d), lambda m: (m, 0)),
            pl.BlockSpec((None, d, d), lambda m: (slot, 0, 0), pipeline_mode=pl.Buffered(1)),
            pl.BlockSpec((1, d), lambda m: (0, 0)),
            pl.BlockSpec(memory_space=pl.ANY),
            pl.BlockSpec(memory_space=pl.ANY),
            pl.BlockSpec((1, d), lambda m: (0, 0)),
        ],
        out_specs=pl.BlockSpec((tm, d), lambda m: (m, 0)),
        out_shape=jax.ShapeDtypeStruct((rows, d), F32),
        scratch_shapes=[
            pltpu.VMEM((tm, d), BF16), pltpu.VMEM((tm, LANES), F32),
            pltpu.VMEM((MLP_WEIGHT_SLOTS, d, tf), BF16), pltpu.VMEM((MLP_WEIGHT_SLOTS, tf, d), BF16),
            pltpu.SemaphoreType.DMA((2, MLP_WEIGHT_SLOTS)),
        ],
        compiler_params=pltpu.CompilerParams(
            dimension_semantics=("arbitrary",),
            vmem_limit_bytes=VMEM_LIMIT_BYTES),
        name="oproj_mlp",
    )(x, attn, wo, g, wup, wdn, gfin)


def _band_mask(n_q_chunks, n_k_chunks, n_prev):
    ci = np.arange(n_q_chunks * CHUNK)[:, None] // CHUNK
    cj = np.arange(n_k_chunks * CHUNK)[None, :] // CHUNK
    return (cj >= ci) & (cj <= ci + n_prev)


def _rel_bias_tile(table, n_rows, n_cols, key_offset):
    length = n_rows + n_cols
    rel = np.arange(length) - (n_rows - 1) - key_offset
    diag = table[:, np.clip(rel, -A_REL_CLIP, A_REL_CLIP) + A_REL_CLIP]
    flat = jnp.tile(diag, (1, n_rows))[:, :n_rows * (length - 1)]
    return flat.reshape(table.shape[0], n_rows, length - 1)[:, :, n_rows - 1:n_rows - 1 + n_cols]


def _a_prompt_bias(table):
    bias = _rel_bias_tile(table, A_TQ, 3 * A_TQ, 2 * A_TQ)
    band = _band_mask(A_TQ // CHUNK, 3 * A_TQ // CHUNK, A_PREV_CHUNKS)
    col = np.arange(3 * A_TQ)[None, :]
    masks = np.stack([band & (col >= (2 - v) * A_TQ) for v in range(3)])
    return jnp.where(masks[:, None], bias[None], NEG_INF)


def _a_sample_bias(table, t, cl):
    bias = _rel_bias_tile(table, t, cl + NEW_KEY_PAD, cl)
    valid = np.broadcast_to(np.arange(cl + NEW_KEY_PAD)[None, :] < cl + t, (t, cl + NEW_KEY_PAD))
    return jnp.where(valid[None], bias, NEG_INF)


def _pair_heads(x, axis):
    n_heads = x.shape[axis]
    group = n_heads // B_KV_HEADS
    split = x.shape[:axis] + (B_KV_HEADS // 2, 2, group) + x.shape[axis + 1:]
    return jnp.swapaxes(x.reshape(split), axis + 1, axis + 2).reshape(x.shape)


def _alibi_tile(n_q_heads, n_rows, n_cols, key_offset, valid):
    group = n_q_heads // B_KV_HEADS
    slopes = (2.0 ** (-8.0 * np.arange(1, n_q_heads + 1) / n_q_heads)).astype(np.float32)
    qi = np.arange(n_rows)[:, None]
    kj = np.arange(n_cols)[None, :]
    rel = np.abs(kj - key_offset - qi).astype(np.float32)
    alibi = (-slopes[:, None, None] * rel[None]).astype(np.float64)
    bias = np.where(valid[None], alibi * LOG2E, NEG_INF)
    stacked = [[np.concatenate([bias[group * (2 * pair) + g], bias[group * (2 * pair + 1) + g]], axis=1)
                for g in range(group)] for pair in range(B_KV_HEADS // 2)]
    return np.asarray(stacked, np.float32).reshape(B_KV_HEADS // 2, group * n_rows, 2 * n_cols)


def _b_prompt_bias(n_q_heads):
    mask = _band_mask(B_TQ // CHUNK, 2 * B_TQ // CHUNK, B_PREV_CHUNKS)
    first = mask & (np.arange(2 * B_TQ)[None, :] >= B_TQ)
    return np.stack([_alibi_tile(n_q_heads, B_TQ, 2 * B_TQ, B_TQ, mask),
                     _alibi_tile(n_q_heads, B_TQ, 2 * B_TQ, B_TQ, first)])


def _b_sample_bias(n_q_heads, t, cl):
    valid = np.broadcast_to(np.arange(cl + NEW_KEY_PAD)[None, :] < cl + t, (t, cl + NEW_KEY_PAD))
    return _alibi_tile(n_q_heads, t, cl + NEW_KEY_PAD, cl, valid)[None]


def kernel(x_prompt, x_sample, cache_a_k, cache_a_v, cache_b_k, cache_b_v, norm_mix, norm_ffn,
           norm_final, a_w_qkv, a_w_o, a_rel_bias, b_w_qkv, b_w_o, b_sinks, w_up, w_down):
    batch, seq, d = x_prompt.shape
    dec_batch, dec_seq, _ = x_sample.shape
    depth = norm_mix.shape[0]
    a_kv_width = cache_a_k.shape[3] * cache_a_k.shape[4]
    b_kv_width = cache_b_k.shape[3] * cache_b_k.shape[4]
    cl_a, cl_b = cache_a_k.shape[2], cache_b_k.shape[2]
    n_b_heads = d // B_HEAD_DIM
    tail_a = min(A_PREV_CHUNKS * CHUNK, seq)
    tail_b = min(B_PREV_CHUNKS * CHUNK, seq)

    tm = 512
    tm_qkv = 1024
    tf = 1024
    rows_s = dec_batch * dec_seq

    n_slots = b_w_qkv.shape[0]
    a_w_qkv16 = a_w_qkv.astype(BF16)
    a_w_o16 = a_w_o.astype(BF16)
    b_q16 = _pair_heads(b_w_qkv[:, :, :d].astype(BF16).reshape(n_slots, d, n_b_heads, B_HEAD_DIM), 2)
    b_w_qkv16 = jnp.concatenate(
        [b_q16.reshape(n_slots, d, d), b_w_qkv[:, :, d:].astype(BF16)], axis=2)
    b_w_o16 = _pair_heads(
        b_w_o.astype(BF16).reshape(n_slots, n_b_heads, B_HEAD_DIM, d), 1).reshape(n_slots, d, d)
    w_up16 = w_up.astype(BF16)
    w_down16 = w_down.astype(BF16)
    sinks_perm = _pair_heads(b_sinks, 1) * LOG2E
    a_tables = a_rel_bias * LOG2E
    cache_a_k16 = cache_a_k.astype(BF16).reshape(cache_a_k.shape[:3] + (a_kv_width,))
    cache_a_v16 = cache_a_v.astype(BF16).reshape(cache_a_v.shape[:3] + (a_kv_width,))

    b_bias_prompt = jnp.asarray(_b_prompt_bias(n_b_heads))
    b_bias_sample = jnp.asarray(_b_sample_bias(n_b_heads, dec_seq, cl_b))

    xp = x_prompt.reshape(batch * seq, d)
    xs = x_sample.reshape(rows_s, d)
    gfin = norm_final.reshape(1, d)

    states = {k: [] for k in ("a_kp", "a_vp", "a_ks", "a_vs", "b_kp", "b_vp", "b_ks", "b_vs")}
    for layer in range(depth):
        slot = layer // N_MIXERS
        g_mix = norm_mix[layer].reshape(1, d)
        g_ffn = norm_ffn[layer].reshape(1, d)
        if layer % N_MIXERS == 0:
            qkv_args = dict(tn=1024, q_width=d, kv_width=a_kv_width,
                            q_scale=A_HEAD_DIM ** -0.5 * LOG2E)
            qkv_p, kp, vp = _norm_qkv(xp, g_mix, a_w_qkv16, slot, tm=tm_qkv,
                                      tiles_per_batch=seq // tm_qkv, tail_rows=tail_a, **qkv_args)
            qkv_s, kn, vn = _norm_qkv(xs, g_mix, a_w_qkv16, slot, tm=rows_s,
                                      tiles_per_batch=1, tail_rows=rows_s, **qkv_args)
            table = a_tables[slot]
            mp = _attn_a_prompt(qkv_p, _a_prompt_bias(table), batch=batch, seq=seq, d_model=d)
            ms = _attn_a_sample(
                qkv_s, kn.reshape(dec_batch, dec_seq, a_kv_width),
                vn.reshape(dec_batch, dec_seq, a_kv_width),
                cache_a_k16, cache_a_v16, slot,
                _a_sample_bias(table, dec_seq, cl_a), d_model=d)
            w_o = a_w_o16
            states["a_kp"].append(kp.reshape((batch, tail_a) + cache_a_k.shape[3:]))
            states["a_vp"].append(vp.reshape((batch, tail_a) + cache_a_k.shape[3:]))
            states["a_ks"].append(kn.reshape((dec_batch, dec_seq) + cache_a_k.shape[3:]))
            states["a_vs"].append(vn.reshape((dec_batch, dec_seq) + cache_a_k.shape[3:]))
        else:
            qkv_args = dict(tn=2 * b_kv_width, q_width=d, kv_width=b_kv_width,
                            q_scale=B_HEAD_DIM ** -0.5 * LOG2E)
            qkv_p, kp, vp = _norm_qkv(xp, g_mix, b_w_qkv16, slot, tm=tm_qkv,
                                      tiles_per_batch=seq // tm_qkv, tail_rows=tail_b, **qkv_args)
            qkv_s, kn, vn = _norm_qkv(xs, g_mix, b_w_qkv16, slot, tm=rows_s,
                                      tiles_per_batch=1, tail_rows=rows_s, **qkv_args)
            sinks = sinks_perm[slot].reshape(1, n_b_heads)
            mp = _attn_b_prompt(qkv_p, b_bias_prompt, sinks, batch=batch, seq=seq,
                                d_model=d, kv_width=b_kv_width)
            ms, ks, vs = _attn_b_sample(
                qkv_s, kn.reshape(dec_batch, dec_seq, b_kv_width),
                vn.reshape(dec_batch, dec_seq, b_kv_width),
                cache_b_k[slot].reshape(dec_batch, cl_b, b_kv_width),
                cache_b_v[slot].reshape(dec_batch, cl_b, b_kv_width),
                b_bias_sample, sinks, d_model=d)
            w_o = b_w_o16
            states["b_kp"].append(kp.reshape((batch, tail_b) + cache_b_k.shape[3:]))
            states["b_vp"].append(vp.reshape((batch, tail_b) + cache_b_k.shape[3:]))
            states["b_ks"].append(ks.reshape(cache_b_k.shape[1:]))
            states["b_vs"].append(vs.reshape(cache_b_k.shape[1:]))
        last = layer == depth - 1
        xp = _oproj_mlp(xp, mp, w_o, slot, g_ffn, w_up16, w_down16, layer, gfin,
                        tm=tm, tf=tf, final_norm=last)
        xs = _oproj_mlp(xs, ms, w_o, slot, g_ffn, w_up16, w_down16, layer, gfin,
                        tm=rows_s, tf=tf, final_norm=last)

    def rolled(cache, new_rows):
        return jnp.concatenate([cache, jnp.stack(new_rows)], axis=2)[:, :, dec_seq:]

    return (xp.reshape(batch, seq, d), xs.reshape(dec_batch, dec_seq, d),
            jnp.stack(states["a_kp"]), jnp.stack(states["a_vp"]),
            jnp.stack(states["b_kp"]), jnp.stack(states["b_vp"]),
            rolled(cache_a_k, states["a_ks"]), rolled(cache_a_v, states["a_vs"]),
            jnp.stack(states["b_ks"]), jnp.stack(states["b_vs"]))
```

```python
import functools
from typing import Callable, NamedTuple, Sequence

import jax
import jax.numpy as jnp
import numpy as np
from jax import lax
from jax.experimental import pallas as pl
from jax.experimental.pallas import tpu as pltpu

F32 = jnp.float32
BF16 = jnp.bfloat16

CHUNK = 64
N_MIXERS = 2
A_HEAD_DIM = 128
A_PREV_CHUNKS = 8
A_REL_CLIP = 128
B_PREV_CHUNKS = 2
B_HEAD_DIM = 64
B_KV_HEADS = 8
RMS_EPS = 1e-6
NEG_INF = -1e30
LOG2E = float(np.log2(np.e))

LANES = 128
VMEM_LIMIT_BYTES = 56 * 1024 * 1024

A_TQ = 256
A_HEADS_PER_STEP = 16
B_TQ = 128
B_ROWS_PER_STEP = 1024
NEW_KEY_PAD = 128
MLP_WEIGHT_SLOTS = 2


def _rms_scale(x):
    return x * lax.rsqrt(jnp.mean(x * x, axis=-1, keepdims=True) + RMS_EPS)


def _scale_rows(acc, r):
    return jnp.concatenate(
        [acc[:, c * LANES:(c + 1) * LANES] * r for c in range(acc.shape[1] // LANES)], axis=1)


def _staggered_row_specs(tm, width, n_parts, n_m, first_step):
    assert first_step >= 1 and width % n_parts == 0

    def spec(j):
        return pl.BlockSpec(
            (tm, width // n_parts),
            lambda m, s: (jnp.minimum(m + (s >= first_step + j), n_m - 1), j))

    return [spec(j) for j in range(n_parts)]


def _dot_nt(a, b):
    return lax.dot_general(a, b, (((1,), (1,)), ((), ())), preferred_element_type=F32)


def _softmax_pv(s, v):
    m = s.max(axis=-1, keepdims=True)
    p = jnp.exp2(s - m).astype(BF16)
    out = jnp.dot(p, jnp.concatenate([v, jnp.ones_like(v)], axis=1), preferred_element_type=F32)
    width = v.shape[1]
    return out[:, :width] / out[:, width:]


def _norm_qkv_kernel(*refs, n_x, nq, nk, q_scale, tiles_per_batch, tail_rows):
    x_refs = refs[:n_x]
    g_ref, w_ref, qkv_ref, kst_ref, vst_ref, h_ref, r_ref = refs[n_x:]
    m = pl.program_id(0)
    n = pl.program_id(1)

    @pl.when(n == 0)
    def _():
        width = x_refs[0].shape[1]
        sum_sq = None
        for j, x_ref in enumerate(x_refs):
            cols = slice(j * width, (j + 1) * width)
            x = x_ref[...]
            h_ref[:, cols] = (x * g_ref[:, cols]).astype(BF16)
            part = jnp.sum(x * x, axis=-1, keepdims=True)
            sum_sq = part if sum_sq is None else sum_sq + part
        r = lax.rsqrt(sum_sq / h_ref.shape[1] + RMS_EPS)
        r_ref[...] = jnp.broadcast_to(r, r_ref.shape)

    acc = jnp.dot(h_ref[...], w_ref[...], preferred_element_type=F32)
    r = r_ref[...]
    r_out = r * jnp.where(n < nq, q_scale, 1.0)
    tm, tn = acc.shape
    for c in range(tn // LANES):
        cols = slice(c * LANES, (c + 1) * LANES)
        qkv_ref[:, cols] = (acc[:, cols] * r_out).astype(BF16)

    is_tail = (m % tiles_per_batch) == tiles_per_batch - 1

    def tail():
        return _scale_rows(acc[tm - tail_rows:, :], r[tm - tail_rows:, :])

    if nk == 0:
        @pl.when(is_tail & (n == nq))
        def _():
            kv_width = kst_ref.shape[2]
            kst_ref[0] = tail()[:, :kv_width]
            vst_ref[0] = tail()[:, kv_width:]
    else:
        @pl.when(is_tail & (n >= nq) & (n < nq + nk))
        def _():
            kst_ref[0] = tail()

        @pl.when(is_tail & (n >= nq + nk))
        def _():
            vst_ref[0] = tail()


def _norm_qkv(x, g, w, layer, *, tm, tn, q_width, kv_width, q_scale, tiles_per_batch, tail_rows):
    rows, d = x.shape
    n_total = w.shape[2]
    nq = q_width // tn
    nk = kv_width // tn
    assert (nk > 0 and kv_width % tn == 0) or tn == 2 * kv_width
    state_width = min(tn, kv_width)
    nb = rows // (tm * tiles_per_batch)

    def state_map(first):
        def index_map(m, n):
            tail = (m % tiles_per_batch) == tiles_per_batch - 1
            j = jnp.clip(n - first, 0, max(nk - 1, 0))
            return (m // tiles_per_batch, 0, jnp.where(tail, j, 0))
        return index_map

    n_m, n_n = rows // tm, n_total // tn
    n_x = min(4, n_n - 1)
    kernel = functools.partial(_norm_qkv_kernel, n_x=n_x, nq=nq, nk=nk, q_scale=q_scale,
                               tiles_per_batch=tiles_per_batch, tail_rows=tail_rows)
    return pl.pallas_call(
        kernel,
        grid=(n_m, n_n),
        in_specs=_staggered_row_specs(tm, d, n_x, n_m, n_n - n_x) + [
            pl.BlockSpec((1, d), lambda m, n: (0, 0)),
            pl.BlockSpec((None, d, tn), lambda m, n: (layer, 0, n)),
        ],
        out_specs=[
            pl.BlockSpec((tm, tn), lambda m, n: (m, n)),
            pl.BlockSpec((1, tail_rows, state_width), state_map(nq)),
            pl.BlockSpec((1, tail_rows, state_width), state_map(nq + nk)),
        ],
        out_shape=[
            jax.ShapeDtypeStruct((rows, n_total), BF16),
            jax.ShapeDtypeStruct((nb, tail_rows, kv_width), F32),
            jax.ShapeDtypeStruct((nb, tail_rows, kv_width), F32),
        ],
        scratch_shapes=[pltpu.VMEM((tm, d), BF16), pltpu.VMEM((tm, LANES), F32)],
        compiler_params=pltpu.CompilerParams(
            dimension_semantics=("arbitrary", "arbitrary"),
            vmem_limit_bytes=VMEM_LIMIT_BYTES),
        name="norm_qkv",
    )(*([x] * n_x), g, w)


def _attn_a_prompt_kernel(q_ref, k0_ref, k1_ref, k2_ref, v0_ref, v1_ref, v2_ref,
                          diag_ref, band_ref, o_ref, bias_ref):
    k_refs = (k0_ref, k1_ref, k2_ref)
    v_refs = (v0_ref, v1_ref, v2_ref)
    n_heads = q_ref.shape[1] // A_HEAD_DIM
    tq = q_ref.shape[0]
    n_keys = len(k_refs) * tq
    b, i = pl.program_id(1), pl.program_id(2)

    @pl.when((b == 0) & (i == 0))
    def _():
        for h in range(n_heads):
            rows = jnp.broadcast_to(diag_ref[h:h + 1, :], (tq, diag_ref.shape[1]))
            toeplitz = pltpu.roll(rows, 0, 1, stride=1, stride_axis=0)
            bias_ref[h] = toeplitz[:, :n_keys] + band_ref[...]

    def head_slice(h):
        return slice(h * A_HEAD_DIM, (h + 1) * A_HEAD_DIM)

    def attend(start_mask):
        def scores(h):
            k = jnp.concatenate([r[:, head_slice(h)] for r in k_refs], axis=0)
            s = _dot_nt(q_ref[:, head_slice(h)], k) + bias_ref[h]
            return s if start_mask is None else s + start_mask

        s_next = scores(0)
        for h in range(n_heads):
            s = s_next
            if h + 1 < n_heads:
                s_next = scores(h + 1)
            v = jnp.concatenate([r[:, head_slice(h)] for r in v_refs], axis=0)
            o_ref[:, head_slice(h)] = _softmax_pv(s, v).astype(BF16)

    n_before = len(k_refs) - 1

    @pl.when(i >= n_before)
    def _():
        attend(None)

    @pl.when(i < n_before)
    def _():
        col = lax.broadcasted_iota(jnp.int32, (1, n_keys), 1)
        attend(jnp.where(col < (n_before - i) * tq, NEG_INF, 0.0))


def _attn_a_prompt(qkv, diag, *, batch, seq, d_model):
    tq = A_TQ
    hw = A_HEADS_PER_STEP * A_HEAD_DIM
    nhb = d_model // hw
    nqb = seq // tq
    band = np.where(_band_mask(tq // CHUNK, 3 * tq // CHUNK, A_PREV_CHUNKS), 0.0, NEG_INF)

    def kv_spec(back, col0):
        return pl.BlockSpec(
            (tq, hw), lambda h, b, i: (b * nqb + jnp.maximum(i - back, 0), col0 + h))

    return pl.pallas_call(
        _attn_a_prompt_kernel,
        grid=(nhb, batch, nqb),
        in_specs=[
            pl.BlockSpec((tq, hw), lambda h, b, i: (b * nqb + i, h)),
            kv_spec(2, nhb), kv_spec(1, nhb), kv_spec(0, nhb),
            kv_spec(2, 2 * nhb), kv_spec(1, 2 * nhb), kv_spec(0, 2 * nhb),
            pl.BlockSpec((A_HEADS_PER_STEP, diag.shape[1]), lambda h, b, i: (h, 0)),
            pl.BlockSpec(band.shape, lambda h, b, i: (0, 0)),
        ],
        out_specs=pl.BlockSpec((tq, hw), lambda h, b, i: (b * nqb + i, h)),
        out_shape=jax.ShapeDtypeStruct((batch * seq, d_model), BF16),
        scratch_shapes=[pltpu.VMEM((A_HEADS_PER_STEP, tq, 3 * tq), F32)],
        compiler_params=pltpu.CompilerParams(
            dimension_semantics=("arbitrary", "arbitrary", "arbitrary"),
            vmem_limit_bytes=VMEM_LIMIT_BYTES),
        name="attn_a_prompt",
    )(qkv, qkv, qkv, qkv, qkv, qkv, qkv, diag, jnp.asarray(band, F32))


def _attn_a_sample_kernel(q_ref, kc_ref, vc_ref, kn_ref, vn_ref, bias_ref, o_ref):
    t = q_ref.shape[0]
    kn, vn = kn_ref[0], vn_ref[0]
    pad = jnp.zeros((NEW_KEY_PAD - t, A_HEAD_DIM), BF16)
    for h in range(q_ref.shape[1] // A_HEAD_DIM):
        sl = slice(h * A_HEAD_DIM, (h + 1) * A_HEAD_DIM)
        k = jnp.concatenate([kc_ref[0, :, sl], kn[:, sl].astype(BF16), pad], axis=0)
        v = jnp.concatenate([vc_ref[0, :, sl], vn[:, sl].astype(BF16), pad], axis=0)
        s = _dot_nt(q_ref[:, sl], k) + bias_ref[h]
        o_ref[:, sl] = _softmax_pv(s, v).astype(BF16)


def _attn_a_sample(qkv, k_new, v_new, cache_k, cache_v, slot, bias, *, d_model):
    _, nb, cl, _ = cache_k.shape
    t = k_new.shape[1]
    hw = A_HEADS_PER_STEP * A_HEAD_DIM
    nhb = d_model // hw
    cache_spec = pl.BlockSpec((None, 1, cl, hw), lambda b, h: (slot, b, 0, h))
    new_spec = pl.BlockSpec((1, t, hw), lambda b, h: (b, 0, h))
    return pl.pallas_call(
        _attn_a_sample_kernel,
        grid=(nb, nhb),
        in_specs=[
            pl.BlockSpec((t, hw), lambda b, h: (b, h)),
            cache_spec, cache_spec, new_spec, new_spec,
            pl.BlockSpec((A_HEADS_PER_STEP, t, cl + NEW_KEY_PAD), lambda b, h: (h, 0, 0)),
        ],
        out_specs=pl.BlockSpec((t, hw), lambda b, h: (b, h)),
        out_shape=jax.ShapeDtypeStruct((nb * t, d_model), BF16),
        compiler_params=pltpu.CompilerParams(
            dimension_semantics=("parallel", "parallel"),
            vmem_limit_bytes=VMEM_LIMIT_BYTES),
        name="attn_a_sample",
    )(qkv, cache_k, cache_v, k_new, v_new, bias)


class _QueryBlock(NamedTuple):
    q: Callable
    k_blocks: Sequence
    v_blocks: Sequence
    bias: Callable
    store: Callable


def _attn_b_heads(query_blocks, sink_ref, n_col_blocks):
    n_pairs = B_KV_HEADS // 2
    bpp = n_col_blocks // n_pairs
    items = [(qb, pair) for qb in query_blocks for pair in range(n_pairs)]

    def scores(qb, pair):
        pair_sl = slice(pair * LANES, (pair + 1) * LANES)
        k = jnp.concatenate([kb(pair_sl) for kb in qb.k_blocks], axis=0)
        low_k = lax.broadcasted_iota(jnp.int32, k.shape, 1) < B_HEAD_DIM
        zero = jnp.zeros_like(k)
        k_diag = jnp.concatenate([jnp.where(low_k, k, zero), jnp.where(low_k, zero, k)], axis=0)
        q_stack = jnp.concatenate([qb.q(pair * bpp + c) for c in range(bpp)], axis=0)
        return _dot_nt(q_stack, k_diag) + qb.bias(pair)

    def softmax_pv(qb, pair, s):
        pair_sl = slice(pair * LANES, (pair + 1) * LANES)
        v = jnp.concatenate([vb(pair_sl) for vb in qb.v_blocks], axis=0)
        keys = v.shape[0]
        m_even = s[:, :keys].max(axis=-1, keepdims=True)
        m_odd = s[:, keys:].max(axis=-1, keepdims=True)
        p = jnp.concatenate(
            [jnp.exp2(s[:, :keys] - m_even), jnp.exp2(s[:, keys:] - m_odd)], axis=1).astype(BF16)
        low_k = lax.broadcasted_iota(jnp.int32, v.shape, 1) < B_HEAD_DIM
        zero = jnp.zeros_like(v)
        ones_low = jnp.where(low_k, 1.0, 0.0).astype(BF16)
        ones_high = jnp.where(low_k, 0.0, 1.0).astype(BF16)
        v_diag = jnp.concatenate(
            [jnp.concatenate([jnp.where(low_k, v, zero), ones_low], axis=1),
             jnp.concatenate([jnp.where(low_k, zero, v), ones_high], axis=1)], axis=0)
        return m_even, m_odd, jnp.dot(p, v_diag, preferred_element_type=F32)

    def finish(qb, pair, m_even, m_odd, out):
        tq = out.shape[0] // bpp
        low = lax.broadcasted_iota(jnp.int32, (tq, LANES), 1) < B_HEAD_DIM
        for c in range(bpp):
            rows = slice(c * tq, (c + 1) * tq)
            cb = pair * bpp + c
            sink_even = jnp.exp2(sink_ref[0, 2 * cb] - m_even[rows])
            sink_odd = jnp.exp2(sink_ref[0, 2 * cb + 1] - m_odd[rows])
            den = out[rows, LANES:] + jnp.where(low, sink_even, sink_odd)
            qb.store(cb, (out[rows, :LANES] / den).astype(BF16))

    s_vals, pv_vals = {}, {}
    for t in range(len(items) + 2):
        if t < len(items):
            s_vals[t] = scores(*items[t])
        if 0 <= t - 1 < len(items):
            pv_vals[t - 1] = softmax_pv(*items[t - 1], s_vals.pop(t - 1))
        if 0 <= t - 2 < len(items):
            finish(*items[t - 2], *pv_vals.pop(t - 2))


def _attn_b_prompt_kernel(q_ref, kp_ref, kc_ref, vp_ref, vc_ref, bias_first_ref, bias_ref,
                          sink_ref, o_ref):
    n_sub = q_ref.shape[0] // B_TQ

    def query_block(j):
        rows = slice(j * B_TQ, (j + 1) * B_TQ)
        prev = slice((j - 1) * B_TQ, j * B_TQ)

        def store(cb, value):
            o_ref[rows, cb * LANES:(cb + 1) * LANES] = value

        if j == 0:
            k_blocks = [lambda sl: kp_ref[:, sl], lambda sl: kc_ref[rows, sl]]
            v_blocks = [lambda sl: vp_ref[:, sl], lambda sl: vc_ref[rows, sl]]
            bias = bias_first_ref
        else:
            k_blocks = [lambda sl: kc_ref[prev, sl], lambda sl: kc_ref[rows, sl]]
            v_blocks = [lambda sl: vc_ref[prev, sl], lambda sl: vc_ref[rows, sl]]
            bias = bias_ref
        return _QueryBlock(
            q=lambda cb: q_ref[rows, cb * LANES:(cb + 1) * LANES],
            k_blocks=k_blocks, v_blocks=v_blocks,
            bias=lambda pair: bias[0, pair], store=store)

    _attn_b_heads([query_block(j) for j in range(n_sub)], sink_ref, q_ref.shape[1] // LANES)


def _attn_b_prompt(qkv, bias, sinks, *, batch, seq, d_model, kv_width):
    rows = B_ROWS_PER_STEP
    n_steps = seq // rows
    sub_per_step = rows // B_TQ
    nq_cols = d_model // kv_width

    def cur_spec(col):
        return pl.BlockSpec((rows, kv_width), lambda b, i: (b * n_steps + i, col))

    def prev_spec(col):
        return pl.BlockSpec(
            (B_TQ, kv_width),
            lambda b, i: ((b * n_steps + i) * sub_per_step - jnp.where(i == 0, 0, 1), col))

    bias_block = (1,) + bias.shape[1:]
    return pl.pallas_call(
        _attn_b_prompt_kernel,
        grid=(batch, n_steps),
        in_specs=[
            pl.BlockSpec((rows, d_model), lambda b, i: (b * n_steps + i, 0)),
            prev_spec(nq_cols), cur_spec(nq_cols),
            prev_spec(nq_cols + 1), cur_spec(nq_cols + 1),
            pl.BlockSpec(bias_block, lambda b, i: (jnp.where(i == 0, 1, 0), 0, 0, 0)),
            pl.BlockSpec(bias_block, lambda b, i: (0, 0, 0, 0)),
            pl.BlockSpec(memory_space=pltpu.SMEM),
        ],
        out_specs=pl.BlockSpec((rows, d_model), lambda b, i: (b * n_steps + i, 0)),
        out_shape=jax.ShapeDtypeStruct((batch * seq, d_model), BF16),
        compiler_params=pltpu.CompilerParams(
            dimension_semantics=("parallel", "parallel"),
            vmem_limit_bytes=VMEM_LIMIT_BYTES),
        name="attn_b_prompt",
    )(qkv, qkv, qkv, qkv, qkv, bias, bias, sinks)


def _attn_b_sample_kernel(q_ref, kc_ref, vc_ref, kn_ref, vn_ref, bias_ref, sink_ref,
                          o_ref, ko_ref, vo_ref):
    t = q_ref.shape[0]
    cl = kc_ref.shape[1]
    kc, vc, kn, vn = kc_ref[0], vc_ref[0], kn_ref[0], vn_ref[0]
    ko_ref[0, :cl - t, :] = kc[t:, :]
    ko_ref[0, cl - t:, :] = kn
    vo_ref[0, :cl - t, :] = vc[t:, :]
    vo_ref[0, cl - t:, :] = vn
    pad = jnp.zeros((NEW_KEY_PAD - t, LANES), BF16)

    def store(cb, value):
        o_ref[:, cb * LANES:(cb + 1) * LANES] = value

    block = _QueryBlock(
        q=lambda cb: q_ref[:, cb * LANES:(cb + 1) * LANES],
        k_blocks=[lambda sl: kc[:, sl].astype(BF16),
                  lambda sl: jnp.concatenate([kn[:, sl].astype(BF16), pad], axis=0)],
        v_blocks=[lambda sl: vc[:, sl].astype(BF16),
                  lambda sl: jnp.concatenate([vn[:, sl].astype(BF16), pad], axis=0)],
        bias=lambda pair: bias_ref[0, pair], store=store)
    _attn_b_heads([block], sink_ref, q_ref.shape[1] // LANES)


def _attn_b_sample(qkv, k_new, v_new, cache_k, cache_v, bias, sinks, *, d_model):
    nb, cl, kv_width = cache_k.shape
    t = k_new.shape[1]
    cache_spec = pl.BlockSpec((1, cl, kv_width), lambda b: (b, 0, 0))
    new_spec = pl.BlockSpec((1, t, kv_width), lambda b: (b, 0, 0))
    return pl.pallas_call(
        _attn_b_sample_kernel,
        grid=(nb,),
        in_specs=[
            pl.BlockSpec((t, d_model), lambda b: (b, 0)),
            cache_spec, cache_spec, new_spec, new_spec,
            pl.BlockSpec(bias.shape, lambda b: (0, 0, 0, 0)),
            pl.BlockSpec(memory_space=pltpu.SMEM),
        ],
        out_specs=[pl.BlockSpec((t, d_model), lambda b: (b, 0)), cache_spec, cache_spec],
        out_shape=[
            jax.ShapeDtypeStruct((nb * t, d_model), BF16),
            jax.ShapeDtypeStruct(cache_k.shape, F32),
            jax.ShapeDtypeStruct(cache_v.shape, F32),
        ],
        compiler_params=pltpu.CompilerParams(
            dimension_semantics=("parallel",),
            vmem_limit_bytes=VMEM_LIMIT_BYTES),
        name="attn_b_sample",
    )(qkv, cache_k, cache_v, k_new, v_new, bias, sinks)


def _oproj_mlp_kernel(x_ref, a_ref, wo_ref, g_ref, wup_hbm, wdn_hbm, gfin_ref, o_ref,
                      h_ref, r_ref, wup_buf, wdn_buf, sem, *, layer, tf, final_norm):
    m = pl.program_id(0)
    n_f = wup_hbm.shape[2] // tf

    def weight_copies(f, slot):
        return (
            pltpu.make_async_copy(wup_hbm.at[layer, :, pl.ds(f * tf, tf)], wup_buf.at[slot],
                                  sem.at[0, slot]),
            pltpu.make_async_copy(wdn_hbm.at[layer, pl.ds(f * tf, tf), :], wdn_buf.at[slot],
                                  sem.at[1, slot]))

    def start(f, slot):
        for copy in weight_copies(f, slot):
            copy.start()

    def wait(f, slot):
        for copy in weight_copies(f, slot):
            copy.wait()

    n_slots = wup_buf.shape[0]
    ahead = n_slots - 1

    @pl.when(m == 0)
    def _():
        for f in range(ahead):
            start(f, f)

    x1 = x_ref[...] + jnp.dot(a_ref[...], wo_ref[...], preferred_element_type=F32)
    o_ref[...] = x1
    h_ref[...] = (x1 * g_ref[...]).astype(BF16)
    r = lax.rsqrt(jnp.mean(x1 * x1, axis=-1, keepdims=True) + RMS_EPS)
    r_ref[...] = jnp.broadcast_to(r, r_ref.shape)

    for f in range(n_f):
        slot = f % n_slots
        nxt = f + ahead
        if nxt < n_f:
            start(nxt, nxt % n_slots)
        else:
            @pl.when(m + 1 < pl.num_programs(0))
            def _():
                start(nxt - n_f, nxt - n_f)
        wait(f, slot)
        u = _scale_rows(jnp.dot(h_ref[...], wup_buf[slot], preferred_element_type=F32), r_ref[...])
        act = jnp.square(jnp.maximum(u, 0.0)).astype(BF16)
        o_ref[...] += jnp.dot(act, wdn_buf[slot], preferred_element_type=F32)

    if final_norm:
        o_ref[...] = _rms_scale(o_ref[...]) * gfin_ref[...]


def _oproj_mlp(x, attn, wo, slot, g, wup, wdn, layer, gfin, *, tm, tf, final_norm):
    rows, d = x.shape
    d_ff = wup.shape[2]
    assert d_ff % tf == 0 and (d_ff // tf) % MLP_WEIGHT_SLOTS == 0
    kernel = functools.partial(_oproj_mlp_kernel, layer=layer, tf=tf, final_norm=final_norm)
    return pl.pallas_call(
        kernel,
        grid=(rows // tm,),
        in_specs=[
            pl.BlockSpec((tm, d), lambda m: (m, 0)),
            pl.BlockSpec((tm, d), lambda m: (m, 0)),
            pl.BlockSpec((None, d, d), lambda m: (slot, 0, 0), pipeline_mode=pl.Buffered(1)),
            pl.BlockSpec((1, d), lambda m: (0, 0)),
            pl.BlockSpec(memory_space=pl.ANY),
            pl.BlockSpec(memory_space=pl.ANY),
            pl.BlockSpec((1, d), lambda m: (0, 0)),
        ],
        out_specs=pl.BlockSpec((tm, d), lambda m: (m, 0)),
        out_shape=jax.ShapeDtypeStruct((rows, d), F32),
        scratch_shapes=[
            pltpu.VMEM((tm, d), BF16), pltpu.VMEM((tm, LANES), F32),
            pltpu.VMEM((MLP_WEIGHT_SLOTS, d, tf), BF16), pltpu.VMEM((MLP_WEIGHT_SLOTS, tf, d), BF16),
            pltpu.SemaphoreType.DMA((2, MLP_WEIGHT_SLOTS)),
        ],
        compiler_params=pltpu.CompilerParams(
            dimension_semantics=("arbitrary",),
            vmem_limit_bytes=VMEM_LIMIT_BYTES),
        name="oproj_mlp",
    )(x, attn, wo, g, wup, wdn, gfin)


def _band_mask(n_q_chunks, n_k_chunks, n_prev):
    ci = np.arange(n_q_chunks * CHUNK)[:, None] // CHUNK
    cj = np.arange(n_k_chunks * CHUNK)[None, :] // CHUNK
    return (cj >= ci) & (cj <= ci + n_prev)


def _rel_bias_tile(table, n_rows, n_cols, key_offset):
    length = n_rows + n_cols
    rel = np.arange(length) - (n_rows - 1) - key_offset
    diag = table[:, np.clip(rel, -A_REL_CLIP, A_REL_CLIP) + A_REL_CLIP]
    flat = jnp.tile(diag, (1, n_rows))[:, :n_rows * (length - 1)]
    return flat.reshape(table.shape[0], n_rows, length - 1)[:, :, n_rows - 1:n_rows - 1 + n_cols]


def _a_prompt_diag(table):
    lane = np.arange(4 * A_TQ)
    u = np.where(lane < 3 * A_TQ, lane, lane - 4 * A_TQ)
    return table[:, np.clip(u - 2 * A_TQ, -A_REL_CLIP, A_REL_CLIP) + A_REL_CLIP]


def _a_sample_bias(table, t, cl):
    bias = _rel_bias_tile(table, t, cl + NEW_KEY_PAD, cl)
    valid = np.broadcast_to(np.arange(cl + NEW_KEY_PAD)[None, :] < cl + t, (t, cl + NEW_KEY_PAD))
    return jnp.where(valid[None], bias, NEG_INF)


def _pair_heads(x, axis):
    n_heads = x.shape[axis]
    group = n_heads // B_KV_HEADS
    split = x.shape[:axis] + (B_KV_HEADS // 2, 2, group) + x.shape[axis + 1:]
    return jnp.swapaxes(x.reshape(split), axis + 1, axis + 2).reshape(x.shape)


def _alibi_tile(n_q_heads, n_rows, n_cols, key_offset, valid):
    group = n_q_heads // B_KV_HEADS
    slopes = (2.0 ** (-8.0 * np.arange(1, n_q_heads + 1) / n_q_heads)).astype(np.float32)
    qi = np.arange(n_rows)[:, None]
    kj = np.arange(n_cols)[None, :]
    rel = np.abs(kj - key_offset - qi).astype(np.float32)
    alibi = (-slopes[:, None, None] * rel[None]).astype(np.float64)
    bias = np.where(valid[None], alibi * LOG2E, NEG_INF)
    stacked = [[np.concatenate([bias[group * (2 * pair) + g], bias[group * (2 * pair + 1) + g]], axis=1)
                for g in range(group)] for pair in range(B_KV_HEADS // 2)]
    return np.asarray(stacked, np.float32).reshape(B_KV_HEADS // 2, group * n_rows, 2 * n_cols)


def _b_prompt_bias(n_q_heads):
    mask = _band_mask(B_TQ // CHUNK, 2 * B_TQ // CHUNK, B_PREV_CHUNKS)
    first = mask & (np.arange(2 * B_TQ)[None, :] >= B_TQ)
    return np.stack([_alibi_tile(n_q_heads, B_TQ, 2 * B_TQ, B_TQ, mask),
                     _alibi_tile(n_q_heads, B_TQ, 2 * B_TQ, B_TQ, first)])


def _b_sample_bias(n_q_heads, t, cl):
    valid = np.broadcast_to(np.arange(cl + NEW_KEY_PAD)[None, :] < cl + t, (t, cl + NEW_KEY_PAD))
    return _alibi_tile(n_q_heads, t, cl + NEW_KEY_PAD, cl, valid)[None]


def kernel(x_prompt, x_sample, cache_a_k, cache_a_v, cache_b_k, cache_b_v, norm_mix, norm_ffn,
           norm_final, a_w_qkv, a_w_o, a_rel_bias, b_w_qkv, b_w_o, b_sinks, w_up, w_down):
    batch, seq, d = x_prompt.shape
    dec_batch, dec_seq, _ = x_sample.shape
    depth = norm_mix.shape[0]
    a_kv_width = cache_a_k.shape[3] * cache_a_k.shape[4]
    b_kv_width = cache_b_k.shape[3] * cache_b_k.shape[4]
    cl_a, cl_b = cache_a_k.shape[2], cache_b_k.shape[2]
    n_b_heads = d // B_HEAD_DIM
    tail_a = min(A_PREV_CHUNKS * CHUNK, seq)
    tail_b = min(B_PREV_CHUNKS * CHUNK, seq)

    tm = 512
    tm_qkv = 1024
    tf = 1024
    rows_s = dec_batch * dec_seq

    n_slots = b_w_qkv.shape[0]
    a_w_qkv16 = a_w_qkv.astype(BF16)
    a_w_o16 = a_w_o.astype(BF16)
    b_q16 = _pair_heads(b_w_qkv[:, :, :d].astype(BF16).reshape(n_slots, d, n_b_heads, B_HEAD_DIM), 2)
    b_w_qkv16 = jnp.concatenate(
        [b_q16.reshape(n_slots, d, d), b_w_qkv[:, :, d:].astype(BF16)], axis=2)
    b_w_o16 = _pair_heads(
        b_w_o.astype(BF16).reshape(n_slots, n_b_heads, B_HEAD_DIM, d), 1).reshape(n_slots, d, d)
    w_up16 = w_up.astype(BF16)
    w_down16 = w_down.astype(BF16)
    sinks_perm = _pair_heads(b_sinks, 1) * LOG2E
    a_tables = a_rel_bias * LOG2E
    cache_a_k16 = cache_a_k.astype(BF16).reshape(cache_a_k.shape[:3] + (a_kv_width,))
    cache_a_v16 = cache_a_v.astype(BF16).reshape(cache_a_v.shape[:3] + (a_kv_width,))

    b_bias_prompt = jnp.asarray(_b_prompt_bias(n_b_heads))
    b_bias_sample = jnp.asarray(_b_sample_bias(n_b_heads, dec_seq, cl_b))

    xp = x_prompt.reshape(batch * seq, d)
    xs = x_sample.reshape(rows_s, d)
    gfin = norm_final.reshape(1, d)

    states = {k: [] for k in ("a_kp", "a_vp", "a_ks", "a_vs", "b_kp", "b_vp", "b_ks", "b_vs")}
    for layer in range(depth):
        slot = layer // N_MIXERS
        g_mix = norm_mix[layer].reshape(1, d)
        g_ffn = norm_ffn[layer].reshape(1, d)
        if layer % N_MIXERS == 0:
            qkv_args = dict(tn=1024, q_width=d, kv_width=a_kv_width,
                            q_scale=A_HEAD_DIM ** -0.5 * LOG2E)
            qkv_p, kp, vp = _norm_qkv(xp, g_mix, a_w_qkv16, slot, tm=tm_qkv,
                                      tiles_per_batch=seq // tm_qkv, tail_rows=tail_a, **qkv_args)
            qkv_s, kn, vn = _norm_qkv(xs, g_mix, a_w_qkv16, slot, tm=rows_s,
                                      tiles_per_batch=1, tail_rows=rows_s, **qkv_args)
            table = a_tables[slot]
            mp = _attn_a_prompt(qkv_p, _a_prompt_diag(table), batch=batch, seq=seq, d_model=d)
            ms = _attn_a_sample(
                qkv_s, kn.reshape(dec_batch, dec_seq, a_kv_width),
                vn.reshape(dec_batch, dec_seq, a_kv_width),
                cache_a_k16, cache_a_v16, slot,
                _a_sample_bias(table, dec_seq, cl_a), d_model=d)
            w_o = a_w_o16
            states["a_kp"].append(kp.reshape((batch, tail_a) + cache_a_k.shape[3:]))
            states["a_vp"].append(vp.reshape((batch, tail_a) + cache_a_k.shape[3:]))
            states["a_ks"].append(kn.reshape((dec_batch, dec_seq) + cache_a_k.shape[3:]))
            states["a_vs"].append(vn.reshape((dec_batch, dec_seq) + cache_a_k.shape[3:]))
        else:
            qkv_args = dict(tn=2 * b_kv_width, q_width=d, kv_width=b_kv_width,
                            q_scale=B_HEAD_DIM ** -0.5 * LOG2E)
            qkv_p, kp, vp = _norm_qkv(xp, g_mix, b_w_qkv16, slot, tm=tm_qkv,
                                      tiles_per_batch=seq // tm_qkv, tail_rows=tail_b, **qkv_args)
            qkv_s, kn, vn = _norm_qkv(xs, g_mix, b_w_qkv16, slot, tm=rows_s,
                                      tiles_per_batch=1, tail_rows=rows_s, **qkv_args)
            sinks = sinks_perm[slot].reshape(1, n_b_heads)
            mp = _attn_b_prompt(qkv_p, b_bias_prompt, sinks, batch=batch, seq=seq,
                                d_model=d, kv_width=b_kv_width)
            ms, ks, vs = _attn_b_sample(
                qkv_s, kn.reshape(dec_batch, dec_seq, b_kv_width),
                vn.reshape(dec_batch, dec_seq, b_kv_width),
                cache_b_k[slot].reshape(dec_batch, cl_b, b_kv_width),
                cache_b_v[slot].reshape(dec_batch, cl_b, b_kv_width),
                b_bias_sample, sinks, d_model=d)
            w_o = b_w_o16
            states["b_kp"].append(kp.reshape((batch, tail_b) + cache_b_k.shape[3:]))
            states["b_vp"].append(vp.reshape((batch, tail_b) + cache_b_k.shape[3:]))
            states["b_ks"].append(ks.reshape(cache_b_k.shape[1:]))
            states["b_vs"].append(vs.reshape(cache_b_k.shape[1:]))
        last = layer == depth - 1
        xp = _oproj_mlp(xp, mp, w_o, slot, g_ffn, w_up16, w_down16, layer, gfin,
                        tm=tm, tf=tf, final_norm=last)
        xs = _oproj_mlp(xs, ms, w_o, slot, g_ffn, w_up16, w_down16, layer, gfin,
                        tm=rows_s, tf=tf, final_norm=last)

    def rolled(cache, new_rows):
        return jnp.concatenate([cache, jnp.stack(new_rows)], axis=2)[:, :, dec_seq:]

    return (xp.reshape(batch, seq, d), xs.reshape(dec_batch, dec_seq, d),
            jnp.stack(states["a_kp"]), jnp.stack(states["a_vp"]),
            jnp.stack(states["b_kp"]), jnp.stack(states["b_vp"]),
            rolled(cache_a_k, states["a_ks"]), rolled(cache_a_v, states["a_vs"]),
            jnp.stack(states["b_ks"]), jnp.stack(states["b_vs"]))
```

```python
import functools
from typing import Callable, NamedTuple, Sequence

import jax
import jax.numpy as jnp
import numpy as np
from jax import lax
from jax.experimental import pallas as pl
from jax.experimental.pallas import tpu as pltpu

F32 = jnp.float32
BF16 = jnp.bfloat16

CHUNK = 64
N_MIXERS = 2
A_HEAD_DIM = 128
A_PREV_CHUNKS = 8
A_REL_CLIP = 128
B_PREV_CHUNKS = 2
B_HEAD_DIM = 64
B_KV_HEADS = 8
RMS_EPS = 1e-6
NEG_INF = -1e30
LOG2E = float(np.log2(np.e))

LANES = 128
VMEM_LIMIT_BYTES = 56 * 1024 * 1024

A_TQ = 256
A_HEADS_PER_STEP = 16
B_TQ = 128
B_ROWS_PER_STEP = 1024
NEW_KEY_PAD = 128
MLP_WEIGHT_SLOTS = 2
MLP_ROW_TILE = 512
MLP_DFF_TILE = 1024
QKV_ROW_TILE = 1024
A_QKV_COL_TILE = 1024
X_DMA_PARTS = 4


def _rms_scale(x):
    return x * lax.rsqrt(jnp.mean(x * x, axis=-1, keepdims=True) + RMS_EPS)


def _scale_rows(acc, r):
    return jnp.concatenate(
        [acc[:, c * LANES:(c + 1) * LANES] * r for c in range(acc.shape[1] // LANES)], axis=1)


def _staggered_row_specs(tm, width, n_parts, n_m, first_step):
    assert first_step >= 1 and width % n_parts == 0

    def spec(j):
        return pl.BlockSpec(
            (tm, width // n_parts),
            lambda m, s: (jnp.minimum(m + (s >= first_step + j), n_m - 1), j))

    return [spec(j) for j in range(n_parts)]


def _dot_nt(a, b):
    return lax.dot_general(a, b, (((1,), (1,)), ((), ())), preferred_element_type=F32)


def _softmax_pv(s, v):
    m = s.max(axis=-1, keepdims=True)
    p = jnp.exp2(s - m).astype(BF16)
    out = jnp.dot(p, jnp.concatenate([v, jnp.ones_like(v)], axis=1), preferred_element_type=F32)
    width = v.shape[1]
    return out[:, :width] / out[:, width:]


def _norm_qkv_kernel(*refs, n_x, nq, nk, q_scale, tiles_per_batch, tail_rows):
    x_refs = refs[:n_x]
    g_ref, w_ref, qkv_ref, kst_ref, vst_ref, h_ref, r_ref = refs[n_x:]
    m = pl.program_id(0)
    n = pl.program_id(1)

    @pl.when(n == 0)
    def _():
        width = x_refs[0].shape[1]
        sum_sq = None
        for j, x_ref in enumerate(x_refs):
            cols = slice(j * width, (j + 1) * width)
            x = x_ref[...]
            h_ref[:, cols] = (x * g_ref[:, cols]).astype(BF16)
            part = jnp.sum(x * x, axis=-1, keepdims=True)
            sum_sq = part if sum_sq is None else sum_sq + part
        r = lax.rsqrt(sum_sq / h_ref.shape[1] + RMS_EPS)
        r_ref[...] = jnp.broadcast_to(r, r_ref.shape)

    acc = jnp.dot(h_ref[...], w_ref[...], preferred_element_type=F32)
    r = r_ref[...]
    r_out = r * jnp.where(n < nq, q_scale, 1.0)
    tm, tn = acc.shape
    for c in range(tn // LANES):
        cols = slice(c * LANES, (c + 1) * LANES)
        qkv_ref[:, cols] = (acc[:, cols] * r_out).astype(BF16)

    is_tail = (m % tiles_per_batch) == tiles_per_batch - 1

    def tail():
        return _scale_rows(acc[tm - tail_rows:, :], r[tm - tail_rows:, :])

    if nk == 0:
        @pl.when(is_tail & (n == nq))
        def _():
            kv_width = kst_ref.shape[2]
            kst_ref[0] = tail()[:, :kv_width]
            vst_ref[0] = tail()[:, kv_width:]
    else:
        @pl.when(is_tail & (n >= nq) & (n < nq + nk))
        def _():
            kst_ref[0] = tail()

        @pl.when(is_tail & (n >= nq + nk))
        def _():
            vst_ref[0] = tail()


def _norm_qkv(x, g, w, layer, *, tm, tn, q_width, kv_width, q_scale, tiles_per_batch, tail_rows):
    rows, d = x.shape
    n_total = w.shape[2]
    nq = q_width // tn
    nk = kv_width // tn
    assert (nk > 0 and kv_width % tn == 0) or tn == 2 * kv_width
    state_width = min(tn, kv_width)
    nb = rows // (tm * tiles_per_batch)

    def state_map(first):
        def index_map(m, n):
            tail = (m % tiles_per_batch) == tiles_per_batch - 1
            j = jnp.clip(n - first, 0, max(nk - 1, 0))
            return (m // tiles_per_batch, 0, jnp.where(tail, j, 0))
        return index_map

    n_m, n_n = rows // tm, n_total // tn
    n_x = min(X_DMA_PARTS, n_n - 1)
    kernel = functools.partial(_norm_qkv_kernel, n_x=n_x, nq=nq, nk=nk, q_scale=q_scale,
                               tiles_per_batch=tiles_per_batch, tail_rows=tail_rows)
    return pl.pallas_call(
        kernel,
        grid=(n_m, n_n),
        in_specs=_staggered_row_specs(tm, d, n_x, n_m, n_n - n_x) + [
            pl.BlockSpec((1, d), lambda m, n: (0, 0)),
            pl.BlockSpec((None, d, tn), lambda m, n: (layer, 0, n)),
        ],
        out_specs=[
            pl.BlockSpec((tm, tn), lambda m, n: (m, n)),
            pl.BlockSpec((1, tail_rows, state_width), state_map(nq)),
            pl.BlockSpec((1, tail_rows, state_width), state_map(nq + nk)),
        ],
        out_shape=[
            jax.ShapeDtypeStruct((rows, n_total), BF16),
            jax.ShapeDtypeStruct((nb, tail_rows, kv_width), F32),
            jax.ShapeDtypeStruct((nb, tail_rows, kv_width), F32),
        ],
        scratch_shapes=[pltpu.VMEM((tm, d), BF16), pltpu.VMEM((tm, LANES), F32)],
        compiler_params=pltpu.CompilerParams(
            dimension_semantics=("arbitrary", "arbitrary"),
            vmem_limit_bytes=VMEM_LIMIT_BYTES),
        name="norm_qkv",
    )(*([x] * n_x), g, w)


def _attn_a_prompt_kernel(q_ref, k0_ref, k1_ref, k2_ref, v0_ref, v1_ref, v2_ref,
                          diag_ref, band_ref, o_ref, bias_ref):
    k_refs = (k0_ref, k1_ref, k2_ref)
    v_refs = (v0_ref, v1_ref, v2_ref)
    n_heads = q_ref.shape[1] // A_HEAD_DIM
    tq = q_ref.shape[0]
    n_keys = len(k_refs) * tq
    b, i = pl.program_id(1), pl.program_id(2)

    @pl.when((b == 0) & (i == 0))
    def _():
        for h in range(n_heads):
            rows = jnp.broadcast_to(diag_ref[h:h + 1, :], (tq, diag_ref.shape[1]))
            toeplitz = pltpu.roll(rows, 0, 1, stride=1, stride_axis=0)
            bias_ref[h] = toeplitz[:, :n_keys] + band_ref[...]

    def head_slice(h):
        return slice(h * A_HEAD_DIM, (h + 1) * A_HEAD_DIM)

    def attend(start_mask):
        def scores(h):
            k = jnp.concatenate([r[:, head_slice(h)] for r in k_refs], axis=0)
            s = _dot_nt(q_ref[:, head_slice(h)], k) + bias_ref[h]
            return s if start_mask is None else s + start_mask

        s_next = scores(0)
        for h in range(n_heads):
            s = s_next
            if h + 1 < n_heads:
                s_next = scores(h + 1)
            v = jnp.concatenate([r[:, head_slice(h)] for r in v_refs], axis=0)
            o_ref[:, head_slice(h)] = _softmax_pv(s, v).astype(BF16)

    n_before = len(k_refs) - 1

    @pl.when(i >= n_before)
    def _():
        attend(None)

    @pl.when(i < n_before)
    def _():
        col = lax.broadcasted_iota(jnp.int32, (1, n_keys), 1)
        attend(jnp.where(col < (n_before - i) * tq, NEG_INF, 0.0))


def _attn_a_prompt(qkv, diag, *, batch, seq, d_model):
    tq = A_TQ
    hw = A_HEADS_PER_STEP * A_HEAD_DIM
    nhb = d_model // hw
    nqb = seq // tq
    band = np.where(_band_mask(tq // CHUNK, 3 * tq // CHUNK, A_PREV_CHUNKS), 0.0, NEG_INF)

    def kv_spec(back, col0):
        return pl.BlockSpec(
            (tq, hw), lambda h, b, i: (b * nqb + jnp.maximum(i - back, 0), col0 + h))

    return pl.pallas_call(
        _attn_a_prompt_kernel,
        grid=(nhb, batch, nqb),
        in_specs=[
            pl.BlockSpec((tq, hw), lambda h, b, i: (b * nqb + i, h)),
            kv_spec(2, nhb), kv_spec(1, nhb), kv_spec(0, nhb),
            kv_spec(2, 2 * nhb), kv_spec(1, 2 * nhb), kv_spec(0, 2 * nhb),
            pl.BlockSpec((A_HEADS_PER_STEP, diag.shape[1]), lambda h, b, i: (h, 0)),
            pl.BlockSpec(band.shape, lambda h, b, i: (0, 0)),
        ],
        out_specs=pl.BlockSpec((tq, hw), lambda h, b, i: (b * nqb + i, h)),
        out_shape=jax.ShapeDtypeStruct((batch * seq, d_model), BF16),
        scratch_shapes=[pltpu.VMEM((A_HEADS_PER_STEP, tq, 3 * tq), F32)],
        compiler_params=pltpu.CompilerParams(
            dimension_semantics=("arbitrary", "arbitrary", "arbitrary"),
            vmem_limit_bytes=VMEM_LIMIT_BYTES),
        name="attn_a_prompt",
    )(qkv, qkv, qkv, qkv, qkv, qkv, qkv, diag, jnp.asarray(band, F32))


def _attn_a_sample_kernel(q_ref, kc_ref, vc_ref, kn_ref, vn_ref, bias_ref, o_ref):
    t = q_ref.shape[0]
    kn, vn = kn_ref[0], vn_ref[0]
    pad = jnp.zeros((NEW_KEY_PAD - t, A_HEAD_DIM), BF16)
    for h in range(q_ref.shape[1] // A_HEAD_DIM):
        sl = slice(h * A_HEAD_DIM, (h + 1) * A_HEAD_DIM)
        k = jnp.concatenate([kc_ref[0, :, sl], kn[:, sl].astype(BF16), pad], axis=0)
        v = jnp.concatenate([vc_ref[0, :, sl], vn[:, sl].astype(BF16), pad], axis=0)
        s = _dot_nt(q_ref[:, sl], k) + bias_ref[h]
        o_ref[:, sl] = _softmax_pv(s, v).astype(BF16)


def _attn_a_sample(qkv, k_new, v_new, cache_k, cache_v, slot, bias, *, d_model):
    _, nb, cl, _ = cache_k.shape
    t = k_new.shape[1]
    hw = A_HEADS_PER_STEP * A_HEAD_DIM
    nhb = d_model // hw
    cache_spec = pl.BlockSpec((None, 1, cl, hw), lambda b, h: (slot, b, 0, h))
    new_spec = pl.BlockSpec((1, t, hw), lambda b, h: (b, 0, h))
    return pl.pallas_call(
        _attn_a_sample_kernel,
        grid=(nb, nhb),
        in_specs=[
            pl.BlockSpec((t, hw), lambda b, h: (b, h)),
            cache_spec, cache_spec, new_spec, new_spec,
            pl.BlockSpec((A_HEADS_PER_STEP, t, cl + NEW_KEY_PAD), lambda b, h: (h, 0, 0)),
        ],
        out_specs=pl.BlockSpec((t, hw), lambda b, h: (b, h)),
        out_shape=jax.ShapeDtypeStruct((nb * t, d_model), BF16),
        compiler_params=pltpu.CompilerParams(
            dimension_semantics=("parallel", "parallel"),
            vmem_limit_bytes=VMEM_LIMIT_BYTES),
        name="attn_a_sample",
    )(qkv, cache_k, cache_v, k_new, v_new, bias)


class _QueryBlock(NamedTuple):
    q: Callable
    k_blocks: Sequence
    v_blocks: Sequence
    bias: Callable
    store: Callable


def _attn_b_heads(query_blocks, sink_ref, n_col_blocks):
    n_pairs = B_KV_HEADS // 2
    bpp = n_col_blocks // n_pairs
    items = [(qb, pair) for qb in query_blocks for pair in range(n_pairs)]

    def scores(qb, pair):
        pair_sl = slice(pair * LANES, (pair + 1) * LANES)
        k = jnp.concatenate([kb(pair_sl) for kb in qb.k_blocks], axis=0)
        low_k = lax.broadcasted_iota(jnp.int32, k.shape, 1) < B_HEAD_DIM
        zero = jnp.zeros_like(k)
        k_diag = jnp.concatenate([jnp.where(low_k, k, zero), jnp.where(low_k, zero, k)], axis=0)
        q_stack = jnp.concatenate([qb.q(pair * bpp + c) for c in range(bpp)], axis=0)
        return _dot_nt(q_stack, k_diag) + qb.bias(pair)

    def softmax_pv(qb, pair, s):
        pair_sl = slice(pair * LANES, (pair + 1) * LANES)
        v = jnp.concatenate([vb(pair_sl) for vb in qb.v_blocks], axis=0)
        keys = v.shape[0]
        m_even = s[:, :keys].max(axis=-1, keepdims=True)
        m_odd = s[:, keys:].max(axis=-1, keepdims=True)
        p = jnp.concatenate(
            [jnp.exp2(s[:, :keys] - m_even), jnp.exp2(s[:, keys:] - m_odd)], axis=1).astype(BF16)
        low_k = lax.broadcasted_iota(jnp.int32, v.shape, 1) < B_HEAD_DIM
        zero = jnp.zeros_like(v)
        ones_low = jnp.where(low_k, 1.0, 0.0).astype(BF16)
        ones_high = jnp.where(low_k, 0.0, 1.0).astype(BF16)
        v_diag = jnp.concatenate(
            [jnp.concatenate([jnp.where(low_k, v, zero), ones_low], axis=1),
             jnp.concatenate([jnp.where(low_k, zero, v), ones_high], axis=1)], axis=0)
        return m_even, m_odd, jnp.dot(p, v_diag, preferred_element_type=F32)

    def finish(qb, pair, m_even, m_odd, out):
        tq = out.shape[0] // bpp
        low = lax.broadcasted_iota(jnp.int32, (tq, LANES), 1) < B_HEAD_DIM
        for c in range(bpp):
            rows = slice(c * tq, (c + 1) * tq)
            cb = pair * bpp + c
            sink_even = jnp.exp2(sink_ref[0, 2 * cb] - m_even[rows])
            sink_odd = jnp.exp2(sink_ref[0, 2 * cb + 1] - m_odd[rows])
            den = out[rows, LANES:] + jnp.where(low, sink_even, sink_odd)
            qb.store(cb, (out[rows, :LANES] / den).astype(BF16))

    s_vals, pv_vals = {}, {}
    for t in range(len(items) + 2):
        if t < len(items):
            s_vals[t] = scores(*items[t])
        if 0 <= t - 1 < len(items):
            pv_vals[t - 1] = softmax_pv(*items[t - 1], s_vals.pop(t - 1))
        if 0 <= t - 2 < len(items):
            finish(*items[t - 2], *pv_vals.pop(t - 2))


def _attn_b_prompt_kernel(q_ref, kp_ref, kc_ref, vp_ref, vc_ref, bias_first_ref, bias_ref,
                          sink_ref, o_ref):
    n_sub = q_ref.shape[0] // B_TQ

    def query_block(j):
        rows = slice(j * B_TQ, (j + 1) * B_TQ)
        prev = slice((j - 1) * B_TQ, j * B_TQ)

        def store(cb, value):
            o_ref[rows, cb * LANES:(cb + 1) * LANES] = value

        if j == 0:
            k_blocks = [lambda sl: kp_ref[:, sl], lambda sl: kc_ref[rows, sl]]
            v_blocks = [lambda sl: vp_ref[:, sl], lambda sl: vc_ref[rows, sl]]
            bias = bias_first_ref
        else:
            k_blocks = [lambda sl: kc_ref[prev, sl], lambda sl: kc_ref[rows, sl]]
            v_blocks = [lambda sl: vc_ref[prev, sl], lambda sl: vc_ref[rows, sl]]
            bias = bias_ref
        return _QueryBlock(
            q=lambda cb: q_ref[rows, cb * LANES:(cb + 1) * LANES],
            k_blocks=k_blocks, v_blocks=v_blocks,
            bias=lambda pair: bias[0, pair], store=store)

    _attn_b_heads([query_block(j) for j in range(n_sub)], sink_ref, q_ref.shape[1] // LANES)


def _attn_b_prompt(qkv, bias, sinks, *, batch, seq, d_model, kv_width):
    rows = B_ROWS_PER_STEP
    n_steps = seq // rows
    sub_per_step = rows // B_TQ
    nq_cols = d_model // kv_width

    def cur_spec(col):
        return pl.BlockSpec((rows, kv_width), lambda b, i: (b * n_steps + i, col))

    def prev_spec(col):
        return pl.BlockSpec(
            (B_TQ, kv_width),
            lambda b, i: ((b * n_steps + i) * sub_per_step - jnp.where(i == 0, 0, 1), col))

    bias_block = (1,) + bias.shape[1:]
    return pl.pallas_call(
        _attn_b_prompt_kernel,
        grid=(batch, n_steps),
        in_specs=[
            pl.BlockSpec((rows, d_model), lambda b, i: (b * n_steps + i, 0)),
            prev_spec(nq_cols), cur_spec(nq_cols),
            prev_spec(nq_cols + 1), cur_spec(nq_cols + 1),
            pl.BlockSpec(bias_block, lambda b, i: (jnp.where(i == 0, 1, 0), 0, 0, 0)),
            pl.BlockSpec(bias_block, lambda b, i: (0, 0, 0, 0)),
            pl.BlockSpec(memory_space=pltpu.SMEM),
        ],
        out_specs=pl.BlockSpec((rows, d_model), lambda b, i: (b * n_steps + i, 0)),
        out_shape=jax.ShapeDtypeStruct((batch * seq, d_model), BF16),
        compiler_params=pltpu.CompilerParams(
            dimension_semantics=("parallel", "parallel"),
            vmem_limit_bytes=VMEM_LIMIT_BYTES),
        name="attn_b_prompt",
    )(qkv, qkv, qkv, qkv, qkv, bias, bias, sinks)


def _attn_b_sample_kernel(q_ref, kc_ref, vc_ref, kn_ref, vn_ref, bias_ref, sink_ref,
                          o_ref, ko_ref, vo_ref):
    t = q_ref.shape[0]
    cl = kc_ref.shape[1]
    kc, vc, kn, vn = kc_ref[0], vc_ref[0], kn_ref[0], vn_ref[0]
    ko_ref[0, :cl - t, :] = kc[t:, :]
    ko_ref[0, cl - t:, :] = kn
    vo_ref[0, :cl - t, :] = vc[t:, :]
    vo_ref[0, cl - t:, :] = vn
    pad = jnp.zeros((NEW_KEY_PAD - t, LANES), BF16)

    def store(cb, value):
        o_ref[:, cb * LANES:(cb + 1) * LANES] = value

    block = _QueryBlock(
        q=lambda cb: q_ref[:, cb * LANES:(cb + 1) * LANES],
        k_blocks=[lambda sl: kc[:, sl].astype(BF16),
                  lambda sl: jnp.concatenate([kn[:, sl].astype(BF16), pad], axis=0)],
        v_blocks=[lambda sl: vc[:, sl].astype(BF16),
                  lambda sl: jnp.concatenate([vn[:, sl].astype(BF16), pad], axis=0)],
        bias=lambda pair: bias_ref[0, pair], store=store)
    _attn_b_heads([block], sink_ref, q_ref.shape[1] // LANES)


def _attn_b_sample(qkv, k_new, v_new, cache_k, cache_v, bias, sinks, *, d_model):
    nb, cl, kv_width = cache_k.shape
    t = k_new.shape[1]
    cache_spec = pl.BlockSpec((1, cl, kv_width), lambda b: (b, 0, 0))
    new_spec = pl.BlockSpec((1, t, kv_width), lambda b: (b, 0, 0))
    return pl.pallas_call(
        _attn_b_sample_kernel,
        grid=(nb,),
        in_specs=[
            pl.BlockSpec((t, d_model), lambda b: (b, 0)),
            cache_spec, cache_spec, new_spec, new_spec,
            pl.BlockSpec(bias.shape, lambda b: (0, 0, 0, 0)),
            pl.BlockSpec(memory_space=pltpu.SMEM),
        ],
        out_specs=[pl.BlockSpec((t, d_model), lambda b: (b, 0)), cache_spec, cache_spec],
        out_shape=[
            jax.ShapeDtypeStruct((nb * t, d_model), BF16),
            jax.ShapeDtypeStruct(cache_k.shape, F32),
            jax.ShapeDtypeStruct(cache_v.shape, F32),
        ],
        compiler_params=pltpu.CompilerParams(
            dimension_semantics=("parallel",),
            vmem_limit_bytes=VMEM_LIMIT_BYTES),
        name="attn_b_sample",
    )(qkv, cache_k, cache_v, k_new, v_new, bias, sinks)


def _oproj_mlp_kernel(x_ref, a_ref, wo_ref, g_ref, wup_hbm, wdn_hbm, gfin_ref, o_ref,
                      h_ref, r_ref, wup_buf, wdn_buf, sem, *, layer, tf, final_norm):
    m = pl.program_id(0)
    n_f = wup_hbm.shape[2] // tf

    def weight_copies(f, slot):
        return (
            pltpu.make_async_copy(wup_hbm.at[layer, :, pl.ds(f * tf, tf)], wup_buf.at[slot],
                                  sem.at[0, slot]),
            pltpu.make_async_copy(wdn_hbm.at[layer, pl.ds(f * tf, tf), :], wdn_buf.at[slot],
                                  sem.at[1, slot]))

    def start(f, slot):
        for copy in weight_copies(f, slot):
            copy.start()

    def wait(f, slot):
        for copy in weight_copies(f, slot):
            copy.wait()

    n_slots = wup_buf.shape[0]
    ahead = n_slots - 1

    @pl.when(m == 0)
    def _():
        for f in range(ahead):
            start(f, f)

    x1 = x_ref[...] + jnp.dot(a_ref[...], wo_ref[...], preferred_element_type=F32)
    o_ref[...] = x1
    h_ref[...] = (x1 * g_ref[...]).astype(BF16)
    r = lax.rsqrt(jnp.mean(x1 * x1, axis=-1, keepdims=True) + RMS_EPS)
    r_ref[...] = jnp.broadcast_to(r, r_ref.shape)

    for f in range(n_f):
        slot = f % n_slots
        nxt = f + ahead
        if nxt < n_f:
            start(nxt, nxt % n_slots)
        else:
            @pl.when(m + 1 < pl.num_programs(0))
            def _():
                start(nxt - n_f, nxt - n_f)
        wait(f, slot)
        u = _scale_rows(jnp.dot(h_ref[...], wup_buf[slot], preferred_element_type=F32), r_ref[...])
        act = jnp.square(jnp.maximum(u, 0.0)).astype(BF16)
        o_ref[...] += jnp.dot(act, wdn_buf[slot], preferred_element_type=F32)

    if final_norm:
        o_ref[...] = _rms_scale(o_ref[...]) * gfin_ref[...]


def _oproj_mlp(x, attn, wo, slot, g, wup, wdn, layer, gfin, *, tm, tf, final_norm):
    rows, d = x.shape
    d_ff = wup.shape[2]
    assert d_ff % tf == 0 and (d_ff // tf) % MLP_WEIGHT_SLOTS == 0
    kernel = functools.partial(_oproj_mlp_kernel, layer=layer, tf=tf, final_norm=final_norm)
    return pl.pallas_call(
        kernel,
        grid=(rows // tm,),
        in_specs=[
            pl.BlockSpec((tm, d), lambda m: (m, 0)),
            pl.BlockSpec((tm, d), lambda m: (m, 0)),
            pl.BlockSpec((None, d, d), lambda m: (slot, 0, 0), pipeline_mode=pl.Buffered(1)),
            pl.BlockSpec((1, d), lambda m: (0, 0)),
            pl.BlockSpec(memory_space=pl.ANY),
            pl.BlockSpec(memory_space=pl.ANY),
            pl.BlockSpec((1, d), lambda m: (0, 0)),
        ],
        out_specs=pl.BlockSpec((tm, d), lambda m: (m, 0)),
        out_shape=jax.ShapeDtypeStruct((rows, d), F32),
        scratch_shapes=[
            pltpu.VMEM((tm, d), BF16), pltpu.VMEM((tm, LANES), F32),
            pltpu.VMEM((MLP_WEIGHT_SLOTS, d, tf), BF16), pltpu.VMEM((MLP_WEIGHT_SLOTS, tf, d), BF16),
            pltpu.SemaphoreType.DMA((2, MLP_WEIGHT_SLOTS)),
        ],
        compiler_params=pltpu.CompilerParams(
            dimension_semantics=("arbitrary",),
            vmem_limit_bytes=VMEM_LIMIT_BYTES),
        name="oproj_mlp",
    )(x, attn, wo, g, wup, wdn, gfin)


def _band_mask(n_q_chunks, n_k_chunks, n_prev):
    ci = np.arange(n_q_chunks * CHUNK)[:, None] // CHUNK
    cj = np.arange(n_k_chunks * CHUNK)[None, :] // CHUNK
    return (cj >= ci) & (cj <= ci + n_prev)


def _rel_bias_tile(table, n_rows, n_cols, key_offset):
    length = n_rows + n_cols
    rel = np.arange(length) - (n_rows - 1) - key_offset
    diag = table[:, np.clip(rel, -A_REL_CLIP, A_REL_CLIP) + A_REL_CLIP]
    flat = jnp.tile(diag, (1, n_rows))[:, :n_rows * (length - 1)]
    return flat.reshape(table.shape[0], n_rows, length - 1)[:, :, n_rows - 1:n_rows - 1 + n_cols]


def _a_prompt_diag(table):
    lane = np.arange(4 * A_TQ)
    u = np.where(lane < 3 * A_TQ, lane, lane - 4 * A_TQ)
    return table[:, np.clip(u - 2 * A_TQ, -A_REL_CLIP, A_REL_CLIP) + A_REL_CLIP]


def _a_sample_bias(table, t, cl):
    bias = _rel_bias_tile(table, t, cl + NEW_KEY_PAD, cl)
    valid = np.broadcast_to(np.arange(cl + NEW_KEY_PAD)[None, :] < cl + t, (t, cl + NEW_KEY_PAD))
    return jnp.where(valid[None], bias, NEG_INF)


def _pair_heads(x, axis):
    n_heads = x.shape[axis]
    group = n_heads // B_KV_HEADS
    split = x.shape[:axis] + (B_KV_HEADS // 2, 2, group) + x.shape[axis + 1:]
    return jnp.swapaxes(x.reshape(split), axis + 1, axis + 2).reshape(x.shape)


def _alibi_tile(n_q_heads, n_rows, n_cols, key_offset, valid):
    group = n_q_heads // B_KV_HEADS
    slopes = (2.0 ** (-8.0 * np.arange(1, n_q_heads + 1) / n_q_heads)).astype(np.float32)
    qi = np.arange(n_rows)[:, None]
    kj = np.arange(n_cols)[None, :]
    rel = np.abs(kj - key_offset - qi).astype(np.float32)
    alibi = (-slopes[:, None, None] * rel[None]).astype(np.float64)
    bias = np.where(valid[None], alibi * LOG2E, NEG_INF)
    stacked = [[np.concatenate([bias[group * (2 * pair) + g], bias[group * (2 * pair + 1) + g]], axis=1)
                for g in range(group)] for pair in range(B_KV_HEADS // 2)]
    return np.asarray(stacked, np.float32).reshape(B_KV_HEADS // 2, group * n_rows, 2 * n_cols)


def _b_prompt_bias(n_q_heads):
    mask = _band_mask(B_TQ // CHUNK, 2 * B_TQ // CHUNK, B_PREV_CHUNKS)
    first = mask & (np.arange(2 * B_TQ)[None, :] >= B_TQ)
    return np.stack([_alibi_tile(n_q_heads, B_TQ, 2 * B_TQ, B_TQ, mask),
                     _alibi_tile(n_q_heads, B_TQ, 2 * B_TQ, B_TQ, first)])


def _b_sample_bias(n_q_heads, t, cl):
    valid = np.broadcast_to(np.arange(cl + NEW_KEY_PAD)[None, :] < cl + t, (t, cl + NEW_KEY_PAD))
    return _alibi_tile(n_q_heads, t, cl + NEW_KEY_PAD, cl, valid)[None]


def kernel(x_prompt, x_sample, cache_a_k, cache_a_v, cache_b_k, cache_b_v, norm_mix, norm_ffn,
           norm_final, a_w_qkv, a_w_o, a_rel_bias, b_w_qkv, b_w_o, b_sinks, w_up, w_down):
    batch, seq, d = x_prompt.shape
    dec_batch, dec_seq, _ = x_sample.shape
    depth = norm_mix.shape[0]
    a_kv_width = cache_a_k.shape[3] * cache_a_k.shape[4]
    b_kv_width = cache_b_k.shape[3] * cache_b_k.shape[4]
    cl_a, cl_b = cache_a_k.shape[2], cache_b_k.shape[2]
    n_b_heads = d // B_HEAD_DIM
    tail_a = min(A_PREV_CHUNKS * CHUNK, seq)
    tail_b = min(B_PREV_CHUNKS * CHUNK, seq)

    tm, tm_qkv, tf = MLP_ROW_TILE, QKV_ROW_TILE, MLP_DFF_TILE
    rows_s = dec_batch * dec_seq

    n_slots = b_w_qkv.shape[0]
    a_w_qkv16 = a_w_qkv.astype(BF16)
    a_w_o16 = a_w_o.astype(BF16)
    b_q16 = _pair_heads(b_w_qkv[:, :, :d].astype(BF16).reshape(n_slots, d, n_b_heads, B_HEAD_DIM), 2)
    b_w_qkv16 = jnp.concatenate(
        [b_q16.reshape(n_slots, d, d), b_w_qkv[:, :, d:].astype(BF16)], axis=2)
    b_w_o16 = _pair_heads(
        b_w_o.astype(BF16).reshape(n_slots, n_b_heads, B_HEAD_DIM, d), 1).reshape(n_slots, d, d)
    w_up16 = w_up.astype(BF16)
    w_down16 = w_down.astype(BF16)
    sinks_perm = _pair_heads(b_sinks, 1) * LOG2E
    a_tables = a_rel_bias * LOG2E
    cache_a_k16 = cache_a_k.astype(BF16).reshape(cache_a_k.shape[:3] + (a_kv_width,))
    cache_a_v16 = cache_a_v.astype(BF16).reshape(cache_a_v.shape[:3] + (a_kv_width,))

    b_bias_prompt = jnp.asarray(_b_prompt_bias(n_b_heads))
    b_bias_sample = jnp.asarray(_b_sample_bias(n_b_heads, dec_seq, cl_b))

    xp = x_prompt.reshape(batch * seq, d)
    xs = x_sample.reshape(rows_s, d)
    gfin = norm_final.reshape(1, d)

    states = {k: [] for k in ("a_kp", "a_vp", "a_ks", "a_vs", "b_kp", "b_vp", "b_ks", "b_vs")}
    for layer in range(depth):
        slot = layer // N_MIXERS
        g_mix = norm_mix[layer].reshape(1, d)
        g_ffn = norm_ffn[layer].reshape(1, d)
        if layer % N_MIXERS == 0:
            qkv_args = dict(tn=A_QKV_COL_TILE, q_width=d, kv_width=a_kv_width,
                            q_scale=A_HEAD_DIM ** -0.5 * LOG2E)
            qkv_p, kp, vp = _norm_qkv(xp, g_mix, a_w_qkv16, slot, tm=tm_qkv,
                                      tiles_per_batch=seq // tm_qkv, tail_rows=tail_a, **qkv_args)
            qkv_s, kn, vn = _norm_qkv(xs, g_mix, a_w_qkv16, slot, tm=rows_s,
                                      tiles_per_batch=1, tail_rows=rows_s, **qkv_args)
            table = a_tables[slot]
            mp = _attn_a_prompt(qkv_p, _a_prompt_diag(table), batch=batch, seq=seq, d_model=d)
            ms = _attn_a_sample(
                qkv_s, kn.reshape(dec_batch, dec_seq, a_kv_width),
                vn.reshape(dec_batch, dec_seq, a_kv_width),
                cache_a_k16, cache_a_v16, slot,
                _a_sample_bias(table, dec_seq, cl_a), d_model=d)
            w_o = a_w_o16
            states["a_kp"].append(kp.reshape((batch, tail_a) + cache_a_k.shape[3:]))
            states["a_vp"].append(vp.reshape((batch, tail_a) + cache_a_k.shape[3:]))
            states["a_ks"].append(kn.reshape((dec_batch, dec_seq) + cache_a_k.shape[3:]))
            states["a_vs"].append(vn.reshape((dec_batch, dec_seq) + cache_a_k.shape[3:]))
        else:
            qkv_args = dict(tn=2 * b_kv_width, q_width=d, kv_width=b_kv_width,
                            q_scale=B_HEAD_DIM ** -0.5 * LOG2E)
            qkv_p, kp, vp = _norm_qkv(xp, g_mix, b_w_qkv16, slot, tm=tm_qkv,
                                      tiles_per_batch=seq // tm_qkv, tail_rows=tail_b, **qkv_args)
            qkv_s, kn, vn = _norm_qkv(xs, g_mix, b_w_qkv16, slot, tm=rows_s,
                                      tiles_per_batch=1, tail_rows=rows_s, **qkv_args)
            sinks = sinks_perm[slot].reshape(1, n_b_heads)
            mp = _attn_b_prompt(qkv_p, b_bias_prompt, sinks, batch=batch, seq=seq,
                                d_model=d, kv_width=b_kv_width)
            ms, ks, vs = _attn_b_sample(
                qkv_s, kn.reshape(dec_batch, dec_seq, b_kv_width),
                vn.reshape(dec_batch, dec_seq, b_kv_width),
                cache_b_k[slot].reshape(dec_batch, cl_b, b_kv_width),
                cache_b_v[slot].reshape(dec_batch, cl_b, b_kv_width),
                b_bias_sample, sinks, d_model=d)
            w_o = b_w_o16
            states["b_kp"].append(kp.reshape((batch, tail_b) + cache_b_k.shape[3:]))
            states["b_vp"].append(vp.reshape((batch, tail_b) + cache_b_k.shape[3:]))
            states["b_ks"].append(ks.reshape(cache_b_k.shape[1:]))
            states["b_vs"].append(vs.reshape(cache_b_k.shape[1:]))
        last = layer == depth - 1
        xp = _oproj_mlp(xp, mp, w_o, slot, g_ffn, w_up16, w_down16, layer, gfin,
                        tm=tm, tf=tf, final_norm=last)
        xs = _oproj_mlp(xs, ms, w_o, slot, g_ffn, w_up16, w_down16, layer, gfin,
                        tm=rows_s, tf=tf, final_norm=last)

    def rolled(cache, new_rows):
        return jnp.concatenate([cache, jnp.stack(new_rows)], axis=2)[:, :, dec_seq:]

    return (xp.reshape(batch, seq, d), xs.reshape(dec_batch, dec_seq, d),
            jnp.stack(states["a_kp"]), jnp.stack(states["a_vp"]),
            jnp.stack(states["b_kp"]), jnp.stack(states["b_vp"]),
            rolled(cache_a_k, states["a_ks"]), rolled(cache_a_v, states["a_vs"]),
            jnp.stack(states["b_ks"]), jnp.stack(states["b_vs"]))
```

```python
import functools
from typing import Callable, NamedTuple, Sequence

import jax
import jax.numpy as jnp
import numpy as np
from jax import lax
from jax.experimental import pallas as pl
from jax.experimental.pallas import tpu as pltpu

F32 = jnp.float32
BF16 = jnp.bfloat16

CHUNK = 64
N_MIXERS = 2
A_HEAD_DIM = 128
A_PREV_CHUNKS = 8
A_REL_CLIP = 128
B_PREV_CHUNKS = 2
B_HEAD_DIM = 64
B_KV_HEADS = 8
RMS_EPS = 1e-6
NEG_INF = -1e30
LOG2E = float(np.log2(np.e))

LANES = 128
VMEM_LIMIT_BYTES = 56 * 1024 * 1024

A_TQ = 256
A_TQ_KEY_BLOCKS = 3
A_ROWS_PER_STEP = 512
A_HEADS_PER_STEP = 16
B_TQ = 128
B_ROWS_PER_STEP = 1024
NEW_KEY_PAD = 128
MLP_WEIGHT_SLOTS = 2
MLP_ROW_TILE = 512
MLP_DFF_TILE = 1024
QKV_ROW_TILE = 1024
A_QKV_COL_TILE = 1024
X_DMA_PARTS = 4


def _rms_scale(x):
    return x * lax.rsqrt(jnp.mean(x * x, axis=-1, keepdims=True) + RMS_EPS)


def _scale_rows(acc, r):
    return jnp.concatenate(
        [acc[:, c * LANES:(c + 1) * LANES] * r for c in range(acc.shape[1] // LANES)], axis=1)


def _staggered_row_specs(tm, width, n_parts, n_m, first_step):
    assert first_step >= 1 and width % n_parts == 0

    def spec(j):
        return pl.BlockSpec(
            (tm, width // n_parts),
            lambda m, s: (jnp.minimum(m + (s >= first_step + j), n_m - 1), j))

    return [spec(j) for j in range(n_parts)]


def _dot_nt(a, b):
    return lax.dot_general(a, b, (((1,), (1,)), ((), ())), preferred_element_type=F32)


def _softmax_pv(s, v):
    m = s.max(axis=-1, keepdims=True)
    p = jnp.exp2(s - m).astype(BF16)
    out = jnp.dot(p, jnp.concatenate([v, jnp.ones_like(v)], axis=1), preferred_element_type=F32)
    width = v.shape[1]
    return out[:, :width] / out[:, width:]


def _norm_qkv_kernel(*refs, n_x, nq, nk, q_scale, tiles_per_batch, tail_rows):
    x_refs = refs[:n_x]
    g_ref, w_ref, qkv_ref, kst_ref, vst_ref, h_ref, r_ref = refs[n_x:]
    m = pl.program_id(0)
    n = pl.program_id(1)

    @pl.when(n == 0)
    def _():
        width = x_refs[0].shape[1]
        sum_sq = None
        for j, x_ref in enumerate(x_refs):
            cols = slice(j * width, (j + 1) * width)
            x = x_ref[...]
            h_ref[:, cols] = (x * g_ref[:, cols]).astype(BF16)
            part = jnp.sum(x * x, axis=-1, keepdims=True)
            sum_sq = part if sum_sq is None else sum_sq + part
        r = lax.rsqrt(sum_sq / h_ref.shape[1] + RMS_EPS)
        r_ref[...] = jnp.broadcast_to(r, r_ref.shape)

    acc = jnp.dot(h_ref[...], w_ref[...], preferred_element_type=F32)
    r = r_ref[...]
    r_out = r * jnp.where(n < nq, q_scale, 1.0)
    tm, tn = acc.shape
    for c in range(tn // LANES):
        cols = slice(c * LANES, (c + 1) * LANES)
        qkv_ref[:, cols] = (acc[:, cols] * r_out).astype(BF16)

    is_tail = (m % tiles_per_batch) == tiles_per_batch - 1

    def tail():
        return _scale_rows(acc[tm - tail_rows:, :], r[tm - tail_rows:, :])

    if nk == 0:
        @pl.when(is_tail & (n == nq))
        def _():
            kv_width = kst_ref.shape[2]
            kst_ref[0] = tail()[:, :kv_width]
            vst_ref[0] = tail()[:, kv_width:]
    else:
        @pl.when(is_tail & (n >= nq) & (n < nq + nk))
        def _():
            kst_ref[0] = tail()

        @pl.when(is_tail & (n >= nq + nk))
        def _():
            vst_ref[0] = tail()


def _norm_qkv(x, g, w, layer, *, tm, tn, q_width, kv_width, q_scale, tiles_per_batch, tail_rows):
    rows, d = x.shape
    n_total = w.shape[2]
    nq = q_width // tn
    nk = kv_width // tn
    assert (nk > 0 and kv_width % tn == 0) or tn == 2 * kv_width
    state_width = min(tn, kv_width)
    nb = rows // (tm * tiles_per_batch)

    def state_map(first):
        def index_map(m, n):
            tail = (m % tiles_per_batch) == tiles_per_batch - 1
            j = jnp.clip(n - first, 0, max(nk - 1, 0))
            return (m // tiles_per_batch, 0, jnp.where(tail, j, 0))
        return index_map

    n_m, n_n = rows // tm, n_total // tn
    n_x = min(X_DMA_PARTS, n_n - 1)
    kernel = functools.partial(_norm_qkv_kernel, n_x=n_x, nq=nq, nk=nk, q_scale=q_scale,
                               tiles_per_batch=tiles_per_batch, tail_rows=tail_rows)
    return pl.pallas_call(
        kernel,
        grid=(n_m, n_n),
        in_specs=_staggered_row_specs(tm, d, n_x, n_m, n_n - n_x) + [
            pl.BlockSpec((1, d), lambda m, n: (0, 0)),
            pl.BlockSpec((None, d, tn), lambda m, n: (layer, 0, n)),
        ],
        out_specs=[
            pl.BlockSpec((tm, tn), lambda m, n: (m, n)),
            pl.BlockSpec((1, tail_rows, state_width), state_map(nq)),
            pl.BlockSpec((1, tail_rows, state_width), state_map(nq + nk)),
        ],
        out_shape=[
            jax.ShapeDtypeStruct((rows, n_total), BF16),
            jax.ShapeDtypeStruct((nb, tail_rows, kv_width), F32),
            jax.ShapeDtypeStruct((nb, tail_rows, kv_width), F32),
        ],
        scratch_shapes=[pltpu.VMEM((tm, d), BF16), pltpu.VMEM((tm, LANES), F32)],
        compiler_params=pltpu.CompilerParams(
            dimension_semantics=("arbitrary", "arbitrary"),
            vmem_limit_bytes=VMEM_LIMIT_BYTES),
        name="norm_qkv",
    )(*([x] * n_x), g, w)


def _attn_a_prompt_kernel(q_ref, kp_ref, kc_ref, vp_ref, vc_ref, diag_ref, band_ref,
                          o_ref, bias_ref):
    n_heads = q_ref.shape[1] // A_HEAD_DIM
    tq = A_TQ
    n_sub = q_ref.shape[0] // tq
    n_before = A_TQ_KEY_BLOCKS - 1
    n_keys = A_TQ_KEY_BLOCKS * tq
    b, i = pl.program_id(1), pl.program_id(2)

    @pl.when((b == 0) & (i == 0))
    def _():
        for h in range(n_heads):
            rows = jnp.broadcast_to(diag_ref[h:h + 1, :], (tq, diag_ref.shape[1]))
            toeplitz = pltpu.roll(rows, 0, 1, stride=1, stride_axis=0)
            bias_ref[h] = toeplitz[:, :n_keys] + band_ref[...]

    def head_slice(h):
        return slice(h * A_HEAD_DIM, (h + 1) * A_HEAD_DIM)

    def key_blocks(prev_ref, cur_ref, j, h):
        blocks = []
        for blk in range(j - n_before, j + 1):
            ref, at = (cur_ref, blk) if blk >= 0 else (prev_ref, n_sub + blk)
            blocks.append(ref[at * tq:(at + 1) * tq, head_slice(h)])
        return jnp.concatenate(blocks, axis=0)

    def attend(start_masks):
        items = [(j, h) for j in range(n_sub) for h in range(n_heads)]

        def scores(j, h):
            q = q_ref[j * tq:(j + 1) * tq, head_slice(h)]
            s = _dot_nt(q, key_blocks(kp_ref, kc_ref, j, h)) + bias_ref[h]
            if start_masks is None or start_masks[j] is None:
                return s
            return s + start_masks[j]

        s_next = scores(*items[0])
        for t, (j, h) in enumerate(items):
            s = s_next
            if t + 1 < len(items):
                s_next = scores(*items[t + 1])
            out = _softmax_pv(s, key_blocks(vp_ref, vc_ref, j, h))
            o_ref[j * tq:(j + 1) * tq, head_slice(h)] = out.astype(BF16)

    @pl.when(i > 0)
    def _():
        attend(None)

    @pl.when(i == 0)
    def _():
        col = lax.broadcasted_iota(jnp.int32, (1, n_keys), 1)
        attend([jnp.where(col < (n_before - j) * tq, NEG_INF, 0.0) if j < n_before else None
                for j in range(n_sub)])


def _attn_a_prompt(qkv, diag, *, batch, seq, d_model):
    tq, rows = A_TQ, A_ROWS_PER_STEP
    hw = A_HEADS_PER_STEP * A_HEAD_DIM
    nhb = d_model // hw
    n_steps = seq // rows
    assert rows % tq == 0 and rows >= (A_TQ_KEY_BLOCKS - 1) * tq
    band = np.where(_band_mask(tq // CHUNK, A_TQ_KEY_BLOCKS * tq // CHUNK, A_PREV_CHUNKS),
                    0.0, NEG_INF)

    def row_spec(back, col0):
        return pl.BlockSpec(
            (rows, hw), lambda h, b, i: (b * n_steps + jnp.maximum(i - back, 0), col0 + h))

    return pl.pallas_call(
        _attn_a_prompt_kernel,
        grid=(nhb, batch, n_steps),
        in_specs=[
            row_spec(0, 0),
            row_spec(1, nhb), row_spec(0, nhb),
            row_spec(1, 2 * nhb), row_spec(0, 2 * nhb),
            pl.BlockSpec((A_HEADS_PER_STEP, diag.shape[1]), lambda h, b, i: (h, 0)),
            pl.BlockSpec(band.shape, lambda h, b, i: (0, 0)),
        ],
        out_specs=row_spec(0, 0),
        out_shape=jax.ShapeDtypeStruct((batch * seq, d_model), BF16),
        scratch_shapes=[pltpu.VMEM((A_HEADS_PER_STEP, tq, A_TQ_KEY_BLOCKS * tq), F32)],
        compiler_params=pltpu.CompilerParams(
            dimension_semantics=("arbitrary", "arbitrary", "arbitrary"),
            vmem_limit_bytes=VMEM_LIMIT_BYTES),
        name="attn_a_prompt",
    )(qkv, qkv, qkv, qkv, qkv, diag, jnp.asarray(band, F32))


def _attn_a_sample_kernel(q_ref, kc_ref, vc_ref, kn_ref, vn_ref, bias_ref, o_ref):
    t = q_ref.shape[0]
    kn, vn = kn_ref[0], vn_ref[0]
    pad = jnp.zeros((NEW_KEY_PAD - t, A_HEAD_DIM), BF16)
    for h in range(q_ref.shape[1] // A_HEAD_DIM):
        sl = slice(h * A_HEAD_DIM, (h + 1) * A_HEAD_DIM)
        k = jnp.concatenate([kc_ref[0, :, sl], kn[:, sl].astype(BF16), pad], axis=0)
        v = jnp.concatenate([vc_ref[0, :, sl], vn[:, sl].astype(BF16), pad], axis=0)
        s = _dot_nt(q_ref[:, sl], k) + bias_ref[h]
        o_ref[:, sl] = _softmax_pv(s, v).astype(BF16)


def _attn_a_sample(qkv, k_new, v_new, cache_k, cache_v, slot, bias, *, d_model):
    _, nb, cl, _ = cache_k.shape
    t = k_new.shape[1]
    hw = A_HEADS_PER_STEP * A_HEAD_DIM
    nhb = d_model // hw
    cache_spec = pl.BlockSpec((None, 1, cl, hw), lambda b, h: (slot, b, 0, h))
    new_spec = pl.BlockSpec((1, t, hw), lambda b, h: (b, 0, h))
    return pl.pallas_call(
        _attn_a_sample_kernel,
        grid=(nb, nhb),
        in_specs=[
            pl.BlockSpec((t, hw), lambda b, h: (b, h)),
            cache_spec, cache_spec, new_spec, new_spec,
            pl.BlockSpec((A_HEADS_PER_STEP, t, cl + NEW_KEY_PAD), lambda b, h: (h, 0, 0)),
        ],
        out_specs=pl.BlockSpec((t, hw), lambda b, h: (b, h)),
        out_shape=jax.ShapeDtypeStruct((nb * t, d_model), BF16),
        compiler_params=pltpu.CompilerParams(
            dimension_semantics=("parallel", "parallel"),
            vmem_limit_bytes=VMEM_LIMIT_BYTES),
        name="attn_a_sample",
    )(qkv, cache_k, cache_v, k_new, v_new, bias)


class _QueryBlock(NamedTuple):
    q: Callable
    k_blocks: Sequence
    v_blocks: Sequence
    bias: Callable
    store: Callable


def _attn_b_heads(query_blocks, sink_ref, n_col_blocks):
    n_pairs = B_KV_HEADS // 2
    bpp = n_col_blocks // n_pairs
    items = [(qb, pair) for qb in query_blocks for pair in range(n_pairs)]

    def scores(qb, pair):
        pair_sl = slice(pair * LANES, (pair + 1) * LANES)
        k = jnp.concatenate([kb(pair_sl) for kb in qb.k_blocks], axis=0)
        low_k = lax.broadcasted_iota(jnp.int32, k.shape, 1) < B_HEAD_DIM
        zero = jnp.zeros_like(k)
        k_diag = jnp.concatenate([jnp.where(low_k, k, zero), jnp.where(low_k, zero, k)], axis=0)
        q_stack = jnp.concatenate([qb.q(pair * bpp + c) for c in range(bpp)], axis=0)
        return _dot_nt(q_stack, k_diag) + qb.bias(pair)

    def softmax_pv(qb, pair, s):
        pair_sl = slice(pair * LANES, (pair + 1) * LANES)
        v = jnp.concatenate([vb(pair_sl) for vb in qb.v_blocks], axis=0)
        keys = v.shape[0]
        m_even = s[:, :keys].max(axis=-1, keepdims=True)
        m_odd = s[:, keys:].max(axis=-1, keepdims=True)
        p = jnp.concatenate(
            [jnp.exp2(s[:, :keys] - m_even), jnp.exp2(s[:, keys:] - m_odd)], axis=1).astype(BF16)
        low_k = lax.broadcasted_iota(jnp.int32, v.shape, 1) < B_HEAD_DIM
        zero = jnp.zeros_like(v)
        ones_low = jnp.where(low_k, 1.0, 0.0).astype(BF16)
        ones_high = jnp.where(low_k, 0.0, 1.0).astype(BF16)
        v_diag = jnp.concatenate(
            [jnp.concatenate([jnp.where(low_k, v, zero), ones_low], axis=1),
             jnp.concatenate([jnp.where(low_k, zero, v), ones_high], axis=1)], axis=0)
        return m_even, m_odd, jnp.dot(p, v_diag, preferred_element_type=F32)

    def finish(qb, pair, m_even, m_odd, out):
        tq = out.shape[0] // bpp
        low = lax.broadcasted_iota(jnp.int32, (tq, LANES), 1) < B_HEAD_DIM
        for c in range(bpp):
            rows = slice(c * tq, (c + 1) * tq)
            cb = pair * bpp + c
            sink_even = jnp.exp2(sink_ref[0, 2 * cb] - m_even[rows])
            sink_odd = jnp.exp2(sink_ref[0, 2 * cb + 1] - m_odd[rows])
            den = out[rows, LANES:] + jnp.where(low, sink_even, sink_odd)
            qb.store(cb, (out[rows, :LANES] / den).astype(BF16))

    s_vals, pv_vals = {}, {}
    for t in range(len(items) + 2):
        if t < len(items):
            s_vals[t] = scores(*items[t])
        if 0 <= t - 1 < len(items):
            pv_vals[t - 1] = softmax_pv(*items[t - 1], s_vals.pop(t - 1))
        if 0 <= t - 2 < len(items):
            finish(*items[t - 2], *pv_vals.pop(t - 2))


def _attn_b_prompt_kernel(q_ref, kp_ref, kc_ref, vp_ref, vc_ref, bias_first_ref, bias_ref,
                          sink_ref, o_ref):
    n_sub = q_ref.shape[0] // B_TQ

    def query_block(j):
        rows = slice(j * B_TQ, (j + 1) * B_TQ)
        prev = slice((j - 1) * B_TQ, j * B_TQ)

        def store(cb, value):
            o_ref[rows, cb * LANES:(cb + 1) * LANES] = value

        if j == 0:
            k_blocks = [lambda sl: kp_ref[:, sl], lambda sl: kc_ref[rows, sl]]
            v_blocks = [lambda sl: vp_ref[:, sl], lambda sl: vc_ref[rows, sl]]
            bias = bias_first_ref
        else:
            k_blocks = [lambda sl: kc_ref[prev, sl], lambda sl: kc_ref[rows, sl]]
            v_blocks = [lambda sl: vc_ref[prev, sl], lambda sl: vc_ref[rows, sl]]
            bias = bias_ref
        return _QueryBlock(
            q=lambda cb: q_ref[rows, cb * LANES:(cb + 1) * LANES],
            k_blocks=k_blocks, v_blocks=v_blocks,
            bias=lambda pair: bias[0, pair], store=store)

    _attn_b_heads([query_block(j) for j in range(n_sub)], sink_ref, q_ref.shape[1] // LANES)


def _attn_b_prompt(qkv, bias, sinks, *, batch, seq, d_model, kv_width):
    rows = B_ROWS_PER_STEP
    n_steps = seq // rows
    sub_per_step = rows // B_TQ
    nq_cols = d_model // kv_width

    def cur_spec(col):
        return pl.BlockSpec((rows, kv_width), lambda b, i: (b * n_steps + i, col))

    def prev_spec(col):
        return pl.BlockSpec(
            (B_TQ, kv_width),
            lambda b, i: ((b * n_steps + i) * sub_per_step - jnp.where(i == 0, 0, 1), col))

    bias_block = (1,) + bias.shape[1:]
    return pl.pallas_call(
        _attn_b_prompt_kernel,
        grid=(batch, n_steps),
        in_specs=[
            pl.BlockSpec((rows, d_model), lambda b, i: (b * n_steps + i, 0)),
            prev_spec(nq_cols), cur_spec(nq_cols),
            prev_spec(nq_cols + 1), cur_spec(nq_cols + 1),
            pl.BlockSpec(bias_block, lambda b, i: (jnp.where(i == 0, 1, 0), 0, 0, 0)),
            pl.BlockSpec(bias_block, lambda b, i: (0, 0, 0, 0)),
            pl.BlockSpec(memory_space=pltpu.SMEM),
        ],
        out_specs=pl.BlockSpec((rows, d_model), lambda b, i: (b * n_steps + i, 0)),
        out_shape=jax.ShapeDtypeStruct((batch * seq, d_model), BF16),
        compiler_params=pltpu.CompilerParams(
            dimension_semantics=("parallel", "parallel"),
            vmem_limit_bytes=VMEM_LIMIT_BYTES),
        name="attn_b_prompt",
    )(qkv, qkv, qkv, qkv, qkv, bias, bias, sinks)


def _attn_b_sample_kernel(q_ref, kc_ref, vc_ref, kn_ref, vn_ref, bias_ref, sink_ref,
                          o_ref, ko_ref, vo_ref):
    t = q_ref.shape[0]
    cl = kc_ref.shape[1]
    kc, vc, kn, vn = kc_ref[0], vc_ref[0], kn_ref[0], vn_ref[0]
    ko_ref[0, :cl - t, :] = kc[t:, :]
    ko_ref[0, cl - t:, :] = kn
    vo_ref[0, :cl - t, :] = vc[t:, :]
    vo_ref[0, cl - t:, :] = vn
    pad = jnp.zeros((NEW_KEY_PAD - t, LANES), BF16)

    def store(cb, value):
        o_ref[:, cb * LANES:(cb + 1) * LANES] = value

    block = _QueryBlock(
        q=lambda cb: q_ref[:, cb * LANES:(cb + 1) * LANES],
        k_blocks=[lambda sl: kc[:, sl].astype(BF16),
                  lambda sl: jnp.concatenate([kn[:, sl].astype(BF16), pad], axis=0)],
        v_blocks=[lambda sl: vc[:, sl].astype(BF16),
                  lambda sl: jnp.concatenate([vn[:, sl].astype(BF16), pad], axis=0)],
        bias=lambda pair: bias_ref[0, pair], store=store)
    _attn_b_heads([block], sink_ref, q_ref.shape[1] // LANES)


def _attn_b_sample(qkv, k_new, v_new, cache_k, cache_v, bias, sinks, *, d_model):
    nb, cl, kv_width = cache_k.shape
    t = k_new.shape[1]
    cache_spec = pl.BlockSpec((1, cl, kv_width), lambda b: (b, 0, 0))
    new_spec = pl.BlockSpec((1, t, kv_width), lambda b: (b, 0, 0))
    return pl.pallas_call(
        _attn_b_sample_kernel,
        grid=(nb,),
        in_specs=[
            pl.BlockSpec((t, d_model), lambda b: (b, 0)),
            cache_spec, cache_spec, new_spec, new_spec,
            pl.BlockSpec(bias.shape, lambda b: (0, 0, 0, 0)),
            pl.BlockSpec(memory_space=pltpu.SMEM),
        ],
        out_specs=[pl.BlockSpec((t, d_model), lambda b: (b, 0)), cache_spec, cache_spec],
        out_shape=[
            jax.ShapeDtypeStruct((nb * t, d_model), BF16),
            jax.ShapeDtypeStruct(cache_k.shape, F32),
            jax.ShapeDtypeStruct(cache_v.shape, F32),
        ],
        compiler_params=pltpu.CompilerParams(
            dimension_semantics=("parallel",),
            vmem_limit_bytes=VMEM_LIMIT_BYTES),
        name="attn_b_sample",
    )(qkv, cache_k, cache_v, k_new, v_new, bias, sinks)


def _oproj_mlp_kernel(x_ref, a_ref, wo_ref, g_ref, wup_hbm, wdn_hbm, gfin_ref, o_ref,
                      h_ref, r_ref, wup_buf, wdn_buf, sem, *, layer, tf, final_norm):
    m = pl.program_id(0)
    n_f = wup_hbm.shape[2] // tf

    def weight_copies(f, slot):
        return (
            pltpu.make_async_copy(wup_hbm.at[layer, :, pl.ds(f * tf, tf)], wup_buf.at[slot],
                                  sem.at[0, slot]),
            pltpu.make_async_copy(wdn_hbm.at[layer, pl.ds(f * tf, tf), :], wdn_buf.at[slot],
                                  sem.at[1, slot]))

    def start(f, slot):
        for copy in weight_copies(f, slot):
            copy.start()

    def wait(f, slot):
        for copy in weight_copies(f, slot):
            copy.wait()

    n_slots = wup_buf.shape[0]
    ahead = n_slots - 1

    @pl.when(m == 0)
    def _():
        for f in range(ahead):
            start(f, f)

    x1 = x_ref[...] + jnp.dot(a_ref[...], wo_ref[...], preferred_element_type=F32)
    o_ref[...] = x1
    h_ref[...] = (x1 * g_ref[...]).astype(BF16)
    r = lax.rsqrt(jnp.mean(x1 * x1, axis=-1, keepdims=True) + RMS_EPS)
    r_ref[...] = jnp.broadcast_to(r, r_ref.shape)

    for f in range(n_f):
        slot = f % n_slots
        nxt = f + ahead
        if nxt < n_f:
            start(nxt, nxt % n_slots)
        else:
            @pl.when(m + 1 < pl.num_programs(0))
            def _():
                start(nxt - n_f, nxt - n_f)
        wait(f, slot)
        u = _scale_rows(jnp.dot(h_ref[...], wup_buf[slot], preferred_element_type=F32), r_ref[...])
        act = jnp.square(jnp.maximum(u, 0.0)).astype(BF16)
        o_ref[...] += jnp.dot(act, wdn_buf[slot], preferred_element_type=F32)

    if final_norm:
        o_ref[...] = _rms_scale(o_ref[...]) * gfin_ref[...]


def _oproj_mlp(x, attn, wo, slot, g, wup, wdn, layer, gfin, *, tm, tf, final_norm):
    rows, d = x.shape
    d_ff = wup.shape[2]
    assert d_ff % tf == 0 and (d_ff // tf) % MLP_WEIGHT_SLOTS == 0
    kernel = functools.partial(_oproj_mlp_kernel, layer=layer, tf=tf, final_norm=final_norm)
    return pl.pallas_call(
        kernel,
        grid=(rows // tm,),
        in_specs=[
            pl.BlockSpec((tm, d), lambda m: (m, 0)),
            pl.BlockSpec((tm, d), lambda m: (m, 0)),
            pl.BlockSpec((None, d, d), lambda m: (slot, 0, 0), pipeline_mode=pl.Buffered(1)),
            pl.BlockSpec((1, d), lambda m: (0, 0)),
            pl.BlockSpec(memory_space=pl.ANY),
            pl.BlockSpec(memory_space=pl.ANY),
            pl.BlockSpec((1, d), lambda m: (0, 0)),
        ],
        out_specs=pl.BlockSpec((tm, d), lambda m: (m, 0)),
        out_shape=jax.ShapeDtypeStruct((rows, d), F32),
        scratch_shapes=[
            pltpu.VMEM((tm, d), BF16), pltpu.VMEM((tm, LANES), F32),
            pltpu.VMEM((MLP_WEIGHT_SLOTS, d, tf), BF16), pltpu.VMEM((MLP_WEIGHT_SLOTS, tf, d), BF16),
            pltpu.SemaphoreType.DMA((2, MLP_WEIGHT_SLOTS)),
        ],
        compiler_params=pltpu.CompilerParams(
            dimension_semantics=("arbitrary",),
            vmem_limit_bytes=VMEM_LIMIT_BYTES),
        name="oproj_mlp",
    )(x, attn, wo, g, wup, wdn, gfin)


def _band_mask(n_q_chunks, n_k_chunks, n_prev):
    ci = np.arange(n_q_chunks * CHUNK)[:, None] // CHUNK
    cj = np.arange(n_k_chunks * CHUNK)[None, :] // CHUNK
    return (cj >= ci) & (cj <= ci + n_prev)


def _rel_bias_tile(table, n_rows, n_cols, key_offset):
    length = n_rows + n_cols
    rel = np.arange(length) - (n_rows - 1) - key_offset
    diag = table[:, np.clip(rel, -A_REL_CLIP, A_REL_CLIP) + A_REL_CLIP]
    flat = jnp.tile(diag, (1, n_rows))[:, :n_rows * (length - 1)]
    return flat.reshape(table.shape[0], n_rows, length - 1)[:, :, n_rows - 1:n_rows - 1 + n_cols]


def _a_prompt_diag(table):
    lane = np.arange(4 * A_TQ)
    u = np.where(lane < 3 * A_TQ, lane, lane - 4 * A_TQ)
    return table[:, np.clip(u - 2 * A_TQ, -A_REL_CLIP, A_REL_CLIP) + A_REL_CLIP]


def _a_sample_bias(table, t, cl):
    bias = _rel_bias_tile(table, t, cl + NEW_KEY_PAD, cl)
    valid = np.broadcast_to(np.arange(cl + NEW_KEY_PAD)[None, :] < cl + t, (t, cl + NEW_KEY_PAD))
    return jnp.where(valid[None], bias, NEG_INF)


def _pair_heads(x, axis):
    n_heads = x.shape[axis]
    group = n_heads // B_KV_HEADS
    split = x.shape[:axis] + (B_KV_HEADS // 2, 2, group) + x.shape[axis + 1:]
    return jnp.swapaxes(x.reshape(split), axis + 1, axis + 2).reshape(x.shape)


def _alibi_tile(n_q_heads, n_rows, n_cols, key_offset, valid):
    group = n_q_heads // B_KV_HEADS
    slopes = (2.0 ** (-8.0 * np.arange(1, n_q_heads + 1) / n_q_heads)).astype(np.float32)
    qi = np.arange(n_rows)[:, None]
    kj = np.arange(n_cols)[None, :]
    rel = np.abs(kj - key_offset - qi).astype(np.float32)
    alibi = (-slopes[:, None, None] * rel[None]).astype(np.float64)
    bias = np.where(valid[None], alibi * LOG2E, NEG_INF)
    stacked = [[np.concatenate([bias[group * (2 * pair) + g], bias[group * (2 * pair + 1) + g]], axis=1)
                for g in range(group)] for pair in range(B_KV_HEADS // 2)]
    return np.asarray(stacked, np.float32).reshape(B_KV_HEADS // 2, group * n_rows, 2 * n_cols)


def _b_prompt_bias(n_q_heads):
    mask = _band_mask(B_TQ // CHUNK, 2 * B_TQ // CHUNK, B_PREV_CHUNKS)
    first = mask & (np.arange(2 * B_TQ)[None, :] >= B_TQ)
    return np.stack([_alibi_tile(n_q_heads, B_TQ, 2 * B_TQ, B_TQ, mask),
                     _alibi_tile(n_q_heads, B_TQ, 2 * B_TQ, B_TQ, first)])


def _b_sample_bias(n_q_heads, t, cl):
    valid = np.broadcast_to(np.arange(cl + NEW_KEY_PAD)[None, :] < cl + t, (t, cl + NEW_KEY_PAD))
    return _alibi_tile(n_q_heads, t, cl + NEW_KEY_PAD, cl, valid)[None]


def kernel(x_prompt, x_sample, cache_a_k, cache_a_v, cache_b_k, cache_b_v, norm_mix, norm_ffn,
           norm_final, a_w_qkv, a_w_o, a_rel_bias, b_w_qkv, b_w_o, b_sinks, w_up, w_down):
    batch, seq, d = x_prompt.shape
    dec_batch, dec_seq, _ = x_sample.shape
    depth = norm_mix.shape[0]
    a_kv_width = cache_a_k.shape[3] * cache_a_k.shape[4]
    b_kv_width = cache_b_k.shape[3] * cache_b_k.shape[4]
    cl_a, cl_b = cache_a_k.shape[2], cache_b_k.shape[2]
    n_b_heads = d // B_HEAD_DIM
    tail_a = min(A_PREV_CHUNKS * CHUNK, seq)
    tail_b = min(B_PREV_CHUNKS * CHUNK, seq)

    tm, tm_qkv, tf = MLP_ROW_TILE, QKV_ROW_TILE, MLP_DFF_TILE
    rows_s = dec_batch * dec_seq

    n_slots = b_w_qkv.shape[0]
    a_w_qkv16 = a_w_qkv.astype(BF16)
    a_w_o16 = a_w_o.astype(BF16)
    b_q16 = _pair_heads(b_w_qkv[:, :, :d].astype(BF16).reshape(n_slots, d, n_b_heads, B_HEAD_DIM), 2)
    b_w_qkv16 = jnp.concatenate(
        [b_q16.reshape(n_slots, d, d), b_w_qkv[:, :, d:].astype(BF16)], axis=2)
    b_w_o16 = _pair_heads(
        b_w_o.astype(BF16).reshape(n_slots, n_b_heads, B_HEAD_DIM, d), 1).reshape(n_slots, d, d)
    w_up16 = w_up.astype(BF16)
    w_down16 = w_down.astype(BF16)
    sinks_perm = _pair_heads(b_sinks, 1) * LOG2E
    a_tables = a_rel_bias * LOG2E
    cache_a_k16 = cache_a_k.astype(BF16).reshape(cache_a_k.shape[:3] + (a_kv_width,))
    cache_a_v16 = cache_a_v.astype(BF16).reshape(cache_a_v.shape[:3] + (a_kv_width,))

    b_bias_prompt = jnp.asarray(_b_prompt_bias(n_b_heads))
    b_bias_sample = jnp.asarray(_b_sample_bias(n_b_heads, dec_seq, cl_b))

    xp = x_prompt.reshape(batch * seq, d)
    xs = x_sample.reshape(rows_s, d)
    gfin = norm_final.reshape(1, d)

    states = {k: [] for k in ("a_kp", "a_vp", "a_ks", "a_vs", "b_kp", "b_vp", "b_ks", "b_vs")}
    for layer in range(depth):
        slot = layer // N_MIXERS
        g_mix = norm_mix[layer].reshape(1, d)
        g_ffn = norm_ffn[layer].reshape(1, d)
        if layer % N_MIXERS == 0:
            qkv_args = dict(tn=A_QKV_COL_TILE, q_width=d, kv_width=a_kv_width,
                            q_scale=A_HEAD_DIM ** -0.5 * LOG2E)
            qkv_p, kp, vp = _norm_qkv(xp, g_mix, a_w_qkv16, slot, tm=tm_qkv,
                                      tiles_per_batch=seq // tm_qkv, tail_rows=tail_a, **qkv_args)
            qkv_s, kn, vn = _norm_qkv(xs, g_mix, a_w_qkv16, slot, tm=rows_s,
                                      tiles_per_batch=1, tail_rows=rows_s, **qkv_args)
            table = a_tables[slot]
            mp = _attn_a_prompt(qkv_p, _a_prompt_diag(table), batch=batch, seq=seq, d_model=d)
            ms = _attn_a_sample(
                qkv_s, kn.reshape(dec_batch, dec_seq, a_kv_width),
                vn.reshape(dec_batch, dec_seq, a_kv_width),
                cache_a_k16, cache_a_v16, slot,
                _a_sample_bias(table, dec_seq, cl_a), d_model=d)
            w_o = a_w_o16
            states["a_kp"].append(kp.reshape((batch, tail_a) + cache_a_k.shape[3:]))
            states["a_vp"].append(vp.reshape((batch, tail_a) + cache_a_k.shape[3:]))
            states["a_ks"].append(kn.reshape((dec_batch, dec_seq) + cache_a_k.shape[3:]))
            states["a_vs"].append(vn.reshape((dec_batch, dec_seq) + cache_a_k.shape[3:]))
        else:
            qkv_args = dict(tn=2 * b_kv_width, q_width=d, kv_width=b_kv_width,
                            q_scale=B_HEAD_DIM ** -0.5 * LOG2E)
            qkv_p, kp, vp = _norm_qkv(xp, g_mix, b_w_qkv16, slot, tm=tm_qkv,
                                      tiles_per_batch=seq // tm_qkv, tail_rows=tail_b, **qkv_args)
            qkv_s, kn, vn = _norm_qkv(xs, g_mix, b_w_qkv16, slot, tm=rows_s,
                                      tiles_per_batch=1, tail_rows=rows_s, **qkv_args)
            sinks = sinks_perm[slot].reshape(1, n_b_heads)
            mp = _attn_b_prompt(qkv_p, b_bias_prompt, sinks, batch=batch, seq=seq,
                                d_model=d, kv_width=b_kv_width)
            ms, ks, vs = _attn_b_sample(
                qkv_s, kn.reshape(dec_batch, dec_seq, b_kv_width),
                vn.reshape(dec_batch, dec_seq, b_kv_width),
                cache_b_k[slot].reshape(dec_batch, cl_b, b_kv_width),
                cache_b_v[slot].reshape(dec_batch, cl_b, b_kv_width),
                b_bias_sample, sinks, d_model=d)
            w_o = b_w_o16
            states["b_kp"].append(kp.reshape((batch, tail_b) + cache_b_k.shape[3:]))
            states["b_vp"].append(vp.reshape((batch, tail_b) + cache_b_k.shape[3:]))
            states["b_ks"].append(ks.reshape(cache_b_k.shape[1:]))
            states["b_vs"].append(vs.reshape(cache_b_k.shape[1:]))
        last = layer == depth - 1
        xp = _oproj_mlp(xp, mp, w_o, slot, g_ffn, w_up16, w_down16, layer, gfin,
                        tm=tm, tf=tf, final_norm=last)
        xs = _oproj_mlp(xs, ms, w_o, slot, g_ffn, w_up16, w_down16, layer, gfin,
                        tm=rows_s, tf=tf, final_norm=last)

    def rolled(cache, new_rows):
        return jnp.concatenate([cache, jnp.stack(new_rows)], axis=2)[:, :, dec_seq:]

    return (xp.reshape(batch, seq, d), xs.reshape(dec_batch, dec_seq, d),
            jnp.stack(states["a_kp"]), jnp.stack(states["a_vp"]),
            jnp.stack(states["b_kp"]), jnp.stack(states["b_vp"]),
            rolled(cache_a_k, states["a_ks"]), rolled(cache_a_v, states["a_vs"]),
            jnp.stack(states["b_ks"]), jnp.stack(states["b_vs"]))
```
